```python
import math
import jax, jax.numpy as jnp
from jax import lax
import numpy as np

D_MODEL = 4096
BATCH = 2
SEQ = 4096
DEPTH = 4
DEC_BATCH = 8
DEC_SEQ = 16
PAST_LEN = 2048

CHUNK = 64
QBLOCK = 128
ROPE_THETA = 10000.0
NORM_EPS = 1e-6
NEG_INF = -1e30

DIFF_HEADS = 8
DIFF_DH = 64
DIFF_DV = 2 * DIFF_DH
SB_HEADS = 8
SB_DH = 128
MLA_HEADS = 16
MLA_NOPE = 128
MLA_ROPE = 64
MLA_DV = 128
MLA_Q_LORA = 1024
MLA_KV_LORA = 512

DIFF_W = DIFF_HEADS * DIFF_DV
SB_W = SB_HEADS * SB_DH
MLA_W = MLA_HEADS * MLA_DV
N_BRANCHES = 3
IN_SPLITS = (DIFF_HEADS * 2 * DIFF_DH, DIFF_HEADS * 2 * DIFF_DH, DIFF_W, SB_W, SB_W, SB_W, MLA_Q_LORA, MLA_KV_LORA, MLA_ROPE)
IN_WIDTH = sum(IN_SPLITS)
FFN_HIDDEN = -(-8 * D_MODEL // 768) * 256

kernel_name = 'hybrid_stream_encoder_step'


def rmsnorm(x, g):
    xf = x.astype(jnp.float32)
    y = xf * lax.rsqrt(jnp.mean(xf * xf, axis=-1, keepdims=True) + NORM_EPS)
    return (y * g.astype(jnp.float32)).astype(x.dtype)


def rope(x, pos):
    d = x.shape[-1]
    inv_freq = ROPE_THETA ** (-jnp.arange(0, d, 2, dtype=jnp.float32) / d)
    ang = pos.astype(jnp.float32)[:, None] * inv_freq[None, :]
    cos, sin = jnp.cos(ang), jnp.sin(ang)
    xf = x.astype(jnp.float32)
    x1, x2 = xf[..., : d // 2], xf[..., d // 2:]
    return jnp.concatenate([x1 * cos - x2 * sin, x2 * cos + x1 * sin], axis=-1).astype(x.dtype)


def rope_pair(x, pos):
    return jnp.concatenate([rope(x[..., :DIFF_DH], pos), rope(x[..., DIFF_DH:], pos)], axis=-1)


def chunk_mask(q_pos, k_pos):
    return (k_pos // CHUNK)[None, :] <= (q_pos // CHUNK)[:, None]


def sweep_query_blocks(block_fn, q_parts, q_pos):
    t = q_pos.shape[0]
    if t <= QBLOCK:
        return block_fn(*q_parts, q_pos)
    nb = t // QBLOCK

    def to_blocks(z):
        return jnp.moveaxis(z.reshape(z.shape[:2] + (nb, QBLOCK) + z.shape[3:]), 2, 0)

    out = lax.map(lambda a: block_fn(*a[0], a[1]),
                  (tuple(to_blocks(z) for z in q_parts), q_pos.reshape(nb, QBLOCK)))
    out = jnp.moveaxis(out, 0, 2)
    return out.reshape(out.shape[:2] + (t,) + out.shape[4:])


def diff_attention(q, k, v, lam, q_pos, k_pos):
    scale = DIFF_DH ** -0.5
    k1, k2 = k[..., :DIFF_DH], k[..., DIFF_DH:]

    def block(q1, q2, qp):
        mask = chunk_mask(qp, k_pos)
        s1 = jnp.einsum('bhqd,bhkd->bhqk', q1, k1).astype(jnp.float32) * scale
        s2 = jnp.einsum('bhqd,bhkd->bhqk', q2, k2).astype(jnp.float32) * scale
        p = (jax.nn.softmax(jnp.where(mask, s1, NEG_INF), axis=-1)
             - lam * jax.nn.softmax(jnp.where(mask, s2, NEG_INF), axis=-1))
        return jnp.einsum('bhqk,bhkd->bhqd', p.astype(v.dtype), v)

    return sweep_query_blocks(block, (q[..., :DIFF_DH], q[..., DIFF_DH:]), q_pos)


def stick_breaking_attention(q, k, v, q_pos, k_pos):
    scale = SB_DH ** -0.5

    def block(qb, qp):
        mask = k_pos[None, :] < qp[:, None]
        z = jnp.einsum('bhqd,bhkd->bhqk', qb, k).astype(jnp.float32) * scale
        log_stay = jnp.where(mask, jax.nn.log_sigmoid(-z), 0.0)
        log_between = lax.cumsum(log_stay, axis=3, reverse=True) - log_stay
        a = jnp.where(mask, jnp.exp(jax.nn.log_sigmoid(z) + log_between), 0.0)
        return jnp.einsum('bhqk,bhkd->bhqd', a.astype(v.dtype), v)

    return sweep_query_blocks(block, (q,), q_pos)


def mla_attention(q_nope, q_rope, k_nope, k_rope, v, q_pos, k_pos):
    scale = (MLA_NOPE + MLA_ROPE) ** -0.5

    def block(qn, qr, qp):
        s = (jnp.einsum('bhqd,bhkd->bhqk', qn, k_nope)
             + jnp.einsum('bhqr,bkr->bhqk', qr, k_rope)).astype(jnp.float32) * scale
        p = jax.nn.softmax(jnp.where(chunk_mask(qp, k_pos), s, NEG_INF), axis=-1)
        return jnp.einsum('bhqk,bhkd->bhqd', p.astype(v.dtype), v)

    return sweep_query_blocks(block, (q_nope, q_rope), q_pos)


def run_trunk(x, past, w):
    b, t = x.shape[0], x.shape[1]
    p_len = 0 if past is None else past[0].shape[3]
    pos = p_len + jnp.arange(t, dtype=jnp.int32)
    k_pos = jnp.arange(p_len + t, dtype=jnp.int32)
    split_points = np.cumsum(IN_SPLITS)[:-1].tolist()

    def heads(z, n):
        return z.reshape(b, t, n, -1).transpose(0, 2, 1, 3)

    def merge_heads(z):
        return z.transpose(0, 2, 1, 3).reshape(b, t, -1)

    def with_past(new, idx, layer, axis):
        return new if past is None else jnp.concatenate([past[idx][layer], new], axis=axis)

    rows = ([], [], [], [], [], [])
    h = x
    for l in range(DEPTH):
        xn = rmsnorm(h, w['attn_norm'][l])
        qa, ka, va, qb, kb, vb, cq, ckv, kr = jnp.split(xn @ w['w_in'][l], split_points, axis=-1)
        qa = rope_pair(heads(qa, DIFF_HEADS), pos)
        ka = rope_pair(heads(ka, DIFF_HEADS), pos)
        va = heads(va, DIFF_HEADS)
        lam_init = 0.8 - 0.6 * math.exp(-0.3 * l)
        lv = w['diff_lambda'][l].astype(jnp.float32)
        lam = jnp.exp(jnp.sum(lv[0] * lv[1])) - jnp.exp(jnp.sum(lv[2] * lv[3])) + lam_init
        oa = diff_attention(qa, with_past(ka, 0, l, 2), with_past(va, 1, l, 2), lam, pos, k_pos)
        oa = rmsnorm(oa, w['diff_subln'][l]) * (1.0 - lam_init)
        qb, kb, vb = heads(qb, SB_HEADS), heads(kb, SB_HEADS), heads(vb, SB_HEADS)
        ob = stick_breaking_attention(qb, with_past(kb, 2, l, 2), with_past(vb, 3, l, 2), pos, k_pos)
        q = heads(rmsnorm(cq, w['mla_q_norm'][l]) @ w['mla_w_uq'][l], MLA_HEADS)
        c_kv = rmsnorm(ckv, w['mla_kv_norm'][l])
        k_r = rope(kr, pos)
        c_all = with_past(c_kv, 4, l, 1)
        kv = (c_all @ w['mla_w_ukv'][l]).reshape(b, p_len + t, MLA_HEADS, MLA_NOPE + MLA_DV).transpose(0, 2, 1, 3)
        oc = mla_attention(q[..., :MLA_NOPE], rope(q[..., MLA_NOPE:], pos), kv[..., :MLA_NOPE],
                           with_past(k_r, 5, l, 1), kv[..., MLA_NOPE:], pos, k_pos)
        for r, new in zip(rows, (ka, va, kb, vb, c_kv, k_r)):
            r.append(new)
        ga, gb, gc = jnp.split(jax.nn.sigmoid(xn @ w['w_gate'][l]), N_BRANCHES, axis=-1)
        merged = (ga * (merge_heads(oa) @ w['w_branch_a'][l])
                  + gb * (merge_heads(ob) @ w['w_branch_b'][l])
                  + gc * (merge_heads(oc) @ w['w_branch_c'][l]))
        h = h + merged @ w['w_out'][l]
        xn = rmsnorm(h, w['ffn_norm'][l])
        h = h + (jax.nn.silu(xn @ w['ffn_w1'][l]) * (xn @ w['ffn_w3'][l])) @ w['ffn_w2'][l]
    y = rmsnorm(h, w['final_norm'])
    return y, tuple(jnp.stack(r, axis=0) for r in rows)


def setup_inputs(seed: int = 0) -> dict:
    key = jax.random.key(seed)
    ks = jax.random.split(key, 32)
    counter = iter(range(32))

    def nrm(shape, scale):
        return jax.random.normal(ks[next(counter)], shape, jnp.float32) * scale

    def gain(shape):
        return 1.0 + 0.02 * jax.random.normal(ks[next(counter)], shape, jnp.float32)

    resid = (2 * DEPTH) ** -0.5
    return {
        'x_prompt': nrm((BATCH, SEQ, D_MODEL), 1.0),
        'x_sample': nrm((DEC_BATCH, DEC_SEQ, D_MODEL), 1.0),
        'cache_diff_k': nrm((DEPTH, DEC_BATCH, DIFF_HEADS, PAST_LEN, 2 * DIFF_DH), 1.0),
        'cache_diff_v': nrm((DEPTH, DEC_BATCH, DIFF_HEADS, PAST_LEN, DIFF_DV), 1.0),
        'cache_sb_k': nrm((DEPTH, DEC_BATCH, SB_HEADS, PAST_LEN, SB_DH), 1.0),
        'cache_sb_v': nrm((DEPTH, DEC_BATCH, SB_HEADS, PAST_LEN, SB_DH), 1.0),
        'cache_mla_latent': nrm((DEPTH, DEC_BATCH, PAST_LEN, MLA_KV_LORA), 1.0),
        'cache_mla_krope': nrm((DEPTH, DEC_BATCH, PAST_LEN, MLA_ROPE), 1.0),
        'attn_norm': gain((DEPTH, D_MODEL)),
        'w_in': nrm((DEPTH, D_MODEL, IN_WIDTH), D_MODEL ** -0.5),
        'diff_lambda': nrm((DEPTH, 4, DIFF_DH), 0.1),
        'diff_subln': gain((DEPTH, DIFF_DV)),
        'mla_q_norm': gain((DEPTH, MLA_Q_LORA)),
        'mla_w_uq': nrm((DEPTH, MLA_Q_LORA, MLA_HEADS * (MLA_NOPE + MLA_ROPE)), MLA_Q_LORA ** -0.5),
        'mla_kv_norm': gain((DEPTH, MLA_KV_LORA)),
        'mla_w_ukv': nrm((DEPTH, MLA_KV_LORA, MLA_HEADS * (MLA_NOPE + MLA_DV)), MLA_KV_LORA ** -0.5),
        'w_gate': nrm((DEPTH, D_MODEL, N_BRANCHES * D_MODEL), D_MODEL ** -0.5),
        'w_branch_a': nrm((DEPTH, DIFF_W, D_MODEL), DIFF_W ** -0.5),
        'w_branch_b': nrm((DEPTH, SB_W, D_MODEL), SB_W ** -0.5),
        'w_branch_c': nrm((DEPTH, MLA_W, D_MODEL), MLA_W ** -0.5),
        'w_out': nrm((DEPTH, D_MODEL, D_MODEL), D_MODEL ** -0.5 * resid),
        'ffn_norm': gain((DEPTH, D_MODEL)),
        'ffn_w1': nrm((DEPTH, D_MODEL, FFN_HIDDEN), D_MODEL ** -0.5),
        'ffn_w3': nrm((DEPTH, D_MODEL, FFN_HIDDEN), D_MODEL ** -0.5),
        'ffn_w2': nrm((DEPTH, FFN_HIDDEN, D_MODEL), FFN_HIDDEN ** -0.5 * resid),
        'final_norm': gain((D_MODEL,)),
    }


def reference(x_prompt, x_sample, cache_diff_k, cache_diff_v, cache_sb_k, cache_sb_v, cache_mla_latent,
              cache_mla_krope, attn_norm, w_in, diff_lambda, diff_subln, mla_q_norm, mla_w_uq, mla_kv_norm,
              mla_w_ukv, w_gate, w_branch_a, w_branch_b, w_branch_c, w_out, ffn_norm, ffn_w1, ffn_w3, ffn_w2,
              final_norm):
    w = dict(attn_norm=attn_norm, w_in=w_in, diff_lambda=diff_lambda, diff_subln=diff_subln,
             mla_q_norm=mla_q_norm, mla_w_uq=mla_w_uq, mla_kv_norm=mla_kv_norm, mla_w_ukv=mla_w_ukv,
             w_gate=w_gate, w_branch_a=w_branch_a, w_branch_b=w_branch_b, w_branch_c=w_branch_c,
             w_out=w_out, ffn_norm=ffn_norm, ffn_w1=ffn_w1, ffn_w3=ffn_w3, ffn_w2=ffn_w2,
             final_norm=final_norm)
    y_prompt, (p_dk, p_dv, p_sk, p_sv, p_lat, p_kr) = run_trunk(x_prompt, None, w)
    y_sample, (s_dk, s_dv, s_sk, s_sv, s_lat, s_kr) = run_trunk(
        x_sample, (cache_diff_k, cache_diff_v, cache_sb_k, cache_sb_v, cache_mla_latent, cache_mla_krope), w)
    return (y_prompt, y_sample, p_dk, p_dv, p_sk, p_sv, p_lat, p_kr, s_dk, s_dv, s_sk, s_sv, s_lat, s_kr)
```

```python
import functools
import math

import jax
import jax.numpy as jnp
from jax import lax
from jax.experimental import pallas as pl
from jax.experimental.pallas import tpu as pltpu

D_MODEL = 4096
BATCH = 2
SEQ = 4096
DEPTH = 4
DEC_BATCH = 8
DEC_SEQ = 16
PAST_LEN = 2048

CHUNK = 64
ROPE_THETA = 10000.0
NORM_EPS = 1e-6
NEG_INF = -1e30

DIFF_HEADS = 8
DIFF_DH = 64
SB_HEADS = 8
MLA_HEADS = 16
MLA_ROPE = 64
MLA_Q_LORA = 1024
MLA_KV_LORA = 512
HEAD_W = 128
MLA_QK_PAD = 2 * HEAD_W
N_BRANCHES = 3
COL_QA, COL_KA, COL_VA, COL_QB, COL_KB, COL_VB, COL_CQ, COL_CKV, COL_KR = (
    0, 1024, 2048, 3072, 4096, 5120, 6144, 7168, 7680)
FFN_HIDDEN = -(-8 * D_MODEL // 768) * 256

F32 = jnp.float32
BF16 = jnp.bfloat16

V7X_LANES = 128
V7X_BF16_SUBLANES = 16
V7X_VMEM_LIMIT_CAP = 60 * 1024 * 1024


def _tile(n, cap, mult):
    best = None
    for t in range(mult, min(n, cap) + 1, mult):
        if n % t == 0:
            best = t
    if best is None:
        raise ValueError(f"no tile for {n} (cap {cap}, multiple of {mult})")
    return best


def _params(block_bytes):
    need = 2 * block_bytes + (8 << 20)
    return pltpu.CompilerParams(vmem_limit_bytes=int(min(max(need, 32 << 20), V7X_VMEM_LIMIT_CAP)))


def _nbytes(shape, dtype):
    return math.prod(shape) * jnp.dtype(dtype).itemsize


def _dot(a, b):
    return jnp.dot(a, b, preferred_element_type=F32)


def _dot_nt(a, b):
    return lax.dot_general(a, b, (((1,), (1,)), ((), ())), preferred_element_type=F32)


def _rms(x, g):
    return x * lax.rsqrt(jnp.mean(x * x, axis=-1, keepdims=True) + NORM_EPS) * g


def _rope_lanes(x, cos, sin):
    lane = lax.broadcasted_iota(jnp.int32, x.shape, 1)
    first_half = (lane % 64) < 32
    partner = jnp.where(first_half, pltpu.roll(x, x.shape[1] - 32, 1), pltpu.roll(x, 32, 1))
    return x * cos + partner * sin


def _rmsnorm_kernel(l_ref, x_ref, g_ref, o_ref):
    del l_ref
    o_ref[...] = _rms(x_ref[...], g_ref[...]).astype(o_ref.dtype)


def _rmsnorm_rows(x, g_stack, lidx, out_dtype):
    m, d = x.shape
    tr = _tile(m, 320, V7X_BF16_SUBLANES)
    blk = _nbytes((tr, d), F32) + _nbytes((tr, d), out_dtype)
    return pl.pallas_call(
        _rmsnorm_kernel,
        grid_spec=pltpu.PrefetchScalarGridSpec(
            num_scalar_prefetch=1, grid=(m // tr,),
            in_specs=[pl.BlockSpec((tr, d), lambda i, l: (i, 0)),
                      pl.BlockSpec((None, 1, d), lambda i, l: (l[0], 0, 0))],
            out_specs=pl.BlockSpec((tr, d), lambda i, l: (i, 0))),
        out_shape=jax.ShapeDtypeStruct((m, d), out_dtype),
        compiler_params=_params(blk), name="rmsnorm_rows",
    )(lidx, x, g_stack)


def _mm_kernel(l_ref, x_ref, w_ref, *rest, nk, has_res, rope):
    del l_ref
    rest = list(rest)
    r_ref = rest.pop(0) if has_res else None
    cos_ref, sin_ref = (rest.pop(0), rest.pop(0)) if rope else (None, None)
    o_ref = rest.pop(0)
    part = _dot(x_ref[...].astype(BF16), w_ref[...])

    def finish(acc):
        if has_res:
            acc = acc + r_ref[...]
        if rope:
            cos, sin = cos_ref[...], sin_ref[...]
            gw = cos.shape[1]
            for g in range(acc.shape[1] // gw):
                seg = acc[:, g * gw:(g + 1) * gw]
                o_ref[:, g * gw:(g + 1) * gw] = _rope_lanes(seg, cos, sin).astype(o_ref.dtype)
        else:
            o_ref[...] = acc.astype(o_ref.dtype)

    if nk == 1:
        finish(part)
    else:
        acc_ref, = rest
        k = pl.program_id(2)

        @pl.when(k == 0)
        def _():
            acc_ref[...] = part

        @pl.when(k > 0)
        def _():
            acc_ref[...] += part

        @pl.when(k == nk - 1)
        def _():
            finish(acc_ref[...])


def _matmul(x, w_stack, lidx, *, out_dtype, tm_cap, tn_cap, tk=None, res=None, rope=None,
            x_stacked=False, name="matmul"):
    m, kdim = x.shape[-2:]
    n = w_stack.shape[2]
    tm = _tile(m, tm_cap, V7X_BF16_SUBLANES)
    tn = _tile(n, tn_cap, V7X_LANES) if n % V7X_LANES == 0 else n
    tk = kdim if tk is None else tk
    nk = kdim // tk
    if x_stacked:
        x_spec = pl.BlockSpec((None, tm, tk), lambda i, j, k, l: (l[0], i, k))
    else:
        x_spec = pl.BlockSpec((tm, tk), lambda i, j, k, l: (i, k))
    in_specs = [x_spec, pl.BlockSpec((None, tk, tn), lambda i, j, k, l: (l[0], k, j))]
    args = [x, w_stack]
    blk = _nbytes((tm, tk), x.dtype) + _nbytes((tk, tn), w_stack.dtype) + 2 * _nbytes((tm, tn), F32)
    if res is not None:
        in_specs.append(pl.BlockSpec((tm, tn), lambda i, j, k, l: (i, j)))
        args.append(res)
        blk += _nbytes((tm, tn), F32)
    if rope is not None:
        gw = rope[0].shape[1]
        assert tn % gw == 0
        in_specs += [pl.BlockSpec((tm, gw), lambda i, j, k, l: (i, 0))] * 2
        args += list(rope)
        blk += 2 * _nbytes((tm, gw), F32)
    return pl.pallas_call(
        functools.partial(_mm_kernel, nk=nk, has_res=res is not None, rope=rope is not None),
        grid_spec=pltpu.PrefetchScalarGridSpec(
            num_scalar_prefetch=1, grid=(m // tm, n // tn, nk),
            in_specs=in_specs,
            out_specs=pl.BlockSpec((tm, tn), lambda i, j, k, l: (i, j)),
            scratch_shapes=[pltpu.VMEM((tm, tn), F32)] if nk > 1 else []),
        out_shape=jax.ShapeDtypeStruct((m, n), out_dtype),
        compiler_params=_params(blk), name=name,
    )(lidx, *args)


def _kr_kernel(l_ref, xn_ref, w_ref, cos_ref, sin_ref, kr_in, kr_o, kr128_o):
    del l_ref, kr_in
    r = _rope_lanes(_dot(xn_ref[...], w_ref[...]), cos_ref[...], sin_ref[...])
    kr_o[...] = r[:, :MLA_ROPE]
    kr128_o[...] = r.astype(BF16)


def _shared_rope_key(xn, w_kr_pad, cos, sin, kr_buf, lidx):
    m, d = xn.shape
    tm = _tile(m, 640, V7X_BF16_SUBLANES)
    blk = _nbytes((tm, d), BF16) + _nbytes((d, HEAD_W), BF16) + 4 * _nbytes((tm, HEAD_W), F32)
    return pl.pallas_call(
        _kr_kernel,
        grid_spec=pltpu.PrefetchScalarGridSpec(
            num_scalar_prefetch=1, grid=(m // tm,),
            in_specs=[pl.BlockSpec((tm, d), lambda i, l: (i, 0)),
                      pl.BlockSpec((None, d, HEAD_W), lambda i, l: (l[0], 0, 0)),
                      pl.BlockSpec((tm, HEAD_W), lambda i, l: (i, 0)),
                      pl.BlockSpec((tm, HEAD_W), lambda i, l: (i, 0)),
                      pl.BlockSpec(memory_space=pl.ANY)],
            out_specs=[pl.BlockSpec((None, tm, MLA_ROPE), lambda i, l: (l[0], i, 0)),
                       pl.BlockSpec((tm, HEAD_W), lambda i, l: (i, 0))]),
        out_shape=[jax.ShapeDtypeStruct(kr_buf.shape, F32), jax.ShapeDtypeStruct((m, HEAD_W), BF16)],
        input_output_aliases={5: 0},
        compiler_params=_params(blk), name="shared_rope_key",
    )(lidx, xn, w_kr_pad, cos, sin, kr_buf)


def _post_heads_kernel(l_ref, qa_ref, ka_ref, va_ref, qb_ref, kb_ref, vb_ref, cos_ref, sin_ref,
                       dk_in, dv_in, sk_in, sv_in, qa_o, qb_o, dk_o, dv_o, sk_o, sv_o):
    del l_ref, dk_in, dv_in, sk_in, sv_in
    cos, sin = cos_ref[...], sin_ref[...]
    qa_o[...] = _rope_lanes(qa_ref[...], cos, sin).astype(BF16)
    dk_o[...] = _rope_lanes(ka_ref[...], cos, sin)
    dv_o[...] = va_ref[...]
    qb_o[...] = qb_ref[...].astype(BF16)
    sk_o[...] = kb_ref[...]
    sv_o[...] = vb_ref[...]


def _post_heads(proj, cos, sin, bufs, lidx, *, row0, nb, t, tt):
    nt = t // tt
    rb0 = row0 // tt
    assert row0 % tt == 0

    def col(cb):
        return pl.BlockSpec((tt, HEAD_W), lambda b, ti, h, l: (rb0 + b * nt + ti, cb // HEAD_W + h))

    tab = pl.BlockSpec((tt, HEAD_W), lambda b, ti, h, l: (rb0 + b * nt + ti, 0))
    qspec = pl.BlockSpec((tt, HEAD_W), lambda b, ti, h, l: (b * nt + ti, h))
    bspec = pl.BlockSpec((None, None, None, tt, HEAD_W), lambda b, ti, h, l: (l[0], b, h, ti, 0))
    anyspec = pl.BlockSpec(memory_space=pl.ANY)
    qshape = jax.ShapeDtypeStruct((nb * t, DIFF_HEADS * HEAD_W), BF16)
    blk = 8 * _nbytes((tt, HEAD_W), F32) + 6 * _nbytes((tt, HEAD_W), F32)
    return pl.pallas_call(
        _post_heads_kernel,
        grid_spec=pltpu.PrefetchScalarGridSpec(
            num_scalar_prefetch=1, grid=(nb, nt, DIFF_HEADS),
            in_specs=[col(COL_QA), col(COL_KA), col(COL_VA), col(COL_QB), col(COL_KB), col(COL_VB),
                      tab, tab, anyspec, anyspec, anyspec, anyspec],
            out_specs=[qspec, qspec, bspec, bspec, bspec, bspec]),
        out_shape=[qshape, qshape] + [jax.ShapeDtypeStruct(b.shape, F32) for b in bufs],
        input_output_aliases={9: 2, 10: 3, 11: 4, 12: 5},
        compiler_params=_params(blk), name="post_heads",
    )(lidx, proj, proj, proj, proj, proj, proj, cos, sin, *bufs)


def _post_mla_kernel(l_ref, cq_ref, ckv_ref, gq_ref, gkv_ref, lat_in, cqn_o, ckvb_o, lat_o):
    del l_ref, lat_in
    cqn_o[...] = _rms(cq_ref[...], gq_ref[...]).astype(BF16)
    c = _rms(ckv_ref[...], gkv_ref[...])
    lat_o[...] = c
    ckvb_o[...] = c.astype(BF16)


def _post_mla(proj, gq, gkv, lat_buf, lidx):
    m = proj.shape[0]
    tt = _tile(m, 640, V7X_BF16_SUBLANES)
    blk = 3 * _nbytes((tt, MLA_Q_LORA + MLA_KV_LORA), F32)
    return pl.pallas_call(
        _post_mla_kernel,
        grid_spec=pltpu.PrefetchScalarGridSpec(
            num_scalar_prefetch=1, grid=(m // tt,),
            in_specs=[pl.BlockSpec((tt, MLA_Q_LORA), lambda i, l: (i, COL_CQ // MLA_Q_LORA)),
                      pl.BlockSpec((tt, MLA_KV_LORA), lambda i, l: (i, COL_CKV // MLA_KV_LORA)),
                      pl.BlockSpec((None, 1, MLA_Q_LORA), lambda i, l: (l[0], 0, 0)),
                      pl.BlockSpec((None, 1, MLA_KV_LORA), lambda i, l: (l[0], 0, 0)),
                      pl.BlockSpec(memory_space=pl.ANY)],
            out_specs=[pl.BlockSpec((tt, MLA_Q_LORA), lambda i, l: (i, 0)),
                       pl.BlockSpec((tt, MLA_KV_LORA), lambda i, l: (i, 0)),
                       pl.BlockSpec((None, tt, MLA_KV_LORA), lambda i, l: (l[0], i, 0))]),
        out_shape=[jax.ShapeDtypeStruct((m, MLA_Q_LORA), BF16),
                   jax.ShapeDtypeStruct((m, MLA_KV_LORA), BF16),
                   jax.ShapeDtypeStruct(lat_buf.shape, F32)],
        input_output_aliases={5: 2},
        compiler_params=_params(blk), name="post_mla",
    )(lidx, proj, proj, gq, gkv, lat_buf)


def _softmax_step(s, v, m_ref, l_ref, acc_ref):
    m_prev = m_ref[...]
    m_new = jnp.maximum(m_prev, jnp.max(s, axis=-1, keepdims=True))
    alpha = jnp.exp(m_prev - m_new)
    p = jnp.exp(s - m_new)
    l_ref[...] = alpha * l_ref[...] + jnp.sum(p, axis=-1, keepdims=True)
    acc_ref[...] = alpha * acc_ref[...] + _dot(p.astype(BF16), v)
    m_ref[...] = m_new


def _softmax_init(m_ref, l_ref, acc_ref):
    m_ref[...] = jnp.full(m_ref.shape, NEG_INF, F32)
    l_ref[...] = jnp.zeros(l_ref.shape, F32)
    acc_ref[...] = jnp.zeros(acc_ref.shape, F32)


def _positions(shape, q0, k0):
    qpos = q0 + lax.broadcasted_iota(jnp.int32, shape, 0)
    kpos = k0 + lax.broadcasted_iota(jnp.int32, shape, 1)
    return qpos, kpos


def _chunk_mask(shape, q0, k0):
    qpos, kpos = _positions(shape, q0, k0)
    return (kpos // CHUNK) <= (qpos // CHUNK)


def _diff_lambda(lam_ref, li_ref):
    lv = lam_ref[...]
    lam_init = li_ref[:, 0:1]
    d1 = jnp.sum(lv[0:1, :] * lv[1:2, :], axis=-1, keepdims=True)
    d2 = jnp.sum(lv[2:3, :] * lv[3:4, :], axis=-1, keepdims=True)
    return jnp.exp(d1) - jnp.exp(d2) + lam_init, lam_init


def _split_components(q):
    lane = lax.broadcasted_iota(jnp.int32, q.shape, 1)
    zero = jnp.zeros_like(q)
    return jnp.where(lane < DIFF_DH, q, zero), jnp.where(lane >= DIFF_DH, q, zero)


def _softplus(z):
    return jnp.maximum(z, 0.0) + jnp.log1p(jnp.exp(-jnp.abs(z)))


def _upper_ones(n):
    r = lax.broadcasted_iota(jnp.int32, (n, n), 0)
    c = lax.broadcasted_iota(jnp.int32, (n, n), 1)
    return jnp.where(r > c, 1.0, 0.0).astype(BF16)


def _sb_block(z, mask, carry, v, tri):
    sp = _softplus(z)
    log_stay = -sp if mask is None else jnp.where(mask, -sp, 0.0)
    hi = log_stay.astype(BF16)
    lo = (log_stay - hi.astype(F32)).astype(BF16)
    between = _dot(hi, tri) + _dot(lo, tri) + carry
    a = jnp.exp(z - sp + between)
    if mask is not None:
        a = jnp.where(mask, a, 0.0)
    return _dot(a.astype(BF16), v), carry + jnp.sum(log_stay, axis=-1, keepdims=True)


def _diff_prompt_kernel(l_ref, q_ref, k_ref, v_ref, lam_ref, g_ref, li_ref, o_ref,
                        m1, l1, a1, m2, l2, a2, *, tq, tk):
    del l_ref
    qi = pl.program_id(2)
    ratio = tq // tk
    q1, q2 = _split_components(q_ref[...])
    scale = DIFF_DH ** -0.5
    _softmax_init(m1, l1, a1)
    _softmax_init(m2, l2, a2)

    def block(j, masked):
        start = pl.multiple_of(j * tk, tk)
        kb = k_ref[pl.ds(start, tk), :].astype(BF16)
        vb = v_ref[pl.ds(start, tk), :].astype(BF16)
        s1 = _dot_nt(q1, kb) * scale
        s2 = _dot_nt(q2, kb) * scale
        if masked:
            mask = _chunk_mask(s1.shape, qi * tq, start)
            s1 = jnp.where(mask, s1, NEG_INF)
            s2 = jnp.where(mask, s2, NEG_INF)
        _softmax_step(s1, vb, m1, l1, a1)
        _softmax_step(s2, vb, m2, l2, a2)

    def body(j, c):
        block(j, False)
        return c

    lax.fori_loop(0, qi * ratio, body, 0)
    for u in range(ratio):
        block(qi * ratio + u, True)
    lam, lam_init = _diff_lambda(lam_ref, li_ref)
    o = a1[...] / l1[...] - lam * (a2[...] / l2[...])
    o_ref[...] = (_rms(o, g_ref[...]) * (1.0 - lam_init)).astype(o_ref.dtype)


def _diff_prompt(q, k_buf, v_buf, lam, g, li, lidx, *, nb, t):
    tq = _tile(t, 256, CHUNK)
    tk = tq
    nq = t // tq
    kv_spec = pl.BlockSpec((None, None, None, t, HEAD_W), lambda b, h, i, l: (l[0], b, h, 0, 0))
    blk = 2 * _nbytes((t, HEAD_W), F32) + 16 * _nbytes((tq, tk), F32)
    return pl.pallas_call(
        functools.partial(_diff_prompt_kernel, tq=tq, tk=tk),
        grid_spec=pltpu.PrefetchScalarGridSpec(
            num_scalar_prefetch=1, grid=(nb, DIFF_HEADS, nq),
            in_specs=[pl.BlockSpec((tq, HEAD_W), lambda b, h, i, l: (b * nq + i, h)),
                      kv_spec, kv_spec,
                      pl.BlockSpec((None, 4, DIFF_DH), lambda b, h, i, l: (l[0], 0, 0)),
                      pl.BlockSpec((None, 1, HEAD_W), lambda b, h, i, l: (l[0], 0, 0)),
                      pl.BlockSpec((None, 1, HEAD_W), lambda b, h, i, l: (l[0], 0, 0))],
            out_specs=pl.BlockSpec((tq, HEAD_W), lambda b, h, i, l: (b * nq + i, h)),
            scratch_shapes=[pltpu.VMEM((tq, 1), F32), pltpu.VMEM((tq, 1), F32), pltpu.VMEM((tq, HEAD_W), F32),
                            pltpu.VMEM((tq, 1), F32), pltpu.VMEM((tq, 1), F32), pltpu.VMEM((tq, HEAD_W), F32)]),
        out_shape=jax.ShapeDtypeStruct((nb * t, DIFF_HEADS * HEAD_W), BF16),
        compiler_params=_params(blk), name="diff_prompt",
    )(lidx, q, k_buf, v_buf, lam, g, li)


def _sb_prompt_kernel(l_ref, q_ref, k_ref, v_ref, o_ref, c_ref, acc_ref, *, tq, tk):
    del l_ref
    qi = pl.program_id(2)
    ratio = tq // tk
    q = q_ref[...]
    scale = HEAD_W ** -0.5
    tri = _upper_ones(tk)
    c_ref[...] = jnp.zeros(c_ref.shape, F32)
    acc_ref[...] = jnp.zeros(acc_ref.shape, F32)

    def block(j, masked):
        start = pl.multiple_of(j * tk, tk)
        kb = k_ref[pl.ds(start, tk), :].astype(BF16)
        vb = v_ref[pl.ds(start, tk), :].astype(BF16)
        z = _dot_nt(q, kb) * scale
        mask = None
        if masked:
            qpos, kpos = _positions(z.shape, qi * tq, start)
            mask = kpos < qpos
        out, carry = _sb_block(z, mask, c_ref[...], vb, tri)
        acc_ref[...] += out
        c_ref[...] = carry

    for u in reversed(range(ratio)):
        block(qi * ratio + u, True)

    def body(it, c):
        block(qi * ratio - 1 - it, False)
        return c

    lax.fori_loop(0, qi * ratio, body, 0)
    o_ref[...] = acc_ref[...].astype(o_ref.dtype)


def _sb_prompt(q, k_buf, v_buf, lidx, *, nb, t):
    tq = _tile(t, 256, CHUNK)
    tk = tq
    nq = t // tq
    kv_spec = pl.BlockSpec((None, None, None, t, HEAD_W), lambda b, h, i, l: (l[0], b, h, 0, 0))
    blk = 2 * _nbytes((t, HEAD_W), F32) + 16 * _nbytes((tq, tk), F32)
    return pl.pallas_call(
        functools.partial(_sb_prompt_kernel, tq=tq, tk=tk),
        grid_spec=pltpu.PrefetchScalarGridSpec(
            num_scalar_prefetch=1, grid=(nb, SB_HEADS, nq),
            in_specs=[pl.BlockSpec((tq, HEAD_W), lambda b, h, i, l: (b * nq + i, h)), kv_spec, kv_spec],
            out_specs=pl.BlockSpec((tq, HEAD_W), lambda b, h, i, l: (b * nq + i, h)),
            scratch_shapes=[pltpu.VMEM((tq, 1), F32), pltpu.VMEM((tq, HEAD_W), F32)]),
        out_shape=jax.ShapeDtypeStruct((nb * t, SB_HEADS * HEAD_W), BF16),
        compiler_params=_params(blk), name="sb_prompt",
    )(lidx, q, k_buf, v_buf)


def _mla_prompt_kernel(q_ref, kn_ref, kr_ref, v_ref, o_ref, m_ref, l_ref, acc_ref, *, tq, tk):
    qi = pl.program_id(2)
    ratio = tq // tk
    q = q_ref[...]
    scale = (HEAD_W + MLA_ROPE) ** -0.5
    _softmax_init(m_ref, l_ref, acc_ref)

    def block(j, masked):
        start = pl.multiple_of(j * tk, tk)
        kcat = jnp.concatenate([kn_ref[pl.ds(start, tk), :], kr_ref[pl.ds(start, tk), :]], axis=1)
        s = _dot_nt(q, kcat) * scale
        if masked:
            s = jnp.where(_chunk_mask(s.shape, qi * tq, start), s, NEG_INF)
        _softmax_step(s, v_ref[pl.ds(start, tk), :], m_ref, l_ref, acc_ref)

    def body(j, c):
        block(j, False)
        return c

    lax.fori_loop(0, qi * ratio, body, 0)
    for u in range(ratio):
        block(qi * ratio + u, True)
    o_ref[...] = (acc_ref[...] / l_ref[...]).astype(o_ref.dtype)


def _mla_prompt(qcat, kv, kr128, *, nb, t):
    tq = _tile(t, 256, CHUNK)
    tk = tq
    nq = t // tq
    blk = 3 * _nbytes((t, HEAD_W), BF16) + 16 * _nbytes((tq, tk), F32)
    return pl.pallas_call(
        functools.partial(_mla_prompt_kernel, tq=tq, tk=tk),
        grid=(nb, MLA_HEADS, nq),
        in_specs=[pl.BlockSpec((tq, MLA_QK_PAD), lambda b, h, i: (b * nq + i, h)),
                  pl.BlockSpec((t, HEAD_W), lambda b, h, i: (b, 2 * h)),
                  pl.BlockSpec((t, HEAD_W), lambda b, h, i: (b, 0)),
                  pl.BlockSpec((t, HEAD_W), lambda b, h, i: (b, 2 * h + 1))],
        out_specs=pl.BlockSpec((tq, HEAD_W), lambda b, h, i: (b * nq + i, h)),
        scratch_shapes=[pltpu.VMEM((tq, 1), F32), pltpu.VMEM((tq, 1), F32), pltpu.VMEM((tq, HEAD_W), F32)],
        out_shape=jax.ShapeDtypeStruct((nb * t, MLA_HEADS * HEAD_W), BF16),
        compiler_params=_params(blk), name="mla_prompt",
    )(qcat, kv, kr128, kv)


def _pad_rows(x, rows):
    return jnp.concatenate([x, jnp.zeros((rows - x.shape[0], x.shape[1]), x.dtype)], axis=0)


def _two_part_softmax(sp, sn, vp, vn):
    m = jnp.maximum(jnp.max(sp, axis=-1, keepdims=True), jnp.max(sn, axis=-1, keepdims=True))
    pp, pn = jnp.exp(sp - m), jnp.exp(sn - m)
    denom = jnp.sum(pp, axis=-1, keepdims=True) + jnp.sum(pn, axis=-1, keepdims=True)
    return (_dot(pp.astype(BF16), vp) + _dot(pn.astype(BF16), vn)) / denom


def _diff_sample_kernel(l_ref, q_ref, kp_ref, vp_ref, kn_ref, vn_ref, lam_ref, g_ref, li_ref, o_ref, *, past, t):
    del l_ref
    q1, q2 = _split_components(q_ref[...])
    scale = DIFF_DH ** -0.5
    kp, vp = kp_ref[...].astype(BF16), vp_ref[...].astype(BF16)
    kn = _pad_rows(kn_ref[...], HEAD_W).astype(BF16)
    vn = _pad_rows(vn_ref[...], HEAD_W).astype(BF16)
    mask_p = _chunk_mask((t, past), past, 0)
    qpos, kpos = _positions((t, HEAD_W), past, past)
    mask_n = ((kpos // CHUNK) <= (qpos // CHUNK)) & (kpos < past + t)

    def attend(qx):
        sp = jnp.where(mask_p, _dot_nt(qx, kp) * scale, NEG_INF)
        sn = jnp.where(mask_n, _dot_nt(qx, kn) * scale, NEG_INF)
        return _two_part_softmax(sp, sn, vp, vn)

    lam, lam_init = _diff_lambda(lam_ref, li_ref)
    o = attend(q1) - lam * attend(q2)
    o_ref[...] = (_rms(o, g_ref[...]) * (1.0 - lam_init)).astype(o_ref.dtype)


def _diff_sample(q, k_cache, v_cache, k_buf, v_buf, lam, g, li, lidx, *, nb, t, past):
    cache_spec = pl.BlockSpec((None, None, None, past, HEAD_W), lambda b, h, l: (l[0], b, h, 0, 0))
    new_spec = pl.BlockSpec((None, None, None, t, HEAD_W), lambda b, h, l: (l[0], b, h, 0, 0))
    blk = 2 * _nbytes((past, HEAD_W), F32) + 8 * _nbytes((t, past), F32)
    return pl.pallas_call(
        functools.partial(_diff_sample_kernel, past=past, t=t),
        grid_spec=pltpu.PrefetchScalarGridSpec(
            num_scalar_prefetch=1, grid=(nb, DIFF_HEADS),
            in_specs=[pl.BlockSpec((t, HEAD_W), lambda b, h, l: (b, h)),
                      cache_spec, cache_spec, new_spec, new_spec,
                      pl.BlockSpec((None, 4, DIFF_DH), lambda b, h, l: (l[0], 0, 0)),
                      pl.BlockSpec((None, 1, HEAD_W), lambda b, h, l: (l[0], 0, 0)),
                      pl.BlockSpec((None, 1, HEAD_W), lambda b, h, l: (l[0], 0, 0))],
            out_specs=pl.BlockSpec((t, HEAD_W), lambda b, h, l: (b, h))),
        out_shape=jax.ShapeDtypeStruct((nb * t, DIFF_HEADS * HEAD_W), BF16),
        compiler_params=_params(blk), name="diff_sample",
    )(lidx, q, k_cache, v_cache, k_buf, v_buf, lam, g, li)


def _sb_sample_kernel(l_ref, q_ref, kp_ref, vp_ref, kn_ref, vn_ref, o_ref, *, past, t, cw):
    del l_ref
    q = q_ref[...]
    scale = HEAD_W ** -0.5
    kn = _pad_rows(kn_ref[...], HEAD_W).astype(BF16)
    vn = _pad_rows(vn_ref[...], HEAD_W).astype(BF16)
    qpos, kpos = _positions((t, HEAD_W), past, past)
    mask_n = (kpos < qpos) & (kpos < past + t)
    acc, carry = _sb_block(_dot_nt(q, kn) * scale, mask_n, jnp.zeros((t, 1), F32), vn, _upper_ones(HEAD_W))
    tri = _upper_ones(cw)
    for c in reversed(range(past // cw)):
        kb = kp_ref[c * cw:(c + 1) * cw, :].astype(BF16)
        vb = vp_ref[c * cw:(c + 1) * cw, :].astype(BF16)
        qpos, kpos = _positions((t, cw), past, c * cw)
        out, carry = _sb_block(_dot_nt(q, kb) * scale, kpos < qpos, carry, vb, tri)
        acc = acc + out
    o_ref[...] = acc.astype(o_ref.dtype)


def _sb_sample(q, k_cache, v_cache, k_buf, v_buf, lidx, *, nb, t, past):
    cw = _tile(past, 256, V7X_LANES)
    cache_spec = pl.BlockSpec((None, None, None, past, HEAD_W), lambda b, h, l: (l[0], b, h, 0, 0))
    new_spec = pl.BlockSpec((None, None, None, t, HEAD_W), lambda b, h, l: (l[0], b, h, 0, 0))
    blk = 2 * _nbytes((past, HEAD_W), F32) + 8 * _nbytes((t, past), F32)
    return pl.pallas_call(
        functools.partial(_sb_sample_kernel, past=past, t=t, cw=cw),
        grid_spec=pltpu.PrefetchScalarGridSpec(
            num_scalar_prefetch=1, grid=(nb, SB_HEADS),
            in_specs=[pl.BlockSpec((t, HEAD_W), lambda b, h, l: (b, h)),
                      cache_spec, cache_spec, new_spec, new_spec],
            out_specs=pl.BlockSpec((t, HEAD_W), lambda b, h, l: (b, h))),
        out_shape=jax.ShapeDtypeStruct((nb * t, SB_HEADS * HEAD_W), BF16),
        compiler_params=_params(blk), name="sb_sample",
    )(lidx, q, k_cache, v_cache, k_buf, v_buf)


def _mla_sample_kernel(l_ref, q_ref, knp_ref, krp_ref, vp_ref, knn_ref, krn_ref, vn_ref, o_ref, *, past, t):
    del l_ref
    q = q_ref[...]
    scale = (HEAD_W + MLA_ROPE) ** -0.5
    kp = jnp.concatenate([knp_ref[...], krp_ref[...]], axis=1)
    kn = _pad_rows(jnp.concatenate([knn_ref[...], krn_ref[...]], axis=1), HEAD_W)
    vn = _pad_rows(vn_ref[...], HEAD_W)
    qpos, kpos = _positions((t, HEAD_W), past, past)
    mask_n = ((kpos // CHUNK) <= (qpos // CHUNK)) & (kpos < past + t)
    sp = jnp.where(_chunk_mask((t, past), past, 0), _dot_nt(q, kp) * scale, NEG_INF)
    sn = jnp.where(mask_n, _dot_nt(q, kn) * scale, NEG_INF)
    o_ref[...] = _two_part_softmax(sp, sn, vp_ref[...], vn).astype(o_ref.dtype)


def _mla_sample(qcat, kv_past, kr_cache128, kv_new, kr_new128, lidx, *, nb, t, past, row0):
    rb0 = row0 // t
    blk = 3 * _nbytes((past, HEAD_W), BF16) + 8 * _nbytes((t, past), F32)
    return pl.pallas_call(
        functools.partial(_mla_sample_kernel, past=past, t=t),
        grid_spec=pltpu.PrefetchScalarGridSpec(
            num_scalar_prefetch=1, grid=(nb, MLA_HEADS),
            in_specs=[pl.BlockSpec((t, MLA_QK_PAD), lambda b, h, l: (rb0 + b, h)),
                      pl.BlockSpec((past, HEAD_W), lambda b, h, l: (b, 2 * h)),
                      pl.BlockSpec((None, None, past, HEAD_W), lambda b, h, l: (l[0], b, 0, 0)),
                      pl.BlockSpec((past, HEAD_W), lambda b, h, l: (b, 2 * h + 1)),
                      pl.BlockSpec((t, HEAD_W), lambda b, h, l: (rb0 + b, 2 * h)),
                      pl.BlockSpec((t, HEAD_W), lambda b, h, l: (rb0 + b, 0)),
                      pl.BlockSpec((t, HEAD_W), lambda b, h, l: (rb0 + b, 2 * h + 1))],
            out_specs=pl.BlockSpec((t, HEAD_W), lambda b, h, l: (b, h))),
        out_shape=jax.ShapeDtypeStruct((nb * t, MLA_HEADS * HEAD_W), BF16),
        compiler_params=_params(blk), name="mla_sample",
    )(lidx, qcat, kv_past, kr_cache128, kv_past, kv_new, kr_new128, kv_new)


def _gate_merge_kernel(l_ref, xn_ref, oa_ref, ob_ref, oc_ref, wga_ref, wgb_ref, wgc_ref,
                       wa_ref, wb_ref, wc_ref, o_ref):
    del l_ref
    xn = xn_ref[...]

    def branch(wg_ref, mix_ref, w_ref):
        return jax.nn.sigmoid(_dot(xn, wg_ref[...])) * _dot(mix_ref[...], w_ref[...])

    merged = branch(wga_ref, oa_ref, wa_ref) + branch(wgb_ref, ob_ref, wb_ref) + branch(wgc_ref, oc_ref, wc_ref)
    o_ref[...] = merged.astype(o_ref.dtype)


def _gate_merge(xn, oa, ob, oc, w_gate, w_a, w_b, w_c, lidx):
    m, d = xn.shape
    tm = _tile(m, 640, V7X_BF16_SUBLANES)
    tn = _tile(d, 256, V7X_LANES)
    nj = d // tn

    def rows(width):
        return pl.BlockSpec((tm, width), lambda i, j, l: (i, 0))

    def gate(branch):
        return pl.BlockSpec((None, d, tn), lambda i, j, l: (l[0], 0, branch * nj + j))

    def proj(kdim):
        return pl.BlockSpec((None, kdim, tn), lambda i, j, l: (l[0], 0, j))

    wa, wb, wc = oa.shape[1], ob.shape[1], oc.shape[1]
    blk = (_nbytes((tm, d + wa + wb + wc), BF16) + _nbytes((3 * d + wa + wb + wc, tn), BF16)
           + 4 * _nbytes((tm, tn), F32))
    return pl.pallas_call(
        _gate_merge_kernel,
        grid_spec=pltpu.PrefetchScalarGridSpec(
            num_scalar_prefetch=1, grid=(m // tm, nj),
            in_specs=[rows(d), rows(wa), rows(wb), rows(wc), gate(0), gate(1), gate(2),
                      proj(wa), proj(wb), proj(wc)],
            out_specs=pl.BlockSpec((tm, tn), lambda i, j, l: (i, j))),
        out_shape=jax.ShapeDtypeStruct((m, d), BF16),
        compiler_params=_params(blk), name="gate_merge",
    )(lidx, xn, oa, ob, oc, w_gate, w_gate, w_gate, w_a, w_b, w_c)


def _swiglu_kernel(l_ref, x_ref, w1_ref, w3_ref, o_ref):
    del l_ref
    x = x_ref[...]
    o_ref[...] = (jax.nn.silu(_dot(x, w1_ref[...])) * _dot(x, w3_ref[...])).astype(o_ref.dtype)


def _swiglu(xn, w1, w3, lidx):
    m, d = xn.shape
    f = w1.shape[2]
    tm = _tile(m, 1040, V7X_BF16_SUBLANES)
    tn = _tile(f, 256, V7X_LANES)
    wspec = pl.BlockSpec((None, d, tn), lambda i, j, l: (l[0], 0, j))
    blk = _nbytes((tm, d), BF16) + 2 * _nbytes((d, tn), BF16) + 4 * _nbytes((tm, tn), F32)
    return pl.pallas_call(
        _swiglu_kernel,
        grid_spec=pltpu.PrefetchScalarGridSpec(
            num_scalar_prefetch=1, grid=(m // tm, f // tn),
            in_specs=[pl.BlockSpec((tm, d), lambda i, j, l: (i, 0)), wspec, wspec],
            out_specs=pl.BlockSpec((tm, tn), lambda i, j, l: (i, j))),
        out_shape=jax.ShapeDtypeStruct((m, f), BF16),
        compiler_params=_params(blk), name="swiglu",
    )(lidx, xn, w1, w3)


def _rope_tables(pos):
    inv_freq = ROPE_THETA ** (-jnp.arange(0, MLA_ROPE, 2, dtype=F32) / MLA_ROPE)
    ang = pos.astype(F32)[:, None] * inv_freq[None, :]
    c, s = jnp.cos(ang), jnp.sin(ang)
    one, zero = jnp.ones_like(c), jnp.zeros_like(c)
    cos128 = jnp.concatenate([c, c, c, c], axis=-1)
    sin128 = jnp.concatenate([-s, s, -s, s], axis=-1)
    cos256 = jnp.concatenate([one, one, one, one, c, c, one, one], axis=-1)
    sin256 = jnp.concatenate([zero, zero, zero, zero, -s, s, zero, zero], axis=-1)
    return cos128, sin128, cos256, sin256


def kernel(x_prompt, x_sample, cache_diff_k, cache_diff_v, cache_sb_k, cache_sb_v, cache_mla_latent,
           cache_mla_krope, attn_norm, w_in, diff_lambda, diff_subln, mla_q_norm, mla_w_uq, mla_kv_norm,
           mla_w_ukv, w_gate, w_branch_a, w_branch_b, w_branch_c, w_out, ffn_norm, ffn_w1, ffn_w3, ffn_w2,
           final_norm):
    nbp, tp, d = x_prompt.shape
    nbs, ts = x_sample.shape[:2]
    past = cache_diff_k.shape[3]
    depth = w_in.shape[0]
    mp, ms = nbp * tp, nbs * ts
    m = mp + ms

    h0 = jnp.concatenate([x_prompt.reshape(mp, d), x_sample.reshape(ms, d)], axis=0)
    pos = jnp.concatenate([jnp.tile(jnp.arange(tp, dtype=jnp.int32), nbp),
                           jnp.tile(past + jnp.arange(ts, dtype=jnp.int32), nbs)])
    cos128, sin128, cos256, sin256 = _rope_tables(pos)

    w_in_main = w_in[:, :, :COL_KR].astype(BF16)
    w_kr_pad = jnp.pad(w_in[:, :, COL_KR:], ((0, 0), (0, 0), (0, HEAD_W - MLA_ROPE))).astype(BF16)
    uq = mla_w_uq.reshape(depth, MLA_Q_LORA, MLA_HEADS, HEAD_W + MLA_ROPE)
    w_uq_pad = jnp.pad(uq, ((0, 0), (0, 0), (0, 0), (0, MLA_QK_PAD - HEAD_W - MLA_ROPE))).reshape(
        depth, MLA_Q_LORA, MLA_HEADS * MLA_QK_PAD).astype(BF16)
    w_ukv = mla_w_ukv.astype(BF16)
    w_gate_b, w_a, w_b, w_c = (w.astype(BF16) for w in (w_gate, w_branch_a, w_branch_b, w_branch_c))
    w_out_b, w1, w3, w2 = (w.astype(BF16) for w in (w_out, ffn_w1, ffn_w3, ffn_w2))
    kr_cache128 = jnp.pad(cache_mla_krope, ((0, 0), (0, 0), (0, 0), (0, HEAD_W - MLA_ROPE))).astype(BF16)
    lat_cache = cache_mla_latent.reshape(depth, nbs * past, MLA_KV_LORA)

    attn_g = attn_norm.reshape(depth, 1, d)
    ffn_g = ffn_norm.reshape(depth, 1, d)
    gq = mla_q_norm.reshape(depth, 1, MLA_Q_LORA)
    gkv = mla_kv_norm.reshape(depth, 1, MLA_KV_LORA)
    subln = diff_subln.reshape(depth, 1, HEAD_W)
    lam_init = jnp.asarray([0.8 - 0.6 * math.exp(-0.3 * l) for l in range(depth)], F32)
    lam_init = jnp.broadcast_to(lam_init[:, None, None], (depth, 1, HEAD_W))

    def head_bufs(nb, t):
        return tuple(jnp.zeros((depth, nb, DIFF_HEADS, t, HEAD_W), F32) for _ in range(4))

    tt_p = _tile(tp, 512, V7X_BF16_SUBLANES)

    def layer(l, carry):
        h, p_bufs, s_bufs, lat_buf, kr_buf = carry
        lidx = jnp.reshape(l, (1,)).astype(jnp.int32)
        xn = _rmsnorm_rows(h, attn_g, lidx, BF16)
        proj = _matmul(xn, w_in_main, lidx, out_dtype=F32, tm_cap=1040, tn_cap=512, name="in_proj")
        kr_buf, kr128 = _shared_rope_key(xn, w_kr_pad, cos128, sin128, kr_buf, lidx)
        qa_p, qb_p, *p_bufs = _post_heads(proj, cos128, sin128, p_bufs, lidx, row0=0, nb=nbp, t=tp, tt=tt_p)
        qa_s, qb_s, *s_bufs = _post_heads(proj, cos128, sin128, s_bufs, lidx, row0=mp, nb=nbs, t=ts, tt=ts)
        cqn, ckv_b, lat_buf = _post_mla(proj, gq, gkv, lat_buf, lidx)
        qcat = _matmul(cqn, w_uq_pad, lidx, out_dtype=BF16, tm_cap=1040, tn_cap=1024,
                       rope=(cos256, sin256), name="mla_q_up")
        kv = _matmul(ckv_b, w_ukv, lidx, out_dtype=BF16, tm_cap=1040, tn_cap=1024, name="mla_kv_up")
        kv_past = _matmul(lat_cache, w_ukv, lidx, out_dtype=BF16, tm_cap=1024, tn_cap=1024,
                          x_stacked=True, name="mla_kv_up_cache")
        oa_p = _diff_prompt(qa_p, p_bufs[0], p_bufs[1], diff_lambda, subln, lam_init, lidx, nb=nbp, t=tp)
        oa_s = _diff_sample(qa_s, cache_diff_k, cache_diff_v, s_bufs[0], s_bufs[1], diff_lambda, subln,
                            lam_init, lidx, nb=nbs, t=ts, past=past)
        ob_p = _sb_prompt(qb_p, p_bufs[2], p_bufs[3], lidx, nb=nbp, t=tp)
        ob_s = _sb_sample(qb_s, cache_sb_k, cache_sb_v, s_bufs[2], s_bufs[3], lidx, nb=nbs, t=ts, past=past)
        oc_p = _mla_prompt(qcat, kv, kr128, nb=nbp, t=tp)
        oc_s = _mla_sample(qcat, kv_past, kr_cache128, kv, kr128, lidx, nb=nbs, t=ts, past=past, row0=mp)
        oa = jnp.concatenate([oa_p, oa_s], axis=0)
        ob = jnp.concatenate([ob_p, ob_s], axis=0)
        oc = jnp.concatenate([oc_p, oc_s], axis=0)
        merged = _gate_merge(xn, oa, ob, oc, w_gate_b, w_a, w_b, w_c, lidx)
        h = _matmul(merged, w_out_b, lidx, out_dtype=F32, tm_cap=1040, tn_cap=512, res=h, name="out_proj")
        xn2 = _rmsnorm_rows(h, ffn_g, lidx, BF16)
        hid = _swiglu(xn2, w1, w3, lidx)
        h = _matmul(hid, w2, lidx, out_dtype=F32, tm_cap=640, tn_cap=256, res=h, name="ffn_down")
        return h, tuple(p_bufs), tuple(s_bufs), lat_buf, kr_buf

    carry = (h0, head_bufs(nbp, tp), head_bufs(nbs, ts),
             jnp.zeros((depth, m, MLA_KV_LORA), F32), jnp.zeros((depth, m, MLA_ROPE), F32))
    h, p_bufs, s_bufs, lat_buf, kr_buf = lax.fori_loop(0, depth, layer, carry)

    y = _rmsnorm_rows(h, final_norm.reshape(1, 1, d), jnp.zeros((1,), jnp.int32), F32)
    y_prompt = y[:mp].reshape(nbp, tp, d)
    y_sample = y[mp:].reshape(nbs, ts, d)
    p_lat = lat_buf[:, :mp].reshape(depth, nbp, tp, MLA_KV_LORA)
    s_lat = lat_buf[:, mp:].reshape(depth, nbs, ts, MLA_KV_LORA)
    p_kr = kr_buf[:, :mp].reshape(depth, nbp, tp, MLA_ROPE)
    s_kr = kr_buf[:, mp:].reshape(depth, nbs, ts, MLA_ROPE)
    return (y_prompt, y_sample, *p_bufs, p_lat, p_kr, *s_bufs, s_lat, s_kr)
```

```python
import functools
import math

import jax
import jax.numpy as jnp
from jax import lax
from jax.experimental import pallas as pl
from jax.experimental.pallas import tpu as pltpu

D_MODEL = 4096
BATCH = 2
SEQ = 4096
DEPTH = 4
DEC_BATCH = 8
DEC_SEQ = 16
PAST_LEN = 2048

CHUNK = 64
ROPE_THETA = 10000.0
NORM_EPS = 1e-6
NEG_INF = -1e30

DIFF_HEADS = 8
DIFF_DH = 64
SB_HEADS = 8
MLA_HEADS = 16
MLA_ROPE = 64
MLA_Q_LORA = 1024
MLA_KV_LORA = 512
HEAD_W = 128
MLA_QK_PAD = 2 * HEAD_W
N_BRANCHES = 3
COL_QA, COL_KA, COL_VA, COL_QB, COL_KB, COL_VB, COL_CQ, COL_CKV, COL_KR = (
    0, 1024, 2048, 3072, 4096, 5120, 6144, 7168, 7680)
FFN_HIDDEN = -(-8 * D_MODEL // 768) * 256

F32 = jnp.float32
BF16 = jnp.bfloat16

V7X_LANES = 128
V7X_BF16_SUBLANES = 16
V7X_VMEM_LIMIT_CAP = 60 * 1024 * 1024


def _tile(n, cap, mult):
    best = None
    for t in range(mult, min(n, cap) + 1, mult):
        if n % t == 0:
            best = t
    if best is None:
        raise ValueError(f"no tile for {n} (cap {cap}, multiple of {mult})")
    return best


def _params(block_bytes):
    need = 2 * block_bytes + (8 << 20)
    return pltpu.CompilerParams(vmem_limit_bytes=int(min(max(need, 32 << 20), V7X_VMEM_LIMIT_CAP)))


def _nbytes(shape, dtype):
    return math.prod(shape) * jnp.dtype(dtype).itemsize


def _dot(a, b):
    return jnp.dot(a, b, preferred_element_type=F32)


def _dot_nt(a, b):
    return lax.dot_general(a, b, (((1,), (1,)), ((), ())), preferred_element_type=F32)


def _rms(x, g):
    return x * lax.rsqrt(jnp.mean(x * x, axis=-1, keepdims=True) + NORM_EPS) * g


def _rope_lanes(x, cos, sin):
    lane = lax.broadcasted_iota(jnp.int32, x.shape, 1)
    first_half = (lane % 64) < 32
    partner = jnp.where(first_half, pltpu.roll(x, x.shape[1] - 32, 1), pltpu.roll(x, 32, 1))
    return x * cos + partner * sin


def _rmsnorm_kernel(l_ref, x_ref, g_ref, o_ref):
    del l_ref
    o_ref[...] = _rms(x_ref[...], g_ref[...]).astype(o_ref.dtype)


def _rmsnorm_rows(x, g_stack, lidx, out_dtype, row0=0, rows=None):
    d = x.shape[1]
    m = x.shape[0] if rows is None else rows
    tr = _tile(math.gcd(m, row0) if row0 else m, 320, V7X_BF16_SUBLANES)
    rb0 = row0 // tr
    blk = _nbytes((tr, d), F32) + _nbytes((tr, d), out_dtype)
    return pl.pallas_call(
        _rmsnorm_kernel,
        grid_spec=pltpu.PrefetchScalarGridSpec(
            num_scalar_prefetch=1, grid=(m // tr,),
            in_specs=[pl.BlockSpec((tr, d), lambda i, l: (rb0 + i, 0)),
                      pl.BlockSpec((None, 1, d), lambda i, l: (l[0], 0, 0))],
            out_specs=pl.BlockSpec((tr, d), lambda i, l: (i, 0))),
        out_shape=jax.ShapeDtypeStruct((m, d), out_dtype),
        compiler_params=_params(blk), name="rmsnorm_rows",
    )(lidx, x, g_stack)


def _mm_kernel(l_ref, x_ref, w_ref, *rest, nk, has_res, rope):
    del l_ref
    rest = list(rest)
    r_ref = rest.pop(0) if has_res else None
    cos_ref, sin_ref = (rest.pop(0), rest.pop(0)) if rope else (None, None)
    o_ref = rest.pop(0)
    part = _dot(x_ref[...].astype(BF16), w_ref[...].astype(BF16))

    def finish(acc):
        if has_res:
            acc = acc + r_ref[...]
        if rope:
            cos, sin = cos_ref[...], sin_ref[...]
            gw = cos.shape[1]
            for g in range(acc.shape[1] // gw):
                seg = acc[:, g * gw:(g + 1) * gw]
                o_ref[:, g * gw:(g + 1) * gw] = _rope_lanes(seg, cos, sin).astype(o_ref.dtype)
        else:
            o_ref[...] = acc.astype(o_ref.dtype)

    if nk == 1:
        finish(part)
    else:
        acc_ref, = rest
        k = pl.program_id(2)

        @pl.when(k == 0)
        def _():
            acc_ref[...] = part

        @pl.when(k > 0)
        def _():
            acc_ref[...] += part

        @pl.when(k == nk - 1)
        def _():
            finish(acc_ref[...])


def _matmul(x, w_stack, lidx, *, out_dtype, tm_cap, tn_cap, tk=None, res=None, rope=None,
            x_stacked=False, n_cols=None, name="matmul"):
    m, kdim = x.shape[-2:]
    n = w_stack.shape[2] if n_cols is None else n_cols
    tm = _tile(m, tm_cap, V7X_BF16_SUBLANES)
    tn = _tile(n, tn_cap, V7X_LANES) if n % V7X_LANES == 0 else n
    tk = kdim if tk is None else tk
    nk = kdim // tk
    if x_stacked:
        x_spec = pl.BlockSpec((None, tm, tk), lambda i, j, k, l: (l[0], i, k))
    else:
        x_spec = pl.BlockSpec((tm, tk), lambda i, j, k, l: (i, k))
    in_specs = [x_spec, pl.BlockSpec((None, tk, tn), lambda i, j, k, l: (l[0], k, j))]
    args = [x, w_stack]
    blk = _nbytes((tm, tk), x.dtype) + _nbytes((tk, tn), w_stack.dtype) + _nbytes((tm, tn), F32)
    if res is not None:
        in_specs.append(pl.BlockSpec((tm, tn), lambda i, j, k, l: (i, j)))
        args.append(res)
        blk += _nbytes((tm, tn), F32)
    if rope is not None:
        gw = rope[0].shape[1]
        assert tn % gw == 0
        in_specs += [pl.BlockSpec((tm, gw), lambda i, j, k, l: (i, 0))] * 2
        args += list(rope)
        blk += 2 * _nbytes((tm, gw), F32)
    return pl.pallas_call(
        functools.partial(_mm_kernel, nk=nk, has_res=res is not None, rope=rope is not None),
        grid_spec=pltpu.PrefetchScalarGridSpec(
            num_scalar_prefetch=1, grid=(m // tm, n // tn, nk),
            in_specs=in_specs,
            out_specs=pl.BlockSpec((tm, tn), lambda i, j, k, l: (i, j)),
            scratch_shapes=[pltpu.VMEM((tm, tn), F32)] if nk > 1 else []),
        out_shape=jax.ShapeDtypeStruct((m, n), out_dtype),
        compiler_params=_params(blk), name=name,
    )(lidx, *args)


def _kr_kernel(l_ref, xn_ref, w_ref, cos_ref, sin_ref, kr_in, kr_o, kr128_o):
    del l_ref, kr_in
    r = _rope_lanes(_dot(xn_ref[...], w_ref[...]), cos_ref[...], sin_ref[...])
    kr_o[...] = r[:, :MLA_ROPE]
    kr128_o[...] = r.astype(BF16)


def _shared_rope_key(xn, w_kr_pad, cos, sin, kr_buf, lidx):
    m, d = xn.shape
    tm = _tile(m, 640, V7X_BF16_SUBLANES)
    blk = _nbytes((tm, d), BF16) + _nbytes((d, HEAD_W), BF16) + 4 * _nbytes((tm, HEAD_W), F32)
    return pl.pallas_call(
        _kr_kernel,
        grid_spec=pltpu.PrefetchScalarGridSpec(
            num_scalar_prefetch=1, grid=(m // tm,),
            in_specs=[pl.BlockSpec((tm, d), lambda i, l: (i, 0)),
                      pl.BlockSpec((None, d, HEAD_W), lambda i, l: (l[0], 0, 0)),
                      pl.BlockSpec((tm, HEAD_W), lambda i, l: (i, 0)),
                      pl.BlockSpec((tm, HEAD_W), lambda i, l: (i, 0)),
                      pl.BlockSpec(memory_space=pl.ANY)],
            out_specs=[pl.BlockSpec((None, tm, MLA_ROPE), lambda i, l: (l[0], i, 0)),
                       pl.BlockSpec((tm, HEAD_W), lambda i, l: (i, 0))]),
        out_shape=[jax.ShapeDtypeStruct(kr_buf.shape, F32), jax.ShapeDtypeStruct((m, HEAD_W), BF16)],
        input_output_aliases={5: 0},
        compiler_params=_params(blk), name="shared_rope_key",
    )(lidx, xn, w_kr_pad, cos, sin, kr_buf)


def _post_heads_kernel(l_ref, qa_ref, ka_ref, va_ref, qb_ref, kb_ref, vb_ref, cos_ref, sin_ref,
                       dk_in, dv_in, sk_in, sv_in, qa_o, qb_o, dk_o, dv_o, sk_o, sv_o):
    del l_ref, dk_in, dv_in, sk_in, sv_in
    cos, sin = cos_ref[...], sin_ref[...]
    qa_o[...] = _rope_lanes(qa_ref[...], cos, sin).astype(BF16)
    dk_o[...] = _rope_lanes(ka_ref[...], cos, sin)
    dv_o[...] = va_ref[...]
    qb_o[...] = qb_ref[...].astype(BF16)
    sk_o[...] = kb_ref[...]
    sv_o[...] = vb_ref[...]


def _post_heads(proj, cos, sin, bufs, lidx, *, row0, nb, t, tt):
    nt = t // tt
    rb0 = row0 // tt
    assert row0 % tt == 0

    def col(cb):
        return pl.BlockSpec((tt, HEAD_W), lambda b, ti, h, l: (rb0 + b * nt + ti, cb // HEAD_W + h))

    tab = pl.BlockSpec((tt, HEAD_W), lambda b, ti, h, l: (rb0 + b * nt + ti, 0))
    qspec = pl.BlockSpec((tt, HEAD_W), lambda b, ti, h, l: (b * nt + ti, h))
    bspec = pl.BlockSpec((None, None, None, tt, HEAD_W), lambda b, ti, h, l: (l[0], b, h, ti, 0))
    anyspec = pl.BlockSpec(memory_space=pl.ANY)
    qshape = jax.ShapeDtypeStruct((nb * t, DIFF_HEADS * HEAD_W), BF16)
    blk = 8 * _nbytes((tt, HEAD_W), F32) + 6 * _nbytes((tt, HEAD_W), F32)
    return pl.pallas_call(
        _post_heads_kernel,
        grid_spec=pltpu.PrefetchScalarGridSpec(
            num_scalar_prefetch=1, grid=(nb, nt, DIFF_HEADS),
            in_specs=[col(COL_QA), col(COL_KA), col(COL_VA), col(COL_QB), col(COL_KB), col(COL_VB),
                      tab, tab, anyspec, anyspec, anyspec, anyspec],
            out_specs=[qspec, qspec, bspec, bspec, bspec, bspec]),
        out_shape=[qshape, qshape] + [jax.ShapeDtypeStruct(b.shape, F32) for b in bufs],
        input_output_aliases={9: 2, 10: 3, 11: 4, 12: 5},
        compiler_params=_params(blk), name="post_heads",
    )(lidx, proj, proj, proj, proj, proj, proj, cos, sin, *bufs)


def _post_mla_kernel(l_ref, cq_ref, ckv_ref, gq_ref, gkv_ref, lat_in, cqn_o, ckvb_o, lat_o):
    del l_ref, lat_in
    cqn_o[...] = _rms(cq_ref[...], gq_ref[...]).astype(BF16)
    c = _rms(ckv_ref[...], gkv_ref[...])
    lat_o[...] = c
    ckvb_o[...] = c.astype(BF16)


def _post_mla(proj, gq, gkv, lat_buf, lidx):
    m = proj.shape[0]
    tt = _tile(m, 640, V7X_BF16_SUBLANES)
    blk = 3 * _nbytes((tt, MLA_Q_LORA + MLA_KV_LORA), F32)
    return pl.pallas_call(
        _post_mla_kernel,
        grid_spec=pltpu.PrefetchScalarGridSpec(
            num_scalar_prefetch=1, grid=(m // tt,),
            in_specs=[pl.BlockSpec((tt, MLA_Q_LORA), lambda i, l: (i, COL_CQ // MLA_Q_LORA)),
                      pl.BlockSpec((tt, MLA_KV_LORA), lambda i, l: (i, COL_CKV // MLA_KV_LORA)),
                      pl.BlockSpec((None, 1, MLA_Q_LORA), lambda i, l: (l[0], 0, 0)),
                      pl.BlockSpec((None, 1, MLA_KV_LORA), lambda i, l: (l[0], 0, 0)),
                      pl.BlockSpec(memory_space=pl.ANY)],
            out_specs=[pl.BlockSpec((tt, MLA_Q_LORA), lambda i, l: (i, 0)),
                       pl.BlockSpec((tt, MLA_KV_LORA), lambda i, l: (i, 0)),
                       pl.BlockSpec((None, tt, MLA_KV_LORA), lambda i, l: (l[0], i, 0))]),
        out_shape=[jax.ShapeDtypeStruct((m, MLA_Q_LORA), BF16),
                   jax.ShapeDtypeStruct((m, MLA_KV_LORA), BF16),
                   jax.ShapeDtypeStruct(lat_buf.shape, F32)],
        input_output_aliases={5: 2},
        compiler_params=_params(blk), name="post_mla",
    )(lidx, proj, proj, gq, gkv, lat_buf)


LOG2E = math.log2(math.e)
SB_DEAD_LOG = -104.0


def _with_ones(v):
    return jnp.concatenate([v, jnp.ones(v.shape, v.dtype)], axis=1)


def _softmax_step(s2, v_ext, m_ref, acc_ref):
    m_prev = m_ref[...]
    m_new = jnp.maximum(m_prev, jnp.max(s2, axis=-1, keepdims=True))
    alpha = jnp.exp2(m_prev - m_new)
    lanes = m_prev.shape[1]
    p = jnp.concatenate([jnp.exp2(s2[:, c * lanes:(c + 1) * lanes] - m_new).astype(BF16)
                         for c in range(s2.shape[1] // lanes)], axis=1)
    acc_ref[...] = jnp.concatenate([alpha, alpha], axis=1) * acc_ref[...] + _dot(p, v_ext)
    m_ref[...] = m_new


def _softmax_init(m_ref, acc_ref):
    m_ref[...] = jnp.full(m_ref.shape, NEG_INF, F32)
    acc_ref[...] = jnp.zeros(acc_ref.shape, F32)


def _softmax_result(acc_ref):
    acc = acc_ref[...]
    return acc[:, :HEAD_W] / acc[:, HEAD_W:]


def _positions(shape, q0, k0):
    qpos = q0 + lax.broadcasted_iota(jnp.int32, shape, 0)
    kpos = k0 + lax.broadcasted_iota(jnp.int32, shape, 1)
    return qpos, kpos


def _chunk_mask(shape, q0, k0):
    qpos, kpos = _positions(shape, q0, k0)
    return (kpos // CHUNK) <= (qpos // CHUNK)


def _diff_lambda(lam_ref, li_ref):
    lv = lam_ref[...]
    lam_init = li_ref[:, 0:1]
    d1 = jnp.sum(lv[0:1, :] * lv[1:2, :], axis=-1, keepdims=True)
    d2 = jnp.sum(lv[2:3, :] * lv[3:4, :], axis=-1, keepdims=True)
    return jnp.exp(d1) - jnp.exp(d2) + lam_init, lam_init


def _split_components(q):
    lane = lax.broadcasted_iota(jnp.int32, q.shape, 1)
    zero = jnp.zeros_like(q)
    return jnp.where(lane < DIFF_DH, q, zero), jnp.where(lane >= DIFF_DH, q, zero)


def _softplus(z):
    return jnp.maximum(z, 0.0) + jnp.log1p(jnp.exp(-jnp.abs(z)))


def _upper_ones(n):
    r = lax.broadcasted_iota(jnp.int32, (n, n), 0)
    c = lax.broadcasted_iota(jnp.int32, (n, n), 1)
    return jnp.where(r > c, 1.0, 0.0).astype(BF16)


def _sb_block(z, mask, carry, v, tri):
    sp = _softplus(z)
    log_stay = -sp if mask is None else jnp.where(mask, -sp, 0.0)
    hi = log_stay.astype(BF16)
    lo = (log_stay - hi.astype(F32)).astype(BF16)
    between = _dot(hi, tri) + _dot(lo, tri) + carry
    a = jnp.exp(z - sp + between)
    if mask is not None:
        a = jnp.where(mask, a, 0.0)
    return _dot(a.astype(BF16), v), carry + jnp.sum(log_stay, axis=-1, keepdims=True)


def _attn_tiles(t):
    tq = _tile(t, 512, CHUNK)
    return tq, tq


def _diff_prompt_kernel(l_ref, q_ref, k_ref, v_ref, lam_ref, g_ref, li_ref, o_ref, m1, a1, m2, a2, s1_ref, s2_ref,
                        *, tq, tk):
    del l_ref
    assert tq == tk
    qi = pl.program_id(2)
    q1, q2 = _split_components(q_ref[...])
    scale2 = DIFF_DH ** -0.5 * LOG2E
    _softmax_init(m1, a1)
    _softmax_init(m2, a2)

    def store_logits(j):
        kb = k_ref[pl.ds(pl.multiple_of(j * tk, tk), tk), :].astype(BF16)
        s1_ref[...] = _dot_nt(q1, kb) * scale2
        s2_ref[...] = _dot_nt(q2, kb) * scale2

    def values(j):
        return _with_ones(v_ref[pl.ds(pl.multiple_of(j * tk, tk), tk), :].astype(BF16))

    store_logits(0)

    def body(j, c):
        v_ext = values(j)
        _softmax_step(s1_ref[...], v_ext, m1, a1)
        _softmax_step(s2_ref[...], v_ext, m2, a2)
        store_logits(j + 1)
        return c

    lax.fori_loop(0, qi, body, 0)
    mask = _chunk_mask((tq, tk), qi * tq, qi * tk)
    v_ext = values(qi)
    _softmax_step(jnp.where(mask, s1_ref[...], NEG_INF), v_ext, m1, a1)
    _softmax_step(jnp.where(mask, s2_ref[...], NEG_INF), v_ext, m2, a2)
    lam, lam_init = _diff_lambda(lam_ref, li_ref)
    o = _softmax_result(a1) - lam * _softmax_result(a2)
    o_ref[...] = (_rms(o, g_ref[...]) * (1.0 - lam_init)).astype(o_ref.dtype)


def _diff_prompt(q, k_buf, v_buf, lam, g, li, lidx, *, nb, t, rows_total):
    tq, tk = _attn_tiles(t)
    nq = t // tq
    kv_spec = pl.BlockSpec((None, None, None, t, HEAD_W), lambda b, h, i, l: (l[0], b, h, 0, 0))
    blk = 2 * _nbytes((t, HEAD_W), F32) + 12 * _nbytes((tq, tk), F32)
    return pl.pallas_call(
        functools.partial(_diff_prompt_kernel, tq=tq, tk=tk),
        grid_spec=pltpu.PrefetchScalarGridSpec(
            num_scalar_prefetch=1, grid=(nb, DIFF_HEADS, nq),
            in_specs=[pl.BlockSpec((tq, HEAD_W), lambda b, h, i, l: (b * nq + i, h)),
                      kv_spec, kv_spec,
                      pl.BlockSpec((None, 4, DIFF_DH), lambda b, h, i, l: (l[0], 0, 0)),
                      pl.BlockSpec((None, 1, HEAD_W), lambda b, h, i, l: (l[0], 0, 0)),
                      pl.BlockSpec((None, 1, HEAD_W), lambda b, h, i, l: (l[0], 0, 0))],
            out_specs=pl.BlockSpec((tq, HEAD_W), lambda b, h, i, l: (b * nq + i, h)),
            scratch_shapes=[pltpu.VMEM((tq, HEAD_W), F32), pltpu.VMEM((tq, 2 * HEAD_W), F32),
                            pltpu.VMEM((tq, HEAD_W), F32), pltpu.VMEM((tq, 2 * HEAD_W), F32),
                            pltpu.VMEM((tq, tk), F32), pltpu.VMEM((tq, tk), F32)]),
        out_shape=jax.ShapeDtypeStruct((rows_total, DIFF_HEADS * HEAD_W), BF16),
        compiler_params=_params(blk), name="diff_prompt",
    )(lidx, q, k_buf, v_buf, lam, g, li)


def _sb_prompt_kernel(l_ref, q_ref, k_ref, v_ref, o_ref, c_ref, acc_ref, *, tq, tk):
    del l_ref
    qi = pl.program_id(2)
    ratio = tq // tk
    q = q_ref[...]
    scale = HEAD_W ** -0.5
    tri = _upper_ones(tk)
    c_ref[...] = jnp.zeros(c_ref.shape, F32)
    acc_ref[...] = jnp.zeros(acc_ref.shape, F32)

    def block(j, masked):
        start = pl.multiple_of(j * tk, tk)
        kb = k_ref[pl.ds(start, tk), :].astype(BF16)
        vb = v_ref[pl.ds(start, tk), :].astype(BF16)
        z = _dot_nt(q, kb) * scale
        mask = None
        if masked:
            qpos, kpos = _positions(z.shape, qi * tq, start)
            mask = kpos < qpos
        out, carry = _sb_block(z, mask, c_ref[...], vb, tri)
        acc_ref[...] += out
        c_ref[...] = carry

    for u in reversed(range(ratio)):
        block(qi * ratio + u, True)

    def alive():
        return jnp.max(c_ref[...]) > SB_DEAD_LOG

    def cond(state):
        j, live = state
        return jnp.logical_and(j >= 0, live)

    def body(state):
        j, _ = state
        block(j, False)
        return j - 1, alive()

    lax.while_loop(cond, body, (qi * ratio - 1, alive()))
    o_ref[...] = acc_ref[...].astype(o_ref.dtype)


def _sb_prompt(q, k_buf, v_buf, lidx, *, nb, t, rows_total):
    tq, _ = _attn_tiles(t)
    tk = _tile(tq, 256, CHUNK)
    nq = t // tq
    kv_spec = pl.BlockSpec((None, None, None, t, HEAD_W), lambda b, h, i, l: (l[0], b, h, 0, 0))
    blk = 2 * _nbytes((t, HEAD_W), F32) + 16 * _nbytes((tq, tk), F32)
    return pl.pallas_call(
        functools.partial(_sb_prompt_kernel, tq=tq, tk=tk),
        grid_spec=pltpu.PrefetchScalarGridSpec(
            num_scalar_prefetch=1, grid=(nb, SB_HEADS, nq),
            in_specs=[pl.BlockSpec((tq, HEAD_W), lambda b, h, i, l: (b * nq + i, h)), kv_spec, kv_spec],
            out_specs=pl.BlockSpec((tq, HEAD_W), lambda b, h, i, l: (b * nq + i, h)),
            scratch_shapes=[pltpu.VMEM((tq, 1), F32), pltpu.VMEM((tq, HEAD_W), F32)]),
        out_shape=jax.ShapeDtypeStruct((rows_total, SB_HEADS * HEAD_W), BF16),
        compiler_params=_params(blk), name="sb_prompt",
    )(lidx, q, k_buf, v_buf)


def _mla_prompt_kernel(q_ref, kn_ref, kr_ref, v_ref, o_ref, m_ref, acc_ref, s_ref, *, tq, tk):
    assert tq == tk
    qi = pl.program_id(2)
    q = q_ref[...]
    scale2 = (HEAD_W + MLA_ROPE) ** -0.5 * LOG2E
    _softmax_init(m_ref, acc_ref)

    def logits(j):
        start = pl.multiple_of(j * tk, tk)
        kcat = jnp.concatenate([kn_ref[pl.ds(start, tk), :], kr_ref[pl.ds(start, tk), :]], axis=1)
        return _dot_nt(q, kcat) * scale2

    def values(j):
        return _with_ones(v_ref[pl.ds(pl.multiple_of(j * tk, tk), tk), :])

    s_ref[...] = logits(0)

    def body(j, c):
        _softmax_step(s_ref[...], values(j), m_ref, acc_ref)
        s_ref[...] = logits(j + 1)
        return c

    lax.fori_loop(0, qi, body, 0)
    s2 = jnp.where(_chunk_mask((tq, tk), qi * tq, qi * tk), s_ref[...], NEG_INF)
    _softmax_step(s2, values(qi), m_ref, acc_ref)
    o_ref[...] = _softmax_result(acc_ref).astype(o_ref.dtype)


def _mla_prompt(qcat, kv, kr128, *, nb, t, rows_total):
    tq, tk = _attn_tiles(t)
    nq = t // tq
    blk = 3 * _nbytes((t, HEAD_W), BF16) + 12 * _nbytes((tq, tk), F32)
    return pl.pallas_call(
        functools.partial(_mla_prompt_kernel, tq=tq, tk=tk),
        grid=(nb, MLA_HEADS, nq),
        in_specs=[pl.BlockSpec((tq, MLA_QK_PAD), lambda b, h, i: (b * nq + i, h)),
                  pl.BlockSpec((t, HEAD_W), lambda b, h, i: (b, 2 * h)),
                  pl.BlockSpec((t, HEAD_W), lambda b, h, i: (b, 0)),
                  pl.BlockSpec((t, HEAD_W), lambda b, h, i: (b, 2 * h + 1))],
        out_specs=pl.BlockSpec((tq, HEAD_W), lambda b, h, i: (b * nq + i, h)),
        scratch_shapes=[pltpu.VMEM((tq, HEAD_W), F32), pltpu.VMEM((tq, 2 * HEAD_W), F32),
                        pltpu.VMEM((tq, tk), F32)],
        out_shape=jax.ShapeDtypeStruct((rows_total, MLA_HEADS * HEAD_W), BF16),
        compiler_params=_params(blk), name="mla_prompt",
    )(qcat, kv, kr128, kv)


def _pad_rows(x, rows):
    return jnp.concatenate([x, jnp.zeros((rows - x.shape[0], x.shape[1]), x.dtype)], axis=0)


def _two_part_softmax(sp, sn, vp, vn):
    m = jnp.maximum(jnp.max(sp, axis=-1, keepdims=True), jnp.max(sn, axis=-1, keepdims=True))
    pp, pn = jnp.exp(sp - m), jnp.exp(sn - m)
    denom = jnp.sum(pp, axis=-1, keepdims=True) + jnp.sum(pn, axis=-1, keepdims=True)
    return (_dot(pp.astype(BF16), vp) + _dot(pn.astype(BF16), vn)) / denom


def _diff_sample_kernel(l_ref, q_ref, kp_ref, vp_ref, kn_ref, vn_ref, lam_ref, g_ref, li_ref, prev_ref, o_ref,
                        *, past, t):
    del l_ref, prev_ref
    q1, q2 = _split_components(q_ref[...])
    scale = DIFF_DH ** -0.5
    kp, vp = kp_ref[...].astype(BF16), vp_ref[...].astype(BF16)
    kn = _pad_rows(kn_ref[...], HEAD_W).astype(BF16)
    vn = _pad_rows(vn_ref[...], HEAD_W).astype(BF16)
    mask_p = _chunk_mask((t, past), past, 0)
    qpos, kpos = _positions((t, HEAD_W), past, past)
    mask_n = ((kpos // CHUNK) <= (qpos // CHUNK)) & (kpos < past + t)

    def attend(qx):
        sp = jnp.where(mask_p, _dot_nt(qx, kp) * scale, NEG_INF)
        sn = jnp.where(mask_n, _dot_nt(qx, kn) * scale, NEG_INF)
        return _two_part_softmax(sp, sn, vp, vn)

    lam, lam_init = _diff_lambda(lam_ref, li_ref)
    o = attend(q1) - lam * attend(q2)
    o_ref[...] = (_rms(o, g_ref[...]) * (1.0 - lam_init)).astype(o_ref.dtype)


def _diff_sample(q, k_cache, v_cache, k_buf, v_buf, lam, g, li, merged, lidx, *, nb, t, past, row0):
    rb0 = row0 // t
    cache_spec = pl.BlockSpec((None, None, None, past, HEAD_W), lambda b, h, l: (l[0], b, h, 0, 0))
    new_spec = pl.BlockSpec((None, None, None, t, HEAD_W), lambda b, h, l: (l[0], b, h, 0, 0))
    blk = 2 * _nbytes((past, HEAD_W), F32) + 8 * _nbytes((t, past), F32)
    return pl.pallas_call(
        functools.partial(_diff_sample_kernel, past=past, t=t),
        grid_spec=pltpu.PrefetchScalarGridSpec(
            num_scalar_prefetch=1, grid=(nb, DIFF_HEADS),
            in_specs=[pl.BlockSpec((t, HEAD_W), lambda b, h, l: (b, h)),
                      cache_spec, cache_spec, new_spec, new_spec,
                      pl.BlockSpec((None, 4, DIFF_DH), lambda b, h, l: (l[0], 0, 0)),
                      pl.BlockSpec((None, 1, HEAD_W), lambda b, h, l: (l[0], 0, 0)),
                      pl.BlockSpec((None, 1, HEAD_W), lambda b, h, l: (l[0], 0, 0)),
                      pl.BlockSpec(memory_space=pl.ANY)],
            out_specs=pl.BlockSpec((t, HEAD_W), lambda b, h, l: (rb0 + b, h))),
        out_shape=jax.ShapeDtypeStruct(merged.shape, BF16),
        input_output_aliases={9: 0},
        compiler_params=_params(blk), name="diff_sample",
    )(lidx, q, k_cache, v_cache, k_buf, v_buf, lam, g, li, merged)


def _sb_sample_kernel(l_ref, q_ref, kp_ref, vp_ref, kn_ref, vn_ref, prev_ref, o_ref, *, past, t, cw):
    del l_ref, prev_ref
    q = q_ref[...]
    scale = HEAD_W ** -0.5
    kn = _pad_rows(kn_ref[...], HEAD_W).astype(BF16)
    vn = _pad_rows(vn_ref[...], HEAD_W).astype(BF16)
    qpos, kpos = _positions((t, HEAD_W), past, past)
    mask_n = (kpos < qpos) & (kpos < past + t)
    acc, carry = _sb_block(_dot_nt(q, kn) * scale, mask_n, jnp.zeros((t, 1), F32), vn, _upper_ones(HEAD_W))
    tri = _upper_ones(cw)
    for c in reversed(range(past // cw)):
        kb = kp_ref[c * cw:(c + 1) * cw, :].astype(BF16)
        vb = vp_ref[c * cw:(c + 1) * cw, :].astype(BF16)
        qpos, kpos = _positions((t, cw), past, c * cw)
        out, carry = _sb_block(_dot_nt(q, kb) * scale, kpos < qpos, carry, vb, tri)
        acc = acc + out
    o_ref[...] = acc.astype(o_ref.dtype)


def _sb_sample(q, k_cache, v_cache, k_buf, v_buf, merged, lidx, *, nb, t, past, row0):
    rb0 = row0 // t
    cw = _tile(past, 256, V7X_LANES)
    cache_spec = pl.BlockSpec((None, None, None, past, HEAD_W), lambda b, h, l: (l[0], b, h, 0, 0))
    new_spec = pl.BlockSpec((None, None, None, t, HEAD_W), lambda b, h, l: (l[0], b, h, 0, 0))
    blk = 2 * _nbytes((past, HEAD_W), F32) + 8 * _nbytes((t, past), F32)
    return pl.pallas_call(
        functools.partial(_sb_sample_kernel, past=past, t=t, cw=cw),
        grid_spec=pltpu.PrefetchScalarGridSpec(
            num_scalar_prefetch=1, grid=(nb, SB_HEADS),
            in_specs=[pl.BlockSpec((t, HEAD_W), lambda b, h, l: (b, h)),
                      cache_spec, cache_spec, new_spec, new_spec, pl.BlockSpec(memory_space=pl.ANY)],
            out_specs=pl.BlockSpec((t, HEAD_W), lambda b, h, l: (rb0 + b, h))),
        out_shape=jax.ShapeDtypeStruct(merged.shape, BF16),
        input_output_aliases={6: 0},
        compiler_params=_params(blk), name="sb_sample",
    )(lidx, q, k_cache, v_cache, k_buf, v_buf, merged)


def _mla_sample_kernel(l_ref, q_ref, knp_ref, krp_ref, vp_ref, knn_ref, krn_ref, vn_ref, prev_ref, o_ref,
                       *, past, t):
    del l_ref, prev_ref
    q = q_ref[...]
    scale = (HEAD_W + MLA_ROPE) ** -0.5
    kp = jnp.concatenate([knp_ref[...], krp_ref[...]], axis=1)
    kn = _pad_rows(jnp.concatenate([knn_ref[...], krn_ref[...]], axis=1), HEAD_W)
    vn = _pad_rows(vn_ref[...], HEAD_W)
    qpos, kpos = _positions((t, HEAD_W), past, past)
    mask_n = ((kpos // CHUNK) <= (qpos // CHUNK)) & (kpos < past + t)
    sp = jnp.where(_chunk_mask((t, past), past, 0), _dot_nt(q, kp) * scale, NEG_INF)
    sn = jnp.where(mask_n, _dot_nt(q, kn) * scale, NEG_INF)
    o_ref[...] = _two_part_softmax(sp, sn, vp_ref[...], vn).astype(o_ref.dtype)


def _mla_sample(qcat, kv_past, kr_cache128, kv_new, kr_new128, merged, lidx, *, nb, t, past, row0):
    rb0 = row0 // t
    blk = 3 * _nbytes((past, HEAD_W), BF16) + 8 * _nbytes((t, past), F32)
    return pl.pallas_call(
        functools.partial(_mla_sample_kernel, past=past, t=t),
        grid_spec=pltpu.PrefetchScalarGridSpec(
            num_scalar_prefetch=1, grid=(nb, MLA_HEADS),
            in_specs=[pl.BlockSpec((t, MLA_QK_PAD), lambda b, h, l: (rb0 + b, h)),
                      pl.BlockSpec((past, HEAD_W), lambda b, h, l: (b, 2 * h)),
                      pl.BlockSpec((None, None, past, HEAD_W), lambda b, h, l: (l[0], b, 0, 0)),
                      pl.BlockSpec((past, HEAD_W), lambda b, h, l: (b, 2 * h + 1)),
                      pl.BlockSpec((t, HEAD_W), lambda b, h, l: (rb0 + b, 2 * h)),
                      pl.BlockSpec((t, HEAD_W), lambda b, h, l: (rb0 + b, 0)),
                      pl.BlockSpec((t, HEAD_W), lambda b, h, l: (rb0 + b, 2 * h + 1)),
                      pl.BlockSpec(memory_space=pl.ANY)],
            out_specs=pl.BlockSpec((t, HEAD_W), lambda b, h, l: (rb0 + b, h))),
        out_shape=jax.ShapeDtypeStruct(merged.shape, BF16),
        input_output_aliases={8: 0},
        compiler_params=_params(blk), name="mla_sample",
    )(lidx, qcat, kv_past, kr_cache128, kv_past, kv_new, kr_new128, kv_new, merged)


def _gate_merge_kernel(l_ref, xn_ref, oa_ref, ob_ref, oc_ref, wga_ref, wgb_ref, wgc_ref,
                       wa_ref, wb_ref, wc_ref, o_ref):
    del l_ref
    xn = xn_ref[...]

    def branch(wg_ref, mix_ref, w_ref):
        return jax.nn.sigmoid(_dot(xn, wg_ref[...])) * _dot(mix_ref[...], w_ref[...])

    merged = branch(wga_ref, oa_ref, wa_ref) + branch(wgb_ref, ob_ref, wb_ref) + branch(wgc_ref, oc_ref, wc_ref)
    o_ref[...] = merged.astype(o_ref.dtype)


def _gate_merge(xn, oa, ob, oc, w_gate, w_a, w_b, w_c, lidx):
    m, d = xn.shape
    tm = _tile(m, 640, V7X_BF16_SUBLANES)
    tn = _tile(d, 256, V7X_LANES)
    nj = d // tn

    def rows(width):
        return pl.BlockSpec((tm, width), lambda i, j, l: (i, 0))

    def gate(branch):
        return pl.BlockSpec((None, d, tn), lambda i, j, l: (l[0], 0, branch * nj + j))

    def proj(kdim):
        return pl.BlockSpec((None, kdim, tn), lambda i, j, l: (l[0], 0, j))

    wa, wb, wc = oa.shape[1], ob.shape[1], oc.shape[1]
    blk = (_nbytes((tm, d + wa + wb + wc), BF16) + _nbytes((3 * d + wa + wb + wc, tn), BF16)
           + 4 * _nbytes((tm, tn), F32))
    return pl.pallas_call(
        _gate_merge_kernel,
        grid_spec=pltpu.PrefetchScalarGridSpec(
            num_scalar_prefetch=1, grid=(m // tm, nj),
            in_specs=[rows(d), rows(wa), rows(wb), rows(wc), gate(0), gate(1), gate(2),
                      proj(wa), proj(wb), proj(wc)],
            out_specs=pl.BlockSpec((tm, tn), lambda i, j, l: (i, j))),
        out_shape=jax.ShapeDtypeStruct((m, d), BF16),
        compiler_params=_params(blk), name="gate_merge",
    )(lidx, xn, oa, ob, oc, w_gate, w_gate, w_gate, w_a, w_b, w_c)


def _swiglu_kernel(l_ref, x_ref, w1_ref, w3_ref, o_ref):
    del l_ref
    x = x_ref[...]
    o_ref[...] = (jax.nn.silu(_dot(x, w1_ref[...].astype(BF16)))
                  * _dot(x, w3_ref[...].astype(BF16))).astype(o_ref.dtype)


def _swiglu(xn, w1, w3, lidx):
    m, d = xn.shape
    f = w1.shape[2]
    tm = _tile(m, 1040, V7X_BF16_SUBLANES)
    tn = _tile(f, 256, V7X_LANES)
    wspec = pl.BlockSpec((None, d, tn), lambda i, j, l: (l[0], 0, j))
    blk = _nbytes((tm, d), BF16) + 3 * _nbytes((d, tn), w1.dtype) + 4 * _nbytes((tm, tn), F32)
    return pl.pallas_call(
        _swiglu_kernel,
        grid_spec=pltpu.PrefetchScalarGridSpec(
            num_scalar_prefetch=1, grid=(m // tm, f // tn),
            in_specs=[pl.BlockSpec((tm, d), lambda i, j, l: (i, 0)), wspec, wspec],
            out_specs=pl.BlockSpec((tm, tn), lambda i, j, l: (i, j))),
        out_shape=jax.ShapeDtypeStruct((m, f), BF16),
        compiler_params=_params(blk), name="swiglu",
    )(lidx, xn, w1, w3)


def _rope_tables(pos):
    inv_freq = ROPE_THETA ** (-jnp.arange(0, MLA_ROPE, 2, dtype=F32) / MLA_ROPE)
    ang = pos.astype(F32)[:, None] * inv_freq[None, :]
    c, s = jnp.cos(ang), jnp.sin(ang)
    one, zero = jnp.ones_like(c), jnp.zeros_like(c)
    cos128 = jnp.concatenate([c, c, c, c], axis=-1)
    sin128 = jnp.concatenate([-s, s, -s, s], axis=-1)
    cos256 = jnp.concatenate([one, one, one, one, c, c, one, one], axis=-1)
    sin256 = jnp.concatenate([zero, zero, zero, zero, -s, s, zero, zero], axis=-1)
    return cos128, sin128, cos256, sin256


def kernel(x_prompt, x_sample, cache_diff_k, cache_diff_v, cache_sb_k, cache_sb_v, cache_mla_latent,
           cache_mla_krope, attn_norm, w_in, diff_lambda, diff_subln, mla_q_norm, mla_w_uq, mla_kv_norm,
           mla_w_ukv, w_gate, w_branch_a, w_branch_b, w_branch_c, w_out, ffn_norm, ffn_w1, ffn_w3, ffn_w2,
           final_norm):
    nbp, tp, d = x_prompt.shape
    nbs, ts = x_sample.shape[:2]
    past = cache_diff_k.shape[3]
    depth = w_in.shape[0]
    mp, ms = nbp * tp, nbs * ts
    m = mp + ms

    h0 = jnp.concatenate([x_prompt.reshape(mp, d), x_sample.reshape(ms, d)], axis=0)
    pos = jnp.concatenate([jnp.tile(jnp.arange(tp, dtype=jnp.int32), nbp),
                           jnp.tile(past + jnp.arange(ts, dtype=jnp.int32), nbs)])
    cos128, sin128, cos256, sin256 = _rope_tables(pos)

    w_kr_pad = jnp.pad(w_in[:, :, COL_KR:], ((0, 0), (0, 0), (0, HEAD_W - MLA_ROPE))).astype(BF16)
    uq = mla_w_uq.reshape(depth, MLA_Q_LORA, MLA_HEADS, HEAD_W + MLA_ROPE)
    w_uq_pad = jnp.pad(uq, ((0, 0), (0, 0), (0, 0), (0, MLA_QK_PAD - HEAD_W - MLA_ROPE))).reshape(
        depth, MLA_Q_LORA, MLA_HEADS * MLA_QK_PAD)
    w_gate_b, w_a, w_b, w_c = (w.astype(BF16) for w in (w_gate, w_branch_a, w_branch_b, w_branch_c))
    w2 = ffn_w2.astype(BF16)
    kr_cache128 = jnp.pad(cache_mla_krope, ((0, 0), (0, 0), (0, 0), (0, HEAD_W - MLA_ROPE))).astype(BF16)
    lat_cache = cache_mla_latent.reshape(depth, nbs * past, MLA_KV_LORA)

    attn_g = attn_norm.reshape(depth, 1, d)
    ffn_g = ffn_norm.reshape(depth, 1, d)
    gq = mla_q_norm.reshape(depth, 1, MLA_Q_LORA)
    gkv = mla_kv_norm.reshape(depth, 1, MLA_KV_LORA)
    subln = diff_subln.reshape(depth, 1, HEAD_W)
    lam_init = jnp.asarray([0.8 - 0.6 * math.exp(-0.3 * l) for l in range(depth)], F32)
    lam_init = jnp.broadcast_to(lam_init[:, None, None], (depth, 1, HEAD_W))

    def head_bufs(nb, t):
        return tuple(jnp.zeros((depth, nb, DIFF_HEADS, t, HEAD_W), F32) for _ in range(4))

    tt_p = _tile(tp, 512, V7X_BF16_SUBLANES)

    def layer(l, carry):
        h, p_bufs, s_bufs, lat_buf, kr_buf = carry
        lidx = jnp.reshape(l, (1,)).astype(jnp.int32)
        xn = _rmsnorm_rows(h, attn_g, lidx, BF16)
        proj = _matmul(xn, w_in, lidx, out_dtype=F32, tm_cap=1040, tn_cap=512, n_cols=COL_KR, name="in_proj")
        kr_buf, kr128 = _shared_rope_key(xn, w_kr_pad, cos128, sin128, kr_buf, lidx)
        qa_p, qb_p, *p_bufs = _post_heads(proj, cos128, sin128, p_bufs, lidx, row0=0, nb=nbp, t=tp, tt=tt_p)
        qa_s, qb_s, *s_bufs = _post_heads(proj, cos128, sin128, s_bufs, lidx, row0=mp, nb=nbs, t=ts, tt=ts)
        cqn, ckv_b, lat_buf = _post_mla(proj, gq, gkv, lat_buf, lidx)
        qcat = _matmul(cqn, w_uq_pad, lidx, out_dtype=BF16, tm_cap=1040, tn_cap=1024,
                       rope=(cos256, sin256), name="mla_q_up")
        kv = _matmul(ckv_b, mla_w_ukv, lidx, out_dtype=BF16, tm_cap=1040, tn_cap=1024, name="mla_kv_up")
        kv_past = _matmul(lat_cache, mla_w_ukv, lidx, out_dtype=BF16, tm_cap=1024, tn_cap=1024,
                          x_stacked=True, name="mla_kv_up_cache")
        oa = _diff_prompt(qa_p, p_bufs[0], p_bufs[1], diff_lambda, subln, lam_init, lidx, nb=nbp, t=tp, rows_total=m)
        oa = _diff_sample(qa_s, cache_diff_k, cache_diff_v, s_bufs[0], s_bufs[1], diff_lambda, subln,
                          lam_init, oa, lidx, nb=nbs, t=ts, past=past, row0=mp)
        ob = _sb_prompt(qb_p, p_bufs[2], p_bufs[3], lidx, nb=nbp, t=tp, rows_total=m)
        ob = _sb_sample(qb_s, cache_sb_k, cache_sb_v, s_bufs[2], s_bufs[3], ob, lidx, nb=nbs, t=ts, past=past,
                        row0=mp)
        oc = _mla_prompt(qcat, kv, kr128, nb=nbp, t=tp, rows_total=m)
        oc = _mla_sample(qcat, kv_past, kr_cache128, kv, kr128, oc, lidx, nb=nbs, t=ts, past=past, row0=mp)
        merged = _gate_merge(xn, oa, ob, oc, w_gate_b, w_a, w_b, w_c, lidx)
        h = _matmul(merged, w_out, lidx, out_dtype=F32, tm_cap=1040, tn_cap=512, res=h, name="out_proj")
        xn2 = _rmsnorm_rows(h, ffn_g, lidx, BF16)
        hid = _swiglu(xn2, ffn_w1, ffn_w3, lidx)
        h = _matmul(hid, w2, lidx, out_dtype=F32, tm_cap=640, tn_cap=256, res=h, name="ffn_down")
        return h, tuple(p_bufs), tuple(s_bufs), lat_buf, kr_buf

    carry = (h0, head_bufs(nbp, tp), head_bufs(nbs, ts),
             jnp.zeros((depth, m, MLA_KV_LORA), F32), jnp.zeros((depth, m, MLA_ROPE), F32))
    h, p_bufs, s_bufs, lat_buf, kr_buf = lax.fori_loop(0, depth, layer, carry)

    final_g, l0 = final_norm.reshape(1, 1, d), jnp.zeros((1,), jnp.int32)
    y_prompt = _rmsnorm_rows(h, final_g, l0, F32, row0=0, rows=mp).reshape(nbp, tp, d)
    y_sample = _rmsnorm_rows(h, final_g, l0, F32, row0=mp, rows=ms).reshape(nbs, ts, d)
    p_lat = lat_buf[:, :mp].reshape(depth, nbp, tp, MLA_KV_LORA)
    s_lat = lat_buf[:, mp:].reshape(depth, nbs, ts, MLA_KV_LORA)
    p_kr = kr_buf[:, :mp].reshape(depth, nbp, tp, MLA_ROPE)
    s_kr = kr_buf[:, mp:].reshape(depth, nbs, ts, MLA_ROPE)
    return (y_prompt, y_sample, *p_bufs, p_lat, p_kr, *s_bufs, s_lat, s_kr)
```

```python
import functools
import math

import jax
import jax.numpy as jnp
from jax import lax
from jax.experimental import pallas as pl
from jax.experimental.pallas import tpu as pltpu

D_MODEL = 4096
BATCH = 2
SEQ = 4096
DEPTH = 4
DEC_BATCH = 8
DEC_SEQ = 16
PAST_LEN = 2048

CHUNK = 64
ROPE_THETA = 10000.0
NORM_EPS = 1e-6
NEG_INF = -1e30

DIFF_HEADS = 8
DIFF_DH = 64
SB_HEADS = 8
MLA_HEADS = 16
MLA_ROPE = 64
MLA_Q_LORA = 1024
MLA_KV_LORA = 512
HEAD_W = 128
MLA_QK_PAD = 2 * HEAD_W
N_BRANCHES = 3
COL_QA, COL_KA, COL_VA, COL_QB, COL_KB, COL_VB, COL_CQ, COL_CKV, COL_KR = (
    0, 1024, 2048, 3072, 4096, 5120, 6144, 7168, 7680)
FFN_HIDDEN = -(-8 * D_MODEL // 768) * 256

F32 = jnp.float32
BF16 = jnp.bfloat16

V7X_LANES = 128
V7X_BF16_SUBLANES = 16
V7X_VMEM_LIMIT_CAP = 60 * 1024 * 1024


def _tile(n, cap, mult):
    best = None
    for t in range(mult, min(n, cap) + 1, mult):
        if n % t == 0:
            best = t
    if best is None:
        raise ValueError(f"no tile for {n} (cap {cap}, multiple of {mult})")
    return best


def _params(block_bytes):
    need = 2 * block_bytes + (8 << 20)
    return pltpu.CompilerParams(vmem_limit_bytes=int(min(max(need, 32 << 20), V7X_VMEM_LIMIT_CAP)))


def _nbytes(shape, dtype):
    return math.prod(shape) * jnp.dtype(dtype).itemsize


def _dot(a, b):
    return jnp.dot(a, b, preferred_element_type=F32)


def _dot_nt(a, b):
    return lax.dot_general(a, b, (((1,), (1,)), ((), ())), preferred_element_type=F32)


def _rms(x, g):
    return x * lax.rsqrt(jnp.mean(x * x, axis=-1, keepdims=True) + NORM_EPS) * g


def _rope_lanes(x, cos, sin):
    lane = lax.broadcasted_iota(jnp.int32, x.shape, 1)
    first_half = (lane % 64) < 32
    partner = jnp.where(first_half, pltpu.roll(x, x.shape[1] - 32, 1), pltpu.roll(x, 32, 1))
    return x * cos + partner * sin


def _rmsnorm_kernel(l_ref, x_ref, g_ref, o_ref):
    del l_ref
    o_ref[...] = _rms(x_ref[...], g_ref[...]).astype(o_ref.dtype)


def _rmsnorm_rows(x, g_stack, lidx, out_dtype, row0=0, rows=None):
    d = x.shape[1]
    m = x.shape[0] if rows is None else rows
    tr = _tile(math.gcd(m, row0) if row0 else m, 320, V7X_BF16_SUBLANES)
    rb0 = row0 // tr
    blk = _nbytes((tr, d), F32) + _nbytes((tr, d), out_dtype)
    return pl.pallas_call(
        _rmsnorm_kernel,
        grid_spec=pltpu.PrefetchScalarGridSpec(
            num_scalar_prefetch=1, grid=(m // tr,),
            in_specs=[pl.BlockSpec((tr, d), lambda i, l: (rb0 + i, 0)),
                      pl.BlockSpec((None, 1, d), lambda i, l: (l[0], 0, 0))],
            out_specs=pl.BlockSpec((tr, d), lambda i, l: (i, 0))),
        out_shape=jax.ShapeDtypeStruct((m, d), out_dtype),
        compiler_params=_params(blk), name="rmsnorm_rows",
    )(lidx, x, g_stack)


def _cast_kernel(x_ref, o_ref):
    o_ref[...] = x_ref[...].astype(o_ref.dtype)


def _to_bf16(w):
    depth, kdim, n = w.shape
    rows = depth * kdim
    tr = _tile(rows, max(V7X_BF16_SUBLANES, (8 << 20) // (4 * n)), V7X_BF16_SUBLANES)
    out = pl.pallas_call(
        _cast_kernel, grid=(rows // tr,),
        in_specs=[pl.BlockSpec((tr, n), lambda i: (i, 0))],
        out_specs=pl.BlockSpec((tr, n), lambda i: (i, 0)),
        out_shape=jax.ShapeDtypeStruct((rows, n), BF16),
        compiler_params=_params(_nbytes((tr, n), F32) + _nbytes((tr, n), BF16)), name="to_bf16",
    )(w.reshape(rows, n))
    return out.reshape(depth, kdim, n)


def _mm_kernel(l_ref, x_ref, w_ref, *rest, nk, has_res, rope):
    del l_ref
    rest = list(rest)
    r_ref = rest.pop(0) if has_res else None
    cos_ref, sin_ref = (rest.pop(0), rest.pop(0)) if rope else (None, None)
    o_ref = rest.pop(0)
    part = _dot(x_ref[...].astype(BF16), w_ref[...].astype(BF16))

    def finish(acc):
        if has_res:
            acc = acc + r_ref[...]
        if rope:
            cos, sin = cos_ref[...], sin_ref[...]
            gw = cos.shape[1]
            for g in range(acc.shape[1] // gw):
                seg = acc[:, g * gw:(g + 1) * gw]
                o_ref[:, g * gw:(g + 1) * gw] = _rope_lanes(seg, cos, sin).astype(o_ref.dtype)
        else:
            o_ref[...] = acc.astype(o_ref.dtype)

    if nk == 1:
        finish(part)
    else:
        acc_ref, = rest
        k = pl.program_id(2)

        @pl.when(k == 0)
        def _():
            acc_ref[...] = part

        @pl.when(k > 0)
        def _():
            acc_ref[...] += part

        @pl.when(k == nk - 1)
        def _():
            finish(acc_ref[...])


def _matmul(x, w_stack, lidx, *, out_dtype, tm_cap, tn_cap, tk=None, res=None, rope=None,
            x_stacked=False, n_cols=None, name="matmul"):
    m, kdim = x.shape[-2:]
    n = w_stack.shape[2] if n_cols is None else n_cols
    tm = _tile(m, tm_cap, V7X_BF16_SUBLANES)
    tn = _tile(n, tn_cap, V7X_LANES) if n % V7X_LANES == 0 else n
    tk = kdim if tk is None else tk
    nk = kdim // tk
    if x_stacked:
        x_spec = pl.BlockSpec((None, tm, tk), lambda i, j, k, l: (l[0], i, k))
    else:
        x_spec = pl.BlockSpec((tm, tk), lambda i, j, k, l: (i, k))
    in_specs = [x_spec, pl.BlockSpec((None, tk, tn), lambda i, j, k, l: (l[0], k, j))]
    args = [x, w_stack]
    blk = _nbytes((tm, tk), x.dtype) + _nbytes((tk, tn), w_stack.dtype) + _nbytes((tm, tn), F32)
    if res is not None:
        in_specs.append(pl.BlockSpec((tm, tn), lambda i, j, k, l: (i, j)))
        args.append(res)
        blk += _nbytes((tm, tn), F32)
    if rope is not None:
        gw = rope[0].shape[1]
        assert tn % gw == 0
        in_specs += [pl.BlockSpec((tm, gw), lambda i, j, k, l: (i, 0))] * 2
        args += list(rope)
        blk += 2 * _nbytes((tm, gw), F32)
    return pl.pallas_call(
        functools.partial(_mm_kernel, nk=nk, has_res=res is not None, rope=rope is not None),
        grid_spec=pltpu.PrefetchScalarGridSpec(
            num_scalar_prefetch=1, grid=(m // tm, n // tn, nk),
            in_specs=in_specs,
            out_specs=pl.BlockSpec((tm, tn), lambda i, j, k, l: (i, j)),
            scratch_shapes=[pltpu.VMEM((tm, tn), F32)] if nk > 1 else []),
        out_shape=jax.ShapeDtypeStruct((m, n), out_dtype),
        compiler_params=_params(blk), name=name,
    )(lidx, *args)


def _kr_kernel(l_ref, xn_ref, w_ref, cos_ref, sin_ref, kr_in, kr_o, kr128_o):
    del l_ref, kr_in
    r = _rope_lanes(_dot(xn_ref[...], w_ref[...]), cos_ref[...], sin_ref[...])
    kr_o[...] = r[:, :MLA_ROPE]
    kr128_o[...] = r.astype(BF16)


def _shared_rope_key(xn, w_kr_pad, cos, sin, kr_buf, lidx):
    m, d = xn.shape
    tm = _tile(m, 640, V7X_BF16_SUBLANES)
    blk = _nbytes((tm, d), BF16) + _nbytes((d, HEAD_W), BF16) + 4 * _nbytes((tm, HEAD_W), F32)
    return pl.pallas_call(
        _kr_kernel,
        grid_spec=pltpu.PrefetchScalarGridSpec(
            num_scalar_prefetch=1, grid=(m // tm,),
            in_specs=[pl.BlockSpec((tm, d), lambda i, l: (i, 0)),
                      pl.BlockSpec((None, d, HEAD_W), lambda i, l: (l[0], 0, 0)),
                      pl.BlockSpec((tm, HEAD_W), lambda i, l: (i, 0)),
                      pl.BlockSpec((tm, HEAD_W), lambda i, l: (i, 0)),
                      pl.BlockSpec(memory_space=pl.ANY)],
            out_specs=[pl.BlockSpec((None, tm, MLA_ROPE), lambda i, l: (l[0], i, 0)),
                       pl.BlockSpec((tm, HEAD_W), lambda i, l: (i, 0))]),
        out_shape=[jax.ShapeDtypeStruct(kr_buf.shape, F32), jax.ShapeDtypeStruct((m, HEAD_W), BF16)],
        input_output_aliases={5: 0},
        compiler_params=_params(blk), name="shared_rope_key",
    )(lidx, xn, w_kr_pad, cos, sin, kr_buf)


def _post_heads_kernel(l_ref, qa_ref, ka_ref, va_ref, qb_ref, kb_ref, vb_ref, cos_ref, sin_ref,
                       dk_in, dv_in, sk_in, sv_in, qa_o, qb_o, dk_o, dv_o, sk_o, sv_o):
    del l_ref, dk_in, dv_in, sk_in, sv_in
    cos, sin = cos_ref[...], sin_ref[...]
    qa_o[...] = _rope_lanes(qa_ref[...], cos, sin).astype(BF16)
    dk_o[...] = _rope_lanes(ka_ref[...], cos, sin)
    dv_o[...] = va_ref[...]
    qb_o[...] = qb_ref[...].astype(BF16)
    sk_o[...] = kb_ref[...]
    sv_o[...] = vb_ref[...]


def _post_heads(proj, cos, sin, bufs, lidx, *, row0, nb, t, tt):
    nt = t // tt
    rb0 = row0 // tt
    assert row0 % tt == 0

    def col(cb):
        return pl.BlockSpec((tt, HEAD_W), lambda b, ti, h, l: (rb0 + b * nt + ti, cb // HEAD_W + h))

    tab = pl.BlockSpec((tt, HEAD_W), lambda b, ti, h, l: (rb0 + b * nt + ti, 0))
    qspec = pl.BlockSpec((tt, HEAD_W), lambda b, ti, h, l: (b * nt + ti, h))
    bspec = pl.BlockSpec((None, None, None, tt, HEAD_W), lambda b, ti, h, l: (l[0], b, h, ti, 0))
    anyspec = pl.BlockSpec(memory_space=pl.ANY)
    qshape = jax.ShapeDtypeStruct((nb * t, DIFF_HEADS * HEAD_W), BF16)
    blk = 8 * _nbytes((tt, HEAD_W), F32) + 6 * _nbytes((tt, HEAD_W), F32)
    return pl.pallas_call(
        _post_heads_kernel,
        grid_spec=pltpu.PrefetchScalarGridSpec(
            num_scalar_prefetch=1, grid=(nb, nt, DIFF_HEADS),
            in_specs=[col(COL_QA), col(COL_KA), col(COL_VA), col(COL_QB), col(COL_KB), col(COL_VB),
                      tab, tab, anyspec, anyspec, anyspec, anyspec],
            out_specs=[qspec, qspec, bspec, bspec, bspec, bspec]),
        out_shape=[qshape, qshape] + [jax.ShapeDtypeStruct(b.shape, F32) for b in bufs],
        input_output_aliases={9: 2, 10: 3, 11: 4, 12: 5},
        compiler_params=_params(blk), name="post_heads",
    )(lidx, proj, proj, proj, proj, proj, proj, cos, sin, *bufs)


def _post_mla_kernel(l_ref, cq_ref, ckv_ref, gq_ref, gkv_ref, lat_in, cqn_o, ckvb_o, lat_o):
    del l_ref, lat_in
    cqn_o[...] = _rms(cq_ref[...], gq_ref[...]).astype(BF16)
    c = _rms(ckv_ref[...], gkv_ref[...])
    lat_o[...] = c
    ckvb_o[...] = c.astype(BF16)


def _post_mla(proj, gq, gkv, lat_buf, lidx):
    m = proj.shape[0]
    tt = _tile(m, 640, V7X_BF16_SUBLANES)
    blk = 3 * _nbytes((tt, MLA_Q_LORA + MLA_KV_LORA), F32)
    return pl.pallas_call(
        _post_mla_kernel,
        grid_spec=pltpu.PrefetchScalarGridSpec(
            num_scalar_prefetch=1, grid=(m // tt,),
            in_specs=[pl.BlockSpec((tt, MLA_Q_LORA), lambda i, l: (i, COL_CQ // MLA_Q_LORA)),
                      pl.BlockSpec((tt, MLA_KV_LORA), lambda i, l: (i, COL_CKV // MLA_KV_LORA)),
                      pl.BlockSpec((None, 1, MLA_Q_LORA), lambda i, l: (l[0], 0, 0)),
                      pl.BlockSpec((None, 1, MLA_KV_LORA), lambda i, l: (l[0], 0, 0)),
                      pl.BlockSpec(memory_space=pl.ANY)],
            out_specs=[pl.BlockSpec((tt, MLA_Q_LORA), lambda i, l: (i, 0)),
                       pl.BlockSpec((tt, MLA_KV_LORA), lambda i, l: (i, 0)),
                       pl.BlockSpec((None, tt, MLA_KV_LORA), lambda i, l: (l[0], i, 0))]),
        out_shape=[jax.ShapeDtypeStruct((m, MLA_Q_LORA), BF16),
                   jax.ShapeDtypeStruct((m, MLA_KV_LORA), BF16),
                   jax.ShapeDtypeStruct(lat_buf.shape, F32)],
        input_output_aliases={5: 2},
        compiler_params=_params(blk), name="post_mla",
    )(lidx, proj, proj, gq, gkv, lat_buf)


LOG2E = math.log2(math.e)
SB_DEAD_LOG = -104.0


def _with_ones(v):
    return jnp.concatenate([v, jnp.ones(v.shape, v.dtype)], axis=1)


def _softmax_step(s2, v_ext, m_ref, acc_ref):
    m_prev = m_ref[...]
    m_new = jnp.maximum(m_prev, jnp.max(s2, axis=-1, keepdims=True))
    alpha = jnp.exp2(m_prev - m_new)
    lanes = m_prev.shape[1]
    p = jnp.concatenate([jnp.exp2(s2[:, c * lanes:(c + 1) * lanes] - m_new).astype(BF16)
                         for c in range(s2.shape[1] // lanes)], axis=1)
    acc_ref[...] = jnp.concatenate([alpha, alpha], axis=1) * acc_ref[...] + _dot(p, v_ext)
    m_ref[...] = m_new


def _softmax_init(m_ref, acc_ref):
    m_ref[...] = jnp.full(m_ref.shape, NEG_INF, F32)
    acc_ref[...] = jnp.zeros(acc_ref.shape, F32)


def _softmax_result(acc_ref):
    acc = acc_ref[...]
    return acc[:, :HEAD_W] / acc[:, HEAD_W:]


def _positions(shape, q0, k0):
    qpos = q0 + lax.broadcasted_iota(jnp.int32, shape, 0)
    kpos = k0 + lax.broadcasted_iota(jnp.int32, shape, 1)
    return qpos, kpos


def _chunk_mask(shape, q0, k0):
    qpos, kpos = _positions(shape, q0, k0)
    return (kpos // CHUNK) <= (qpos // CHUNK)


def _diff_lambda(lam_ref, li_ref):
    lv = lam_ref[...]
    lam_init = li_ref[:, 0:1]
    d1 = jnp.sum(lv[0:1, :] * lv[1:2, :], axis=-1, keepdims=True)
    d2 = jnp.sum(lv[2:3, :] * lv[3:4, :], axis=-1, keepdims=True)
    return jnp.exp(d1) - jnp.exp(d2) + lam_init, lam_init


def _split_components(q):
    lane = lax.broadcasted_iota(jnp.int32, q.shape, 1)
    zero = jnp.zeros_like(q)
    return jnp.where(lane < DIFF_DH, q, zero), jnp.where(lane >= DIFF_DH, q, zero)


def _softplus(z):
    return jnp.maximum(z, 0.0) + jnp.log1p(jnp.exp(-jnp.abs(z)))


def _upper_ones(n):
    r = lax.broadcasted_iota(jnp.int32, (n, n), 0)
    c = lax.broadcasted_iota(jnp.int32, (n, n), 1)
    return jnp.where(r > c, 1.0, 0.0).astype(BF16)


def _sb_block(z, mask, carry, v, tri):
    sp = _softplus(z)
    log_stay = -sp if mask is None else jnp.where(mask, -sp, 0.0)
    hi = log_stay.astype(BF16)
    lo = (log_stay - hi.astype(F32)).astype(BF16)
    between = _dot(hi, tri) + _dot(lo, tri) + carry
    a = jnp.exp(z - sp + between)
    if mask is not None:
        a = jnp.where(mask, a, 0.0)
    return _dot(a.astype(BF16), v), carry + jnp.sum(log_stay, axis=-1, keepdims=True)


def _attn_tiles(t):
    tq = _tile(t, 512, CHUNK)
    return tq, tq


def _diff_prompt_kernel(l_ref, q_ref, k_ref, v_ref, lam_ref, g_ref, li_ref, bias_ref, o_ref, m_ref, acc_ref, s_ref,
                        *, tq, tk, hp):
    del l_ref
    assert tq == tk
    qi = pl.program_id(2)
    scale2 = DIFF_DH ** -0.5 * LOG2E
    for c in range(2 * hp):
        _softmax_init(m_ref.at[c], acc_ref.at[c])

    def rows(j):
        return pl.ds(pl.multiple_of(j * tk, tk), tk)

    def store_logits(j, hh):
        kb = k_ref[hh, rows(j), :].astype(BF16)
        q1, q2 = _split_components(q_ref[:, hh * HEAD_W:(hh + 1) * HEAD_W])
        s_ref[2 * hh] = _dot_nt(q1, kb) * scale2
        s_ref[2 * hh + 1] = _dot_nt(q2, kb) * scale2

    def step(j, hh, bias):
        v_ext = _with_ones(v_ref[hh, rows(j), :].astype(BF16))
        for c in (2 * hh, 2 * hh + 1):
            s2 = s_ref[c] if bias is None else s_ref[c] + bias
            _softmax_step(s2, v_ext, m_ref.at[c], acc_ref.at[c])

    for hh in range(hp):
        store_logits(0, hh)

    def body(j, carry):
        for hh in range(hp):
            step(j, hh, None)
            store_logits(j + 1, hh)
        return carry

    lax.fori_loop(0, qi, body, 0)
    lam, lam_init = _diff_lambda(lam_ref, li_ref)
    for hh in range(hp):
        step(qi, hh, bias_ref[...])
        o = _softmax_result(acc_ref.at[2 * hh]) - lam * _softmax_result(acc_ref.at[2 * hh + 1])
        o_ref[:, hh * HEAD_W:(hh + 1) * HEAD_W] = (_rms(o, g_ref[...]) * (1.0 - lam_init)).astype(o_ref.dtype)


def _diag_bias(tq, tk):
    r = jnp.arange(tq, dtype=jnp.int32)[:, None] // CHUNK
    c = jnp.arange(tk, dtype=jnp.int32)[None, :] // CHUNK
    return jnp.where(c <= r, 0.0, NEG_INF).astype(F32)


def _diff_prompt(q, k_buf, v_buf, lam, g, li, lidx, *, nb, t, rows_total):
    tq, tk = _attn_tiles(t)
    nq = t // tq
    hp = 2
    kv_spec = pl.BlockSpec((None, None, hp, t, HEAD_W), lambda b, h, i, l: (l[0], b, h, 0, 0))
    blk = 2 * hp * _nbytes((t, HEAD_W), F32) + (1 + 10 * hp) * _nbytes((tq, tk), F32)
    return pl.pallas_call(
        functools.partial(_diff_prompt_kernel, tq=tq, tk=tk, hp=hp),
        grid_spec=pltpu.PrefetchScalarGridSpec(
            num_scalar_prefetch=1, grid=(nb, DIFF_HEADS // hp, nq),
            in_specs=[pl.BlockSpec((tq, hp * HEAD_W), lambda b, h, i, l: (b * nq + i, h)),
                      kv_spec, kv_spec,
                      pl.BlockSpec((None, 4, DIFF_DH), lambda b, h, i, l: (l[0], 0, 0)),
                      pl.BlockSpec((None, 1, HEAD_W), lambda b, h, i, l: (l[0], 0, 0)),
                      pl.BlockSpec((None, 1, HEAD_W), lambda b, h, i, l: (l[0], 0, 0)),
                      pl.BlockSpec((tq, tk), lambda b, h, i, l: (0, 0))],
            out_specs=pl.BlockSpec((tq, hp * HEAD_W), lambda b, h, i, l: (b * nq + i, h)),
            scratch_shapes=[pltpu.VMEM((2 * hp, tq, HEAD_W), F32), pltpu.VMEM((2 * hp, tq, 2 * HEAD_W), F32),
                            pltpu.VMEM((2 * hp, tq, tk), F32)]),
        out_shape=jax.ShapeDtypeStruct((rows_total, DIFF_HEADS * HEAD_W), BF16),
        compiler_params=_params(blk), name="diff_prompt",
    )(lidx, q, k_buf, v_buf, lam, g, li, _diag_bias(tq, tk))


def _sb_prompt_kernel(l_ref, q_ref, k_ref, v_ref, o_ref, c_ref, acc_ref, *, tq, tk, hp):
    del l_ref
    qi = pl.program_id(2)
    ratio = tq // tk
    scale = HEAD_W ** -0.5
    tri = _upper_ones(tk)
    c_ref[...] = jnp.zeros(c_ref.shape, F32)
    acc_ref[...] = jnp.zeros(acc_ref.shape, F32)

    def block(j, masked):
        start = pl.multiple_of(j * tk, tk)
        mask = None
        if masked:
            qpos, kpos = _positions((tq, tk), qi * tq, start)
            mask = kpos < qpos
        for hh in range(hp):
            kb = k_ref[hh, pl.ds(start, tk), :].astype(BF16)
            vb = v_ref[hh, pl.ds(start, tk), :].astype(BF16)
            z = _dot_nt(q_ref[:, hh * HEAD_W:(hh + 1) * HEAD_W], kb) * scale
            out, carry = _sb_block(z, mask, c_ref[hh], vb, tri)
            acc_ref[hh] += out
            c_ref[hh] = carry

    for u in reversed(range(ratio)):
        block(qi * ratio + u, True)

    def alive():
        return jnp.max(c_ref[...]) > SB_DEAD_LOG

    def cond(state):
        j, live = state
        return jnp.logical_and(j >= 0, live)

    def body(state):
        j, _ = state
        block(j, False)
        return j - 1, alive()

    lax.while_loop(cond, body, (qi * ratio - 1, alive()))
    for hh in range(hp):
        o_ref[:, hh * HEAD_W:(hh + 1) * HEAD_W] = acc_ref[hh].astype(o_ref.dtype)


def _sb_prompt(q, k_buf, v_buf, lidx, *, nb, t, rows_total):
    tq, _ = _attn_tiles(t)
    tk = _tile(tq, 256, CHUNK)
    nq = t // tq
    hp = 2
    kv_spec = pl.BlockSpec((None, None, hp, t, HEAD_W), lambda b, h, i, l: (l[0], b, h, 0, 0))
    blk = 2 * hp * _nbytes((t, HEAD_W), F32) + 16 * hp * _nbytes((tq, tk), F32)
    return pl.pallas_call(
        functools.partial(_sb_prompt_kernel, tq=tq, tk=tk, hp=hp),
        grid_spec=pltpu.PrefetchScalarGridSpec(
            num_scalar_prefetch=1, grid=(nb, SB_HEADS // hp, nq),
            in_specs=[pl.BlockSpec((tq, hp * HEAD_W), lambda b, h, i, l: (b * nq + i, h)), kv_spec, kv_spec],
            out_specs=pl.BlockSpec((tq, hp * HEAD_W), lambda b, h, i, l: (b * nq + i, h)),
            scratch_shapes=[pltpu.VMEM((hp, tq, 1), F32), pltpu.VMEM((hp, tq, HEAD_W), F32)]),
        out_shape=jax.ShapeDtypeStruct((rows_total, SB_HEADS * HEAD_W), BF16),
        compiler_params=_params(blk), name="sb_prompt",
    )(lidx, q, k_buf, v_buf)


def _mla_prompt_kernel(q_ref, kv_ref, kr_ref, bias_ref, o_ref, m_ref, acc_ref, s_ref, *, tq, tk, hp):
    assert tq == tk
    qi = pl.program_id(2)
    scale2 = (HEAD_W + MLA_ROPE) ** -0.5 * LOG2E
    for hh in range(hp):
        _softmax_init(m_ref.at[hh], acc_ref.at[hh])

    def rows(j):
        return pl.ds(pl.multiple_of(j * tk, tk), tk)

    def logits(j, hh):
        kcat = jnp.concatenate([kv_ref[rows(j), 2 * hh * HEAD_W:(2 * hh + 1) * HEAD_W], kr_ref[rows(j), :]], axis=1)
        return _dot_nt(q_ref[:, hh * MLA_QK_PAD:(hh + 1) * MLA_QK_PAD], kcat) * scale2

    def values(j, hh):
        return _with_ones(kv_ref[rows(j), (2 * hh + 1) * HEAD_W:(2 * hh + 2) * HEAD_W])

    for hh in range(hp):
        s_ref[hh] = logits(0, hh)

    def body(j, c):
        for hh in range(hp):
            _softmax_step(s_ref[hh], values(j, hh), m_ref.at[hh], acc_ref.at[hh])
            s_ref[hh] = logits(j + 1, hh)
        return c

    lax.fori_loop(0, qi, body, 0)
    for hh in range(hp):
        _softmax_step(s_ref[hh] + bias_ref[...], values(qi, hh), m_ref.at[hh], acc_ref.at[hh])
        o_ref[:, hh * HEAD_W:(hh + 1) * HEAD_W] = _softmax_result(acc_ref.at[hh]).astype(o_ref.dtype)


def _mla_prompt(qcat, kv, kr128, *, nb, t, rows_total):
    tq, tk = _attn_tiles(t)
    nq = t // tq
    hp = 4
    blk = (2 * hp + 1) * _nbytes((t, HEAD_W), BF16) + (1 + 6 * hp) * _nbytes((tq, tk), F32)
    return pl.pallas_call(
        functools.partial(_mla_prompt_kernel, tq=tq, tk=tk, hp=hp),
        grid=(nb, MLA_HEADS // hp, nq),
        in_specs=[pl.BlockSpec((tq, hp * MLA_QK_PAD), lambda b, h, i: (b * nq + i, h)),
                  pl.BlockSpec((t, 2 * hp * HEAD_W), lambda b, h, i: (b, h)),
                  pl.BlockSpec((t, HEAD_W), lambda b, h, i: (b, 0)),
                  pl.BlockSpec((tq, tk), lambda b, h, i: (0, 0))],
        out_specs=pl.BlockSpec((tq, hp * HEAD_W), lambda b, h, i: (b * nq + i, h)),
        scratch_shapes=[pltpu.VMEM((hp, tq, HEAD_W), F32), pltpu.VMEM((hp, tq, 2 * HEAD_W), F32),
                        pltpu.VMEM((hp, tq, tk), F32)],
        out_shape=jax.ShapeDtypeStruct((rows_total, MLA_HEADS * HEAD_W), BF16),
        compiler_params=_params(blk), name="mla_prompt",
    )(qcat, kv, kr128, _diag_bias(tq, tk))


def _pad_rows(x, rows):
    return jnp.concatenate([x, jnp.zeros((rows - x.shape[0], x.shape[1]), x.dtype)], axis=0)


def _two_part_softmax(sp, sn, vp, vn):
    m = jnp.maximum(jnp.max(sp, axis=-1, keepdims=True), jnp.max(sn, axis=-1, keepdims=True))
    pp, pn = jnp.exp(sp - m), jnp.exp(sn - m)
    denom = jnp.sum(pp, axis=-1, keepdims=True) + jnp.sum(pn, axis=-1, keepdims=True)
    return (_dot(pp.astype(BF16), vp) + _dot(pn.astype(BF16), vn)) / denom


def _diff_sample_kernel(l_ref, q_ref, kp_ref, vp_ref, kn_ref, vn_ref, lam_ref, g_ref, li_ref, prev_ref, o_ref,
                        *, past, t):
    del l_ref, prev_ref
    q1, q2 = _split_components(q_ref[...])
    scale = DIFF_DH ** -0.5
    kp, vp = kp_ref[...].astype(BF16), vp_ref[...].astype(BF16)
    kn = _pad_rows(kn_ref[...], HEAD_W).astype(BF16)
    vn = _pad_rows(vn_ref[...], HEAD_W).astype(BF16)
    mask_p = _chunk_mask((t, past), past, 0)
    qpos, kpos = _positions((t, HEAD_W), past, past)
    mask_n = ((kpos // CHUNK) <= (qpos // CHUNK)) & (kpos < past + t)

    def attend(qx):
        sp = jnp.where(mask_p, _dot_nt(qx, kp) * scale, NEG_INF)
        sn = jnp.where(mask_n, _dot_nt(qx, kn) * scale, NEG_INF)
        return _two_part_softmax(sp, sn, vp, vn)

    lam, lam_init = _diff_lambda(lam_ref, li_ref)
    o = attend(q1) - lam * attend(q2)
    o_ref[...] = (_rms(o, g_ref[...]) * (1.0 - lam_init)).astype(o_ref.dtype)


def _diff_sample(q, k_cache, v_cache, k_buf, v_buf, lam, g, li, merged, lidx, *, nb, t, past, row0):
    rb0 = row0 // t
    cache_spec = pl.BlockSpec((None, None, None, past, HEAD_W), lambda b, h, l: (l[0], b, h, 0, 0))
    new_spec = pl.BlockSpec((None, None, None, t, HEAD_W), lambda b, h, l: (l[0], b, h, 0, 0))
    blk = 2 * _nbytes((past, HEAD_W), F32) + 8 * _nbytes((t, past), F32)
    return pl.pallas_call(
        functools.partial(_diff_sample_kernel, past=past, t=t),
        grid_spec=pltpu.PrefetchScalarGridSpec(
            num_scalar_prefetch=1, grid=(nb, DIFF_HEADS),
            in_specs=[pl.BlockSpec((t, HEAD_W), lambda b, h, l: (b, h)),
                      cache_spec, cache_spec, new_spec, new_spec,
                      pl.BlockSpec((None, 4, DIFF_DH), lambda b, h, l: (l[0], 0, 0)),
                      pl.BlockSpec((None, 1, HEAD_W), lambda b, h, l: (l[0], 0, 0)),
                      pl.BlockSpec((None, 1, HEAD_W), lambda b, h, l: (l[0], 0, 0)),
                      pl.BlockSpec(memory_space=pl.ANY)],
            out_specs=pl.BlockSpec((t, HEAD_W), lambda b, h, l: (rb0 + b, h))),
        out_shape=jax.ShapeDtypeStruct(merged.shape, BF16),
        input_output_aliases={9: 0},
        compiler_params=_params(blk), name="diff_sample",
    )(lidx, q, k_cache, v_cache, k_buf, v_buf, lam, g, li, merged)


def _sb_sample_kernel(l_ref, q_ref, kp_ref, vp_ref, kn_ref, vn_ref, prev_ref, o_ref, *, past, t, cw):
    del l_ref, prev_ref
    q = q_ref[...]
    scale = HEAD_W ** -0.5
    kn = _pad_rows(kn_ref[...], HEAD_W).astype(BF16)
    vn = _pad_rows(vn_ref[...], HEAD_W).astype(BF16)
    qpos, kpos = _positions((t, HEAD_W), past, past)
    mask_n = (kpos < qpos) & (kpos < past + t)
    acc, carry = _sb_block(_dot_nt(q, kn) * scale, mask_n, jnp.zeros((t, 1), F32), vn, _upper_ones(HEAD_W))
    tri = _upper_ones(cw)
    for c in reversed(range(past // cw)):
        kb = kp_ref[c * cw:(c + 1) * cw, :].astype(BF16)
        vb = vp_ref[c * cw:(c + 1) * cw, :].astype(BF16)
        qpos, kpos = _positions((t, cw), past, c * cw)
        out, carry = _sb_block(_dot_nt(q, kb) * scale, kpos < qpos, carry, vb, tri)
        acc = acc + out
    o_ref[...] = acc.astype(o_ref.dtype)


def _sb_sample(q, k_cache, v_cache, k_buf, v_buf, merged, lidx, *, nb, t, past, row0):
    rb0 = row0 // t
    cw = _tile(past, 256, V7X_LANES)
    cache_spec = pl.BlockSpec((None, None, None, past, HEAD_W), lambda b, h, l: (l[0], b, h, 0, 0))
    new_spec = pl.BlockSpec((None, None, None, t, HEAD_W), lambda b, h, l: (l[0], b, h, 0, 0))
    blk = 2 * _nbytes((past, HEAD_W), F32) + 8 * _nbytes((t, past), F32)
    return pl.pallas_call(
        functools.partial(_sb_sample_kernel, past=past, t=t, cw=cw),
        grid_spec=pltpu.PrefetchScalarGridSpec(
            num_scalar_prefetch=1, grid=(nb, SB_HEADS),
            in_specs=[pl.BlockSpec((t, HEAD_W), lambda b, h, l: (b, h)),
                      cache_spec, cache_spec, new_spec, new_spec, pl.BlockSpec(memory_space=pl.ANY)],
            out_specs=pl.BlockSpec((t, HEAD_W), lambda b, h, l: (rb0 + b, h))),
        out_shape=jax.ShapeDtypeStruct(merged.shape, BF16),
        input_output_aliases={6: 0},
        compiler_params=_params(blk), name="sb_sample",
    )(lidx, q, k_cache, v_cache, k_buf, v_buf, merged)


def _mla_sample_kernel(l_ref, q_ref, knp_ref, krp_ref, vp_ref, knn_ref, krn_ref, vn_ref, prev_ref, o_ref,
                       *, past, t):
    del l_ref, prev_ref
    q = q_ref[...]
    scale = (HEAD_W + MLA_ROPE) ** -0.5
    kp = jnp.concatenate([knp_ref[...], krp_ref[...]], axis=1)
    kn = _pad_rows(jnp.concatenate([knn_ref[...], krn_ref[...]], axis=1), HEAD_W)
    vn = _pad_rows(vn_ref[...], HEAD_W)
    qpos, kpos = _positions((t, HEAD_W), past, past)
    mask_n = ((kpos // CHUNK) <= (qpos // CHUNK)) & (kpos < past + t)
    sp = jnp.where(_chunk_mask((t, past), past, 0), _dot_nt(q, kp) * scale, NEG_INF)
    sn = jnp.where(mask_n, _dot_nt(q, kn) * scale, NEG_INF)
    o_ref[...] = _two_part_softmax(sp, sn, vp_ref[...], vn).astype(o_ref.dtype)


def _mla_sample(qcat, kv_past, kr_cache128, kv_new, kr_new128, merged, lidx, *, nb, t, past, row0):
    rb0 = row0 // t
    blk = 3 * _nbytes((past, HEAD_W), BF16) + 8 * _nbytes((t, past), F32)
    return pl.pallas_call(
        functools.partial(_mla_sample_kernel, past=past, t=t),
        grid_spec=pltpu.PrefetchScalarGridSpec(
            num_scalar_prefetch=1, grid=(nb, MLA_HEADS),
            in_specs=[pl.BlockSpec((t, MLA_QK_PAD), lambda b, h, l: (rb0 + b, h)),
                      pl.BlockSpec((past, HEAD_W), lambda b, h, l: (b, 2 * h)),
                      pl.BlockSpec((None, None, past, HEAD_W), lambda b, h, l: (l[0], b, 0, 0)),
                      pl.BlockSpec((past, HEAD_W), lambda b, h, l: (b, 2 * h + 1)),
                      pl.BlockSpec((t, HEAD_W), lambda b, h, l: (rb0 + b, 2 * h)),
                      pl.BlockSpec((t, HEAD_W), lambda b, h, l: (rb0 + b, 0)),
                      pl.BlockSpec((t, HEAD_W), lambda b, h, l: (rb0 + b, 2 * h + 1)),
                      pl.BlockSpec(memory_space=pl.ANY)],
            out_specs=pl.BlockSpec((t, HEAD_W), lambda b, h, l: (rb0 + b, h))),
        out_shape=jax.ShapeDtypeStruct(merged.shape, BF16),
        input_output_aliases={8: 0},
        compiler_params=_params(blk), name="mla_sample",
    )(lidx, qcat, kv_past, kr_cache128, kv_past, kv_new, kr_new128, kv_new, merged)


def _gate_merge_kernel(l_ref, xn_ref, oa_ref, ob_ref, oc_ref, wga_ref, wgb_ref, wgc_ref,
                       wa_ref, wb_ref, wc_ref, o_ref):
    del l_ref
    xn = xn_ref[...]

    def branch(wg_ref, mix_ref, w_ref):
        return jax.nn.sigmoid(_dot(xn, wg_ref[...])) * _dot(mix_ref[...], w_ref[...])

    merged = branch(wga_ref, oa_ref, wa_ref) + branch(wgb_ref, ob_ref, wb_ref) + branch(wgc_ref, oc_ref, wc_ref)
    o_ref[...] = merged.astype(o_ref.dtype)


def _gate_merge(xn, oa, ob, oc, w_gate, w_a, w_b, w_c, lidx):
    m, d = xn.shape
    tm = _tile(m, 640, V7X_BF16_SUBLANES)
    tn = _tile(d, 256, V7X_LANES)
    nj = d // tn

    def rows(width):
        return pl.BlockSpec((tm, width), lambda i, j, l: (i, 0))

    def gate(branch):
        return pl.BlockSpec((None, d, tn), lambda i, j, l: (l[0], 0, branch * nj + j))

    def proj(kdim):
        return pl.BlockSpec((None, kdim, tn), lambda i, j, l: (l[0], 0, j))

    wa, wb, wc = oa.shape[1], ob.shape[1], oc.shape[1]
    blk = (_nbytes((tm, d + wa + wb + wc), BF16) + _nbytes((3 * d + wa + wb + wc, tn), BF16)
           + 4 * _nbytes((tm, tn), F32))
    return pl.pallas_call(
        _gate_merge_kernel,
        grid_spec=pltpu.PrefetchScalarGridSpec(
            num_scalar_prefetch=1, grid=(m // tm, nj),
            in_specs=[rows(d), rows(wa), rows(wb), rows(wc), gate(0), gate(1), gate(2),
                      proj(wa), proj(wb), proj(wc)],
            out_specs=pl.BlockSpec((tm, tn), lambda i, j, l: (i, j))),
        out_shape=jax.ShapeDtypeStruct((m, d), BF16),
        compiler_params=_params(blk), name="gate_merge",
    )(lidx, xn, oa, ob, oc, w_gate, w_gate, w_gate, w_a, w_b, w_c)


def _swiglu_kernel(l_ref, x_ref, w1_ref, w3_ref, o_ref):
    del l_ref
    x = x_ref[...]
    o_ref[...] = (jax.nn.silu(_dot(x, w1_ref[...].astype(BF16)))
                  * _dot(x, w3_ref[...].astype(BF16))).astype(o_ref.dtype)


def _swiglu(xn, w1, w3, lidx):
    m, d = xn.shape
    f = w1.shape[2]
    tm = _tile(m, 1040, V7X_BF16_SUBLANES)
    tn = _tile(f, 256, V7X_LANES)
    wspec = pl.BlockSpec((None, d, tn), lambda i, j, l: (l[0], 0, j))
    blk = _nbytes((tm, d), BF16) + 3 * _nbytes((d, tn), w1.dtype) + 4 * _nbytes((tm, tn), F32)
    return pl.pallas_call(
        _swiglu_kernel,
        grid_spec=pltpu.PrefetchScalarGridSpec(
            num_scalar_prefetch=1, grid=(m // tm, f // tn),
            in_specs=[pl.BlockSpec((tm, d), lambda i, j, l: (i, 0)), wspec, wspec],
            out_specs=pl.BlockSpec((tm, tn), lambda i, j, l: (i, j))),
        out_shape=jax.ShapeDtypeStruct((m, f), BF16),
        compiler_params=_params(blk), name="swiglu",
    )(lidx, xn, w1, w3)


def _rope_tables(pos):
    inv_freq = ROPE_THETA ** (-jnp.arange(0, MLA_ROPE, 2, dtype=F32) / MLA_ROPE)
    ang = pos.astype(F32)[:, None] * inv_freq[None, :]
    c, s = jnp.cos(ang), jnp.sin(ang)
    one, zero = jnp.ones_like(c), jnp.zeros_like(c)
    cos128 = jnp.concatenate([c, c, c, c], axis=-1)
    sin128 = jnp.concatenate([-s, s, -s, s], axis=-1)
    cos256 = jnp.concatenate([one, one, one, one, c, c, one, one], axis=-1)
    sin256 = jnp.concatenate([zero, zero, zero, zero, -s, s, zero, zero], axis=-1)
    return cos128, sin128, cos256, sin256


def kernel(x_prompt, x_sample, cache_diff_k, cache_diff_v, cache_sb_k, cache_sb_v, cache_mla_latent,
           cache_mla_krope, attn_norm, w_in, diff_lambda, diff_subln, mla_q_norm, mla_w_uq, mla_kv_norm,
           mla_w_ukv, w_gate, w_branch_a, w_branch_b, w_branch_c, w_out, ffn_norm, ffn_w1, ffn_w3, ffn_w2,
           final_norm):
    nbp, tp, d = x_prompt.shape
    nbs, ts = x_sample.shape[:2]
    past = cache_diff_k.shape[3]
    depth = w_in.shape[0]
    mp, ms = nbp * tp, nbs * ts
    m = mp + ms

    h0 = jnp.concatenate([x_prompt.reshape(mp, d), x_sample.reshape(ms, d)], axis=0)
    pos = jnp.concatenate([jnp.tile(jnp.arange(tp, dtype=jnp.int32), nbp),
                           jnp.tile(past + jnp.arange(ts, dtype=jnp.int32), nbs)])
    cos128, sin128, cos256, sin256 = _rope_tables(pos)

    w_kr_pad = jnp.pad(w_in[:, :, COL_KR:], ((0, 0), (0, 0), (0, HEAD_W - MLA_ROPE))).astype(BF16)
    uq = mla_w_uq.reshape(depth, MLA_Q_LORA, MLA_HEADS, HEAD_W + MLA_ROPE)
    w_uq_pad = jnp.pad(uq, ((0, 0), (0, 0), (0, 0), (0, MLA_QK_PAD - HEAD_W - MLA_ROPE))).reshape(
        depth, MLA_Q_LORA, MLA_HEADS * MLA_QK_PAD)
    w_gate_b, w_a, w_b, w_c = (_to_bf16(w) for w in (w_gate, w_branch_a, w_branch_b, w_branch_c))
    w2 = _to_bf16(ffn_w2)
    kr_cache128 = jnp.pad(cache_mla_krope, ((0, 0), (0, 0), (0, 0), (0, HEAD_W - MLA_ROPE))).astype(BF16)
    lat_cache = cache_mla_latent.reshape(depth, nbs * past, MLA_KV_LORA)

    attn_g = attn_norm.reshape(depth, 1, d)
    ffn_g = ffn_norm.reshape(depth, 1, d)
    gq = mla_q_norm.reshape(depth, 1, MLA_Q_LORA)
    gkv = mla_kv_norm.reshape(depth, 1, MLA_KV_LORA)
    subln = diff_subln.reshape(depth, 1, HEAD_W)
    lam_init = jnp.asarray([0.8 - 0.6 * math.exp(-0.3 * l) for l in range(depth)], F32)
    lam_init = jnp.broadcast_to(lam_init[:, None, None], (depth, 1, HEAD_W))

    def head_bufs(nb, t):
        return tuple(jnp.zeros((depth, nb, DIFF_HEADS, t, HEAD_W), F32) for _ in range(4))

    tt_p = _tile(tp, 512, V7X_BF16_SUBLANES)

    def layer(l, carry):
        h, p_bufs, s_bufs, lat_buf, kr_buf = carry
        lidx = jnp.reshape(l, (1,)).astype(jnp.int32)
        xn = _rmsnorm_rows(h, attn_g, lidx, BF16)
        proj = _matmul(xn, w_in, lidx, out_dtype=F32, tm_cap=1040, tn_cap=512, n_cols=COL_KR, name="in_proj")
        kr_buf, kr128 = _shared_rope_key(xn, w_kr_pad, cos128, sin128, kr_buf, lidx)
        qa_p, qb_p, *p_bufs = _post_heads(proj, cos128, sin128, p_bufs, lidx, row0=0, nb=nbp, t=tp, tt=tt_p)
        qa_s, qb_s, *s_bufs = _post_heads(proj, cos128, sin128, s_bufs, lidx, row0=mp, nb=nbs, t=ts, tt=ts)
        cqn, ckv_b, lat_buf = _post_mla(proj, gq, gkv, lat_buf, lidx)
        qcat = _matmul(cqn, w_uq_pad, lidx, out_dtype=BF16, tm_cap=1040, tn_cap=1024,
                       rope=(cos256, sin256), name="mla_q_up")
        kv = _matmul(ckv_b, mla_w_ukv, lidx, out_dtype=BF16, tm_cap=1040, tn_cap=1024, name="mla_kv_up")
        kv_past = _matmul(lat_cache, mla_w_ukv, lidx, out_dtype=BF16, tm_cap=1024, tn_cap=1024,
                          x_stacked=True, name="mla_kv_up_cache")
        oa = _diff_prompt(qa_p, p_bufs[0], p_bufs[1], diff_lambda, subln, lam_init, lidx, nb=nbp, t=tp, rows_total=m)
        oa = _diff_sample(qa_s, cache_diff_k, cache_diff_v, s_bufs[0], s_bufs[1], diff_lambda, subln,
                          lam_init, oa, lidx, nb=nbs, t=ts, past=past, row0=mp)
        ob = _sb_prompt(qb_p, p_bufs[2], p_bufs[3], lidx, nb=nbp, t=tp, rows_total=m)
        ob = _sb_sample(qb_s, cache_sb_k, cache_sb_v, s_bufs[2], s_bufs[3], ob, lidx, nb=nbs, t=ts, past=past,
                        row0=mp)
        oc = _mla_prompt(qcat, kv, kr128, nb=nbp, t=tp, rows_total=m)
        oc = _mla_sample(qcat, kv_past, kr_cache128, kv, kr128, oc, lidx, nb=nbs, t=ts, past=past, row0=mp)
        merged = _gate_merge(xn, oa, ob, oc, w_gate_b, w_a, w_b, w_c, lidx)
        h = _matmul(merged, w_out, lidx, out_dtype=F32, tm_cap=1040, tn_cap=512, res=h, name="out_proj")
        xn2 = _rmsnorm_rows(h, ffn_g, lidx, BF16)
        hid = _swiglu(xn2, ffn_w1, ffn_w3, lidx)
        h = _matmul(hid, w2, lidx, out_dtype=F32, tm_cap=640, tn_cap=256, res=h, name="ffn_down")
        return h, tuple(p_bufs), tuple(s_bufs), lat_buf, kr_buf

    carry = (h0, head_bufs(nbp, tp), head_bufs(nbs, ts),
             jnp.zeros((depth, m, MLA_KV_LORA), F32), jnp.zeros((depth, m, MLA_ROPE), F32))
    h, p_bufs, s_bufs, lat_buf, kr_buf = lax.fori_loop(0, depth, layer, carry)

    final_g, l0 = final_norm.reshape(1, 1, d), jnp.zeros((1,), jnp.int32)
    y_prompt = _rmsnorm_rows(h, final_g, l0, F32, row0=0, rows=mp).reshape(nbp, tp, d)
    y_sample = _rmsnorm_rows(h, final_g, l0, F32, row0=mp, rows=ms).reshape(nbs, ts, d)
    p_lat = lat_buf[:, :mp].reshape(depth, nbp, tp, MLA_KV_LORA)
    s_lat = lat_buf[:, mp:].reshape(depth, nbs, ts, MLA_KV_LORA)
    p_kr = kr_buf[:, :mp].reshape(depth, nbp, tp, MLA_ROPE)
    s_kr = kr_buf[:, mp:].reshape(depth, nbs, ts, MLA_ROPE)
    return (y_prompt, y_sample, *p_bufs, p_lat, p_kr, *s_bufs, s_lat, s_kr)
```

```python
import functools
import math

import jax
import jax.numpy as jnp
from jax import lax
from jax.experimental import pallas as pl
from jax.experimental.pallas import tpu as pltpu

D_MODEL = 4096
BATCH = 2
SEQ = 4096
DEPTH = 4
DEC_BATCH = 8
DEC_SEQ = 16
PAST_LEN = 2048

CHUNK = 64
ROPE_THETA = 10000.0
NORM_EPS = 1e-6
NEG_INF = -1e30

DIFF_HEADS = 8
DIFF_DH = 64
SB_HEADS = 8
MLA_HEADS = 16
MLA_ROPE = 64
MLA_Q_LORA = 1024
MLA_KV_LORA = 512
HEAD_W = 128
MLA_QK_PAD = 2 * HEAD_W
N_BRANCHES = 3
COL_CQ, COL_KR = 6144, 7680
FFN_HIDDEN = -(-8 * D_MODEL // 768) * 256

F32 = jnp.float32
BF16 = jnp.bfloat16

V7X_LANES = 128
V7X_BF16_SUBLANES = 16
V7X_VMEM_LIMIT_CAP = 60 * 1024 * 1024


def _tile(n, cap, mult):
    best = None
    for t in range(mult, min(n, cap) + 1, mult):
        if n % t == 0:
            best = t
    if best is None:
        raise ValueError(f"no tile for {n} (cap {cap}, multiple of {mult})")
    return best


def _params(block_bytes):
    need = 2 * block_bytes + (8 << 20)
    return pltpu.CompilerParams(vmem_limit_bytes=int(min(max(need, 32 << 20), V7X_VMEM_LIMIT_CAP)))


def _nbytes(shape, dtype):
    return math.prod(shape) * jnp.dtype(dtype).itemsize


def _dot(a, b):
    return jnp.dot(a, b, preferred_element_type=F32)


def _dot_nt(a, b):
    return lax.dot_general(a, b, (((1,), (1,)), ((), ())), preferred_element_type=F32)


def _rms(x, g):
    return x * lax.rsqrt(jnp.mean(x * x, axis=-1, keepdims=True) + NORM_EPS) * g


def _rope_lanes(x, cos, sin):
    lane = lax.broadcasted_iota(jnp.int32, x.shape, 1)
    first_half = (lane % 64) < 32
    partner = jnp.where(first_half, pltpu.roll(x, x.shape[1] - 32, 1), pltpu.roll(x, 32, 1))
    return x * cos + partner * sin


def _rmsnorm_kernel(l_ref, x_ref, g_ref, o_ref):
    del l_ref
    o_ref[...] = _rms(x_ref[...], g_ref[...]).astype(o_ref.dtype)


def _rmsnorm_rows(x, g_stack, lidx, out_dtype, row0=0, rows=None):
    d = x.shape[1]
    m = x.shape[0] if rows is None else rows
    tr = _tile(math.gcd(m, row0) if row0 else m, 320, V7X_BF16_SUBLANES)
    rb0 = row0 // tr
    blk = _nbytes((tr, d), F32) + _nbytes((tr, d), out_dtype)
    return pl.pallas_call(
        _rmsnorm_kernel,
        grid_spec=pltpu.PrefetchScalarGridSpec(
            num_scalar_prefetch=1, grid=(m // tr,),
            in_specs=[pl.BlockSpec((tr, d), lambda i, l: (rb0 + i, 0)),
                      pl.BlockSpec((None, 1, d), lambda i, l: (l[0], 0, 0))],
            out_specs=pl.BlockSpec((tr, d), lambda i, l: (i, 0))),
        out_shape=jax.ShapeDtypeStruct((m, d), out_dtype),
        compiler_params=_params(blk), name="rmsnorm_rows",
    )(lidx, x, g_stack)


def _cast_kernel(x_ref, o_ref):
    o_ref[...] = x_ref[...].astype(o_ref.dtype)


def _to_bf16(w):
    depth, kdim, n = w.shape
    rows = depth * kdim
    tr = _tile(rows, max(V7X_BF16_SUBLANES, (8 << 20) // (4 * n)), V7X_BF16_SUBLANES)
    out = pl.pallas_call(
        _cast_kernel, grid=(rows // tr,),
        in_specs=[pl.BlockSpec((tr, n), lambda i: (i, 0))],
        out_specs=pl.BlockSpec((tr, n), lambda i: (i, 0)),
        out_shape=jax.ShapeDtypeStruct((rows, n), BF16),
        compiler_params=_params(_nbytes((tr, n), F32) + _nbytes((tr, n), BF16)), name="to_bf16",
    )(w.reshape(rows, n))
    return out.reshape(depth, kdim, n)


def _mm_kernel(l_ref, x_ref, w_ref, *rest, nk, has_res, rope):
    del l_ref
    rest = list(rest)
    r_ref = rest.pop(0) if has_res else None
    cos_ref, sin_ref = (rest.pop(0), rest.pop(0)) if rope else (None, None)
    o_ref = rest.pop(0)
    part = _dot(x_ref[...].astype(BF16), w_ref[...].astype(BF16))

    def finish(acc):
        if has_res:
            acc = acc + r_ref[...]
        if rope:
            cos, sin = cos_ref[...], sin_ref[...]
            gw = cos.shape[1]
            for g in range(acc.shape[1] // gw):
                seg = acc[:, g * gw:(g + 1) * gw]
                o_ref[:, g * gw:(g + 1) * gw] = _rope_lanes(seg, cos, sin).astype(o_ref.dtype)
        else:
            o_ref[...] = acc.astype(o_ref.dtype)

    if nk == 1:
        finish(part)
    else:
        acc_ref, = rest
        k = pl.program_id(2)

        @pl.when(k == 0)
        def _():
            acc_ref[...] = part

        @pl.when(k > 0)
        def _():
            acc_ref[...] += part

        @pl.when(k == nk - 1)
        def _():
            finish(acc_ref[...])


def _matmul(x, w_stack, lidx, *, out_dtype, tm_cap, tn_cap, tk=None, res=None, rope=None,
            x_stacked=False, name="matmul"):
    m, kdim = x.shape[-2:]
    n = w_stack.shape[2]
    tm = _tile(m, tm_cap, V7X_BF16_SUBLANES)
    tn = _tile(n, tn_cap, V7X_LANES) if n % V7X_LANES == 0 else n
    tk = kdim if tk is None else tk
    nk = kdim // tk
    if x_stacked:
        x_spec = pl.BlockSpec((None, tm, tk), lambda i, j, k, l: (l[0], i, k))
    else:
        x_spec = pl.BlockSpec((tm, tk), lambda i, j, k, l: (i, k))
    in_specs = [x_spec, pl.BlockSpec((None, tk, tn), lambda i, j, k, l: (l[0], k, j))]
    args = [x, w_stack]
    blk = _nbytes((tm, tk), x.dtype) + _nbytes((tk, tn), w_stack.dtype) + _nbytes((tm, tn), F32)
    if res is not None:
        in_specs.append(pl.BlockSpec((tm, tn), lambda i, j, k, l: (i, j)))
        args.append(res)
        blk += _nbytes((tm, tn), F32)
    if rope is not None:
        gw = rope[0].shape[1]
        assert tn % gw == 0
        in_specs += [pl.BlockSpec((tm, gw), lambda i, j, k, l: (i, 0))] * 2
        args += list(rope)
        blk += 2 * _nbytes((tm, gw), F32)
    return pl.pallas_call(
        functools.partial(_mm_kernel, nk=nk, has_res=res is not None, rope=rope is not None),
        grid_spec=pltpu.PrefetchScalarGridSpec(
            num_scalar_prefetch=1, grid=(m // tm, n // tn, nk),
            in_specs=in_specs,
            out_specs=pl.BlockSpec((tm, tn), lambda i, j, k, l: (i, j)),
            scratch_shapes=[pltpu.VMEM((tm, tn), F32)] if nk > 1 else []),
        out_shape=jax.ShapeDtypeStruct((m, n), out_dtype),
        compiler_params=_params(blk), name=name,
    )(lidx, *args)


def _kr_kernel(l_ref, xn_ref, w_ref, cos_ref, sin_ref, kr_in, kr_o, kr128_o):
    del l_ref, kr_in
    r = _rope_lanes(_dot(xn_ref[...], w_ref[...]), cos_ref[...], sin_ref[...])
    kr_o[...] = r[:, :MLA_ROPE]
    kr128_o[...] = r.astype(BF16)


def _shared_rope_key(xn, w_kr_pad, cos, sin, kr_buf, lidx):
    m, d = xn.shape
    tm = _tile(m, 640, V7X_BF16_SUBLANES)
    blk = _nbytes((tm, d), BF16) + _nbytes((d, HEAD_W), BF16) + 4 * _nbytes((tm, HEAD_W), F32)
    return pl.pallas_call(
        _kr_kernel,
        grid_spec=pltpu.PrefetchScalarGridSpec(
            num_scalar_prefetch=1, grid=(m // tm,),
            in_specs=[pl.BlockSpec((tm, d), lambda i, l: (i, 0)),
                      pl.BlockSpec((None, d, HEAD_W), lambda i, l: (l[0], 0, 0)),
                      pl.BlockSpec((tm, HEAD_W), lambda i, l: (i, 0)),
                      pl.BlockSpec((tm, HEAD_W), lambda i, l: (i, 0)),
                      pl.BlockSpec(memory_space=pl.ANY)],
            out_specs=[pl.BlockSpec((None, tm, MLA_ROPE), lambda i, l: (l[0], i, 0)),
                       pl.BlockSpec((tm, HEAD_W), lambda i, l: (i, 0))]),
        out_shape=[jax.ShapeDtypeStruct(kr_buf.shape, F32), jax.ShapeDtypeStruct((m, HEAD_W), BF16)],
        input_output_aliases={5: 0},
        compiler_params=_params(blk), name="shared_rope_key",
    )(lidx, xn, w_kr_pad, cos, sin, kr_buf)


IN_SECTION_W = DIFF_HEADS * HEAD_W
IN_REST_W = COL_KR - COL_CQ


def _in_proj_kernel(l_ref, x_ref, w_ref, cos_ref, sin_ref, *refs, bpt, tt, spt, has_prev):
    del l_ref
    refs = refs[4 + int(has_prev):]
    qa_o, qb_o, dk_o, dv_o, sk_o, sv_o, rest_o = refs
    j = pl.program_id(1)
    acc = _dot(x_ref[...], w_ref[...].astype(BF16))
    heads = [acc[:, h * HEAD_W:(h + 1) * HEAD_W] for h in range(acc.shape[1] // HEAD_W)]

    def roped():
        cos, sin = cos_ref[...], sin_ref[...]
        return [_rope_lanes(seg, cos, sin) for seg in heads]

    def store_rows(o_ref, segs):
        for h, seg in enumerate(segs):
            o_ref[:, h * HEAD_W:(h + 1) * HEAD_W] = seg.astype(o_ref.dtype)

    def store_heads(o_ref, segs):
        for b in range(bpt):
            for h, seg in enumerate(segs):
                o_ref[b, h] = seg[b * tt:(b + 1) * tt, :]

    def section(k):
        return jnp.logical_and(j >= k * spt, j < (k + 1) * spt)

    pl.when(section(0))(lambda: store_rows(qa_o, roped()))
    pl.when(section(1))(lambda: store_heads(dk_o, roped()))
    pl.when(section(2))(lambda: store_heads(dv_o, heads))
    pl.when(section(3))(lambda: store_rows(qb_o, heads))
    pl.when(section(4))(lambda: store_heads(sk_o, heads))
    pl.when(section(5))(lambda: store_heads(sv_o, heads))

    @pl.when(j >= 6 * spt)
    def _():
        rest_o[...] = acc


def _in_proj(xn, w_in, cos, sin, bufs, lidx, *, row0, nb, t, rows_total, rest_prev=None):
    d = xn.shape[1]
    rows = nb * t
    tm = _tile(math.gcd(rows, row0) if row0 else rows, 1024, V7X_BF16_SUBLANES)
    tn = 2 * HEAD_W
    bpt, tt = (1, tm) if tm <= t else (tm // t, t)
    assert (t % tm == 0) if bpt == 1 else (tm % t == 0)
    nt = t // tt
    rb0 = row0 // tm
    spt = IN_SECTION_W // tn
    rest_tiles = IN_REST_W // tn
    hpt = tn // HEAD_W

    def clamp(j, k, n):
        return jnp.clip(j - k * spt, 0, n - 1)

    def rows_spec(k):
        return pl.BlockSpec((tm, tn), lambda i, j, l: (i, clamp(j, k, spt)))

    def heads_spec(k):
        return pl.BlockSpec((None, bpt, hpt, tt, HEAD_W),
                            lambda i, j, l: (l[0], i // nt, clamp(j, k, spt), i % nt, 0))

    tab = pl.BlockSpec((tm, HEAD_W), lambda i, j, l: (rb0 + i, 0))
    anyspec = pl.BlockSpec(memory_space=pl.ANY)
    has_prev = rest_prev is not None
    qshape = jax.ShapeDtypeStruct((rows, IN_SECTION_W), BF16)
    blk = (_nbytes((tm, d), BF16) + _nbytes((d, tn), w_in.dtype) + 2 * _nbytes((tm, HEAD_W), F32)
           + 8 * _nbytes((tm, tn), F32))
    return pl.pallas_call(
        functools.partial(_in_proj_kernel, bpt=bpt, tt=tt, spt=spt, has_prev=has_prev),
        grid_spec=pltpu.PrefetchScalarGridSpec(
            num_scalar_prefetch=1, grid=(rows // tm, 6 * spt + rest_tiles),
            in_specs=[pl.BlockSpec((tm, d), lambda i, j, l: (rb0 + i, 0)),
                      pl.BlockSpec((None, d, tn), lambda i, j, l: (l[0], 0, j)),
                      tab, tab, anyspec, anyspec, anyspec, anyspec] + [anyspec] * has_prev,
            out_specs=[rows_spec(0), rows_spec(3), heads_spec(1), heads_spec(2), heads_spec(4), heads_spec(5),
                       pl.BlockSpec((tm, tn), lambda i, j, l: (rb0 + i, clamp(j, 6, rest_tiles)))]),
        out_shape=[qshape, qshape] + [jax.ShapeDtypeStruct(b.shape, F32) for b in bufs]
        + [jax.ShapeDtypeStruct((rows_total, IN_REST_W), F32)],
        input_output_aliases={5: 2, 6: 3, 7: 4, 8: 5, **({9: 6} if has_prev else {})},
        compiler_params=_params(blk), name="in_proj",
    )(lidx, xn, w_in, cos, sin, *bufs, *([rest_prev] if has_prev else []))


def _post_mla_kernel(l_ref, cq_ref, ckv_ref, gq_ref, gkv_ref, lat_in, cqn_o, ckvb_o, lat_o):
    del l_ref, lat_in
    cqn_o[...] = _rms(cq_ref[...], gq_ref[...]).astype(BF16)
    c = _rms(ckv_ref[...], gkv_ref[...])
    lat_o[...] = c
    ckvb_o[...] = c.astype(BF16)


def _post_mla(proj, gq, gkv, lat_buf, lidx):
    m = proj.shape[0]
    tt = _tile(m, 640, V7X_BF16_SUBLANES)
    blk = 3 * _nbytes((tt, MLA_Q_LORA + MLA_KV_LORA), F32)
    return pl.pallas_call(
        _post_mla_kernel,
        grid_spec=pltpu.PrefetchScalarGridSpec(
            num_scalar_prefetch=1, grid=(m // tt,),
            in_specs=[pl.BlockSpec((tt, MLA_Q_LORA), lambda i, l: (i, 0)),
                      pl.BlockSpec((tt, MLA_KV_LORA), lambda i, l: (i, MLA_Q_LORA // MLA_KV_LORA)),
                      pl.BlockSpec((None, 1, MLA_Q_LORA), lambda i, l: (l[0], 0, 0)),
                      pl.BlockSpec((None, 1, MLA_KV_LORA), lambda i, l: (l[0], 0, 0)),
                      pl.BlockSpec(memory_space=pl.ANY)],
            out_specs=[pl.BlockSpec((tt, MLA_Q_LORA), lambda i, l: (i, 0)),
                       pl.BlockSpec((tt, MLA_KV_LORA), lambda i, l: (i, 0)),
                       pl.BlockSpec((None, tt, MLA_KV_LORA), lambda i, l: (l[0], i, 0))]),
        out_shape=[jax.ShapeDtypeStruct((m, MLA_Q_LORA), BF16),
                   jax.ShapeDtypeStruct((m, MLA_KV_LORA), BF16),
                   jax.ShapeDtypeStruct(lat_buf.shape, F32)],
        input_output_aliases={5: 2},
        compiler_params=_params(blk), name="post_mla",
    )(lidx, proj, proj, gq, gkv, lat_buf)


LOG2E = math.log2(math.e)
SB_DEAD_LOG = -104.0


def _with_ones(v):
    return jnp.concatenate([v, jnp.ones(v.shape, v.dtype)], axis=1)


def _softmax_step(s2, v_ext, m_ref, acc_ref):
    m_prev = m_ref[...]
    m_new = jnp.maximum(m_prev, jnp.max(s2, axis=-1, keepdims=True))
    alpha = jnp.exp2(m_prev - m_new)
    lanes = m_prev.shape[1]
    p = jnp.concatenate([jnp.exp2(s2[:, c * lanes:(c + 1) * lanes] - m_new).astype(BF16)
                         for c in range(s2.shape[1] // lanes)], axis=1)
    acc_ref[...] = jnp.concatenate([alpha, alpha], axis=1) * acc_ref[...] + _dot(p, v_ext)
    m_ref[...] = m_new


def _softmax_init(m_ref, acc_ref):
    m_ref[...] = jnp.full(m_ref.shape, NEG_INF, F32)
    acc_ref[...] = jnp.zeros(acc_ref.shape, F32)


def _softmax_result(acc_ref):
    acc = acc_ref[...]
    return acc[:, :HEAD_W] / acc[:, HEAD_W:]


def _positions(shape, q0, k0):
    qpos = q0 + lax.broadcasted_iota(jnp.int32, shape, 0)
    kpos = k0 + lax.broadcasted_iota(jnp.int32, shape, 1)
    return qpos, kpos


def _chunk_mask(shape, q0, k0):
    qpos, kpos = _positions(shape, q0, k0)
    return (kpos // CHUNK) <= (qpos // CHUNK)


def _diff_lambda(lam_ref, li_ref):
    lv = lam_ref[...]
    lam_init = li_ref[:, 0:1]
    d1 = jnp.sum(lv[0:1, :] * lv[1:2, :], axis=-1, keepdims=True)
    d2 = jnp.sum(lv[2:3, :] * lv[3:4, :], axis=-1, keepdims=True)
    return jnp.exp(d1) - jnp.exp(d2) + lam_init, lam_init


def _split_components(q):
    lane = lax.broadcasted_iota(jnp.int32, q.shape, 1)
    zero = jnp.zeros_like(q)
    return jnp.where(lane < DIFF_DH, q, zero), jnp.where(lane >= DIFF_DH, q, zero)


def _softplus(z):
    return jnp.maximum(z, 0.0) + jnp.log1p(jnp.exp(-jnp.abs(z)))


def _upper_ones(n):
    r = lax.broadcasted_iota(jnp.int32, (n, n), 0)
    c = lax.broadcasted_iota(jnp.int32, (n, n), 1)
    return jnp.where(r > c, 1.0, 0.0).astype(BF16)


def _sb_block(z, mask, carry, v, tri):
    sp = _softplus(z)
    log_stay = -sp if mask is None else jnp.where(mask, -sp, 0.0)
    hi = log_stay.astype(BF16)
    lo = (log_stay - hi.astype(F32)).astype(BF16)
    between = _dot(hi, tri) + _dot(lo, tri) + carry
    a = jnp.exp(z - sp + between)
    if mask is not None:
        a = jnp.where(mask, a, 0.0)
    return _dot(a.astype(BF16), v), carry + jnp.sum(log_stay, axis=-1, keepdims=True)


def _attn_tiles(t):
    tq = _tile(t, 512, CHUNK)
    return tq, tq


def _diff_prompt_kernel(l_ref, q_ref, k_ref, v_ref, lam_ref, g_ref, li_ref, bias_ref, o_ref, m_ref, acc_ref, s_ref,
                        *, tq, tk, hp):
    del l_ref
    assert tq == tk
    qi = pl.program_id(2)
    scale2 = DIFF_DH ** -0.5 * LOG2E
    for c in range(2 * hp):
        _softmax_init(m_ref.at[c], acc_ref.at[c])

    def rows(j):
        return pl.ds(pl.multiple_of(j * tk, tk), tk)

    def store_logits(j, hh):
        kb = k_ref[hh, rows(j), :].astype(BF16)
        q1, q2 = _split_components(q_ref[:, hh * HEAD_W:(hh + 1) * HEAD_W])
        s_ref[2 * hh] = _dot_nt(q1, kb) * scale2
        s_ref[2 * hh + 1] = _dot_nt(q2, kb) * scale2

    def step(j, hh, bias):
        v_ext = _with_ones(v_ref[hh, rows(j), :].astype(BF16))
        for c in (2 * hh, 2 * hh + 1):
            s2 = s_ref[c] if bias is None else s_ref[c] + bias
            _softmax_step(s2, v_ext, m_ref.at[c], acc_ref.at[c])

    for hh in range(hp):
        store_logits(0, hh)

    def body(j, carry):
        for hh in range(hp):
            step(j, hh, None)
            store_logits(j + 1, hh)
        return carry

    lax.fori_loop(0, qi, body, 0)
    lam, lam_init = _diff_lambda(lam_ref, li_ref)
    for hh in range(hp):
        step(qi, hh, bias_ref[...])
        o = _softmax_result(acc_ref.at[2 * hh]) - lam * _softmax_result(acc_ref.at[2 * hh + 1])
        o_ref[:, hh * HEAD_W:(hh + 1) * HEAD_W] = (_rms(o, g_ref[...]) * (1.0 - lam_init)).astype(o_ref.dtype)


def _diag_bias(tq, tk):
    r = jnp.arange(tq, dtype=jnp.int32)[:, None] // CHUNK
    c = jnp.arange(tk, dtype=jnp.int32)[None, :] // CHUNK
    return jnp.where(c <= r, 0.0, NEG_INF).astype(F32)


def _diff_prompt(q, k_buf, v_buf, lam, g, li, lidx, *, nb, t, rows_total):
    tq, tk = _attn_tiles(t)
    nq = t // tq
    hp = 2
    kv_spec = pl.BlockSpec((None, None, hp, t, HEAD_W), lambda b, h, i, l: (l[0], b, h, 0, 0))
    blk = 2 * hp * _nbytes((t, HEAD_W), F32) + (1 + 10 * hp) * _nbytes((tq, tk), F32)
    return pl.pallas_call(
        functools.partial(_diff_prompt_kernel, tq=tq, tk=tk, hp=hp),
        grid_spec=pltpu.PrefetchScalarGridSpec(
            num_scalar_prefetch=1, grid=(nb, DIFF_HEADS // hp, nq),
            in_specs=[pl.BlockSpec((tq, hp * HEAD_W), lambda b, h, i, l: (b * nq + i, h)),
                      kv_spec, kv_spec,
                      pl.BlockSpec((None, 4, DIFF_DH), lambda b, h, i, l: (l[0], 0, 0)),
                      pl.BlockSpec((None, 1, HEAD_W), lambda b, h, i, l: (l[0], 0, 0)),
                      pl.BlockSpec((None, 1, HEAD_W), lambda b, h, i, l: (l[0], 0, 0)),
                      pl.BlockSpec((tq, tk), lambda b, h, i, l: (0, 0))],
            out_specs=pl.BlockSpec((tq, hp * HEAD_W), lambda b, h, i, l: (b * nq + i, h)),
            scratch_shapes=[pltpu.VMEM((2 * hp, tq, HEAD_W), F32), pltpu.VMEM((2 * hp, tq, 2 * HEAD_W), F32),
                            pltpu.VMEM((2 * hp, tq, tk), F32)]),
        out_shape=jax.ShapeDtypeStruct((rows_total, DIFF_HEADS * HEAD_W), BF16),
        compiler_params=_params(blk), name="diff_prompt",
    )(lidx, q, k_buf, v_buf, lam, g, li, _diag_bias(tq, tk))


def _sb_prompt_kernel(l_ref, q_ref, k_ref, v_ref, o_ref, c_ref, acc_ref, *, tq, tk, hp):
    del l_ref
    qi = pl.program_id(2)
    ratio = tq // tk
    scale = HEAD_W ** -0.5
    tri = _upper_ones(tk)
    c_ref[...] = jnp.zeros(c_ref.shape, F32)
    acc_ref[...] = jnp.zeros(acc_ref.shape, F32)

    def block(j, masked):
        start = pl.multiple_of(j * tk, tk)
        mask = None
        if masked:
            qpos, kpos = _positions((tq, tk), qi * tq, start)
            mask = kpos < qpos
        for hh in range(hp):
            kb = k_ref[hh, pl.ds(start, tk), :].astype(BF16)
            vb = v_ref[hh, pl.ds(start, tk), :].astype(BF16)
            z = _dot_nt(q_ref[:, hh * HEAD_W:(hh + 1) * HEAD_W], kb) * scale
            out, carry = _sb_block(z, mask, c_ref[hh], vb, tri)
            acc_ref[hh] += out
            c_ref[hh] = carry

    for u in reversed(range(ratio)):
        block(qi * ratio + u, True)

    def alive():
        return jnp.max(c_ref[...]) > SB_DEAD_LOG

    def cond(state):
        j, live = state
        return jnp.logical_and(j >= 0, live)

    def body(state):
        j, _ = state
        block(j, False)
        return j - 1, alive()

    lax.while_loop(cond, body, (qi * ratio - 1, alive()))
    for hh in range(hp):
        o_ref[:, hh * HEAD_W:(hh + 1) * HEAD_W] = acc_ref[hh].astype(o_ref.dtype)


def _sb_prompt(q, k_buf, v_buf, lidx, *, nb, t, rows_total):
    tq, _ = _attn_tiles(t)
    tk = _tile(tq, 256, CHUNK)
    nq = t // tq
    hp = 2
    kv_spec = pl.BlockSpec((None, None, hp, t, HEAD_W), lambda b, h, i, l: (l[0], b, h, 0, 0))
    blk = 2 * hp * _nbytes((t, HEAD_W), F32) + 16 * hp * _nbytes((tq, tk), F32)
    return pl.pallas_call(
        functools.partial(_sb_prompt_kernel, tq=tq, tk=tk, hp=hp),
        grid_spec=pltpu.PrefetchScalarGridSpec(
            num_scalar_prefetch=1, grid=(nb, SB_HEADS // hp, nq),
            in_specs=[pl.BlockSpec((tq, hp * HEAD_W), lambda b, h, i, l: (b * nq + i, h)), kv_spec, kv_spec],
            out_specs=pl.BlockSpec((tq, hp * HEAD_W), lambda b, h, i, l: (b * nq + i, h)),
            scratch_shapes=[pltpu.VMEM((hp, tq, 1), F32), pltpu.VMEM((hp, tq, HEAD_W), F32)]),
        out_shape=jax.ShapeDtypeStruct((rows_total, SB_HEADS * HEAD_W), BF16),
        compiler_params=_params(blk), name="sb_prompt",
    )(lidx, q, k_buf, v_buf)


def _mla_prompt_kernel(q_ref, kv_ref, kr_ref, bias_ref, o_ref, m_ref, acc_ref, s_ref, *, tq, tk, hp):
    assert tq == tk
    qi = pl.program_id(2)
    scale2 = (HEAD_W + MLA_ROPE) ** -0.5 * LOG2E
    for hh in range(hp):
        _softmax_init(m_ref.at[hh], acc_ref.at[hh])

    def rows(j):
        return pl.ds(pl.multiple_of(j * tk, tk), tk)

    def logits(j, hh):
        kcat = jnp.concatenate([kv_ref[rows(j), 2 * hh * HEAD_W:(2 * hh + 1) * HEAD_W], kr_ref[rows(j), :]], axis=1)
        return _dot_nt(q_ref[:, hh * MLA_QK_PAD:(hh + 1) * MLA_QK_PAD], kcat) * scale2

    def values(j, hh):
        return _with_ones(kv_ref[rows(j), (2 * hh + 1) * HEAD_W:(2 * hh + 2) * HEAD_W])

    for hh in range(hp):
        s_ref[hh] = logits(0, hh)

    def body(j, c):
        for hh in range(hp):
            _softmax_step(s_ref[hh], values(j, hh), m_ref.at[hh], acc_ref.at[hh])
            s_ref[hh] = logits(j + 1, hh)
        return c

    lax.fori_loop(0, qi, body, 0)
    for hh in range(hp):
        _softmax_step(s_ref[hh] + bias_ref[...], values(qi, hh), m_ref.at[hh], acc_ref.at[hh])
        o_ref[:, hh * HEAD_W:(hh + 1) * HEAD_W] = _softmax_result(acc_ref.at[hh]).astype(o_ref.dtype)


def _mla_prompt(qcat, kv, kr128, *, nb, t, rows_total):
    tq, tk = _attn_tiles(t)
    nq = t // tq
    hp = 4
    blk = (2 * hp + 1) * _nbytes((t, HEAD_W), BF16) + (1 + 6 * hp) * _nbytes((tq, tk), F32)
    return pl.pallas_call(
        functools.partial(_mla_prompt_kernel, tq=tq, tk=tk, hp=hp),
        grid=(nb, MLA_HEADS // hp, nq),
        in_specs=[pl.BlockSpec((tq, hp * MLA_QK_PAD), lambda b, h, i: (b * nq + i, h)),
                  pl.BlockSpec((t, 2 * hp * HEAD_W), lambda b, h, i: (b, h)),
                  pl.BlockSpec((t, HEAD_W), lambda b, h, i: (b, 0)),
                  pl.BlockSpec((tq, tk), lambda b, h, i: (0, 0))],
        out_specs=pl.BlockSpec((tq, hp * HEAD_W), lambda b, h, i: (b * nq + i, h)),
        scratch_shapes=[pltpu.VMEM((hp, tq, HEAD_W), F32), pltpu.VMEM((hp, tq, 2 * HEAD_W), F32),
                        pltpu.VMEM((hp, tq, tk), F32)],
        out_shape=jax.ShapeDtypeStruct((rows_total, MLA_HEADS * HEAD_W), BF16),
        compiler_params=_params(blk), name="mla_prompt",
    )(qcat, kv, kr128, _diag_bias(tq, tk))


def _pad_rows(x, rows):
    return jnp.concatenate([x, jnp.zeros((rows - x.shape[0], x.shape[1]), x.dtype)], axis=0)


def _two_part_softmax(sp, sn, vp, vn):
    m = jnp.maximum(jnp.max(sp, axis=-1, keepdims=True), jnp.max(sn, axis=-1, keepdims=True))
    pp, pn = jnp.exp(sp - m), jnp.exp(sn - m)
    denom = jnp.sum(pp, axis=-1, keepdims=True) + jnp.sum(pn, axis=-1, keepdims=True)
    return (_dot(pp.astype(BF16), vp) + _dot(pn.astype(BF16), vn)) / denom


def _diff_sample_kernel(l_ref, q_ref, kp_ref, vp_ref, kn_ref, vn_ref, lam_ref, g_ref, li_ref, prev_ref, o_ref,
                        *, past, t):
    del l_ref, prev_ref
    q1, q2 = _split_components(q_ref[...])
    scale = DIFF_DH ** -0.5
    kp, vp = kp_ref[...].astype(BF16), vp_ref[...].astype(BF16)
    kn = _pad_rows(kn_ref[...], HEAD_W).astype(BF16)
    vn = _pad_rows(vn_ref[...], HEAD_W).astype(BF16)
    mask_p = _chunk_mask((t, past), past, 0)
    qpos, kpos = _positions((t, HEAD_W), past, past)
    mask_n = ((kpos // CHUNK) <= (qpos // CHUNK)) & (kpos < past + t)

    def attend(qx):
        sp = jnp.where(mask_p, _dot_nt(qx, kp) * scale, NEG_INF)
        sn = jnp.where(mask_n, _dot_nt(qx, kn) * scale, NEG_INF)
        return _two_part_softmax(sp, sn, vp, vn)

    lam, lam_init = _diff_lambda(lam_ref, li_ref)
    o = attend(q1) - lam * attend(q2)
    o_ref[...] = (_rms(o, g_ref[...]) * (1.0 - lam_init)).astype(o_ref.dtype)


def _diff_sample(q, k_cache, v_cache, k_buf, v_buf, lam, g, li, merged, lidx, *, nb, t, past, row0):
    rb0 = row0 // t
    cache_spec = pl.BlockSpec((None, None, None, past, HEAD_W), lambda b, h, l: (l[0], b, h, 0, 0))
    new_spec = pl.BlockSpec((None, None, None, t, HEAD_W), lambda b, h, l: (l[0], b, h, 0, 0))
    blk = 2 * _nbytes((past, HEAD_W), F32) + 8 * _nbytes((t, past), F32)
    return pl.pallas_call(
        functools.partial(_diff_sample_kernel, past=past, t=t),
        grid_spec=pltpu.PrefetchScalarGridSpec(
            num_scalar_prefetch=1, grid=(nb, DIFF_HEADS),
            in_specs=[pl.BlockSpec((t, HEAD_W), lambda b, h, l: (b, h)),
                      cache_spec, cache_spec, new_spec, new_spec,
                      pl.BlockSpec((None, 4, DIFF_DH), lambda b, h, l: (l[0], 0, 0)),
                      pl.BlockSpec((None, 1, HEAD_W), lambda b, h, l: (l[0], 0, 0)),
                      pl.BlockSpec((None, 1, HEAD_W), lambda b, h, l: (l[0], 0, 0)),
                      pl.BlockSpec(memory_space=pl.ANY)],
            out_specs=pl.BlockSpec((t, HEAD_W), lambda b, h, l: (rb0 + b, h))),
        out_shape=jax.ShapeDtypeStruct(merged.shape, BF16),
        input_output_aliases={9: 0},
        compiler_params=_params(blk), name="diff_sample",
    )(lidx, q, k_cache, v_cache, k_buf, v_buf, lam, g, li, merged)


def _sb_sample_kernel(l_ref, q_ref, kp_ref, vp_ref, kn_ref, vn_ref, prev_ref, o_ref, *, past, t, cw):
    del l_ref, prev_ref
    q = q_ref[...]
    scale = HEAD_W ** -0.5
    kn = _pad_rows(kn_ref[...], HEAD_W).astype(BF16)
    vn = _pad_rows(vn_ref[...], HEAD_W).astype(BF16)
    qpos, kpos = _positions((t, HEAD_W), past, past)
    mask_n = (kpos < qpos) & (kpos < past + t)
    acc, carry = _sb_block(_dot_nt(q, kn) * scale, mask_n, jnp.zeros((t, 1), F32), vn, _upper_ones(HEAD_W))
    tri = _upper_ones(cw)
    for c in reversed(range(past // cw)):
        kb = kp_ref[c * cw:(c + 1) * cw, :].astype(BF16)
        vb = vp_ref[c * cw:(c + 1) * cw, :].astype(BF16)
        qpos, kpos = _positions((t, cw), past, c * cw)
        out, carry = _sb_block(_dot_nt(q, kb) * scale, kpos < qpos, carry, vb, tri)
        acc = acc + out
    o_ref[...] = acc.astype(o_ref.dtype)


def _sb_sample(q, k_cache, v_cache, k_buf, v_buf, merged, lidx, *, nb, t, past, row0):
    rb0 = row0 // t
    cw = _tile(past, 256, V7X_LANES)
    cache_spec = pl.BlockSpec((None, None, None, past, HEAD_W), lambda b, h, l: (l[0], b, h, 0, 0))
    new_spec = pl.BlockSpec((None, None, None, t, HEAD_W), lambda b, h, l: (l[0], b, h, 0, 0))
    blk = 2 * _nbytes((past, HEAD_W), F32) + 8 * _nbytes((t, past), F32)
    return pl.pallas_call(
        functools.partial(_sb_sample_kernel, past=past, t=t, cw=cw),
        grid_spec=pltpu.PrefetchScalarGridSpec(
            num_scalar_prefetch=1, grid=(nb, SB_HEADS),
            in_specs=[pl.BlockSpec((t, HEAD_W), lambda b, h, l: (b, h)),
                      cache_spec, cache_spec, new_spec, new_spec, pl.BlockSpec(memory_space=pl.ANY)],
            out_specs=pl.BlockSpec((t, HEAD_W), lambda b, h, l: (rb0 + b, h))),
        out_shape=jax.ShapeDtypeStruct(merged.shape, BF16),
        input_output_aliases={6: 0},
        compiler_params=_params(blk), name="sb_sample",
    )(lidx, q, k_cache, v_cache, k_buf, v_buf, merged)


def _mla_sample_kernel(l_ref, q_ref, knp_ref, krp_ref, vp_ref, knn_ref, krn_ref, vn_ref, prev_ref, o_ref,
                       *, past, t):
    del l_ref, prev_ref
    q = q_ref[...]
    scale = (HEAD_W + MLA_ROPE) ** -0.5
    q_nope, q_rope = q[:, :HEAD_W], q[:, HEAD_W:HEAD_W + MLA_ROPE]
    kn = _pad_rows(jnp.concatenate([knn_ref[...], krn_ref[...]], axis=1), HEAD_W)
    vn = _pad_rows(vn_ref[...], HEAD_W)
    qpos, kpos = _positions((t, HEAD_W), past, past)
    mask_n = ((kpos // CHUNK) <= (qpos // CHUNK)) & (kpos < past + t)
    s_past = _dot_nt(q_nope, knp_ref[...]) + _dot_nt(q_rope, krp_ref[...].astype(BF16))
    sp = jnp.where(_chunk_mask((t, past), past, 0), s_past * scale, NEG_INF)
    sn = jnp.where(mask_n, _dot_nt(q, kn) * scale, NEG_INF)
    o_ref[...] = _two_part_softmax(sp, sn, vp_ref[...], vn).astype(o_ref.dtype)


def _mla_sample(qcat, kv_past, kr_cache, kv_new, kr_new128, merged, lidx, *, nb, t, past, row0):
    rb0 = row0 // t
    blk = 3 * _nbytes((past, HEAD_W), BF16) + 8 * _nbytes((t, past), F32)
    return pl.pallas_call(
        functools.partial(_mla_sample_kernel, past=past, t=t),
        grid_spec=pltpu.PrefetchScalarGridSpec(
            num_scalar_prefetch=1, grid=(nb, MLA_HEADS),
            in_specs=[pl.BlockSpec((t, MLA_QK_PAD), lambda b, h, l: (rb0 + b, h)),
                      pl.BlockSpec((past, HEAD_W), lambda b, h, l: (b, 2 * h)),
                      pl.BlockSpec((None, None, past, MLA_ROPE), lambda b, h, l: (l[0], b, 0, 0)),
                      pl.BlockSpec((past, HEAD_W), lambda b, h, l: (b, 2 * h + 1)),
                      pl.BlockSpec((t, HEAD_W), lambda b, h, l: (rb0 + b, 2 * h)),
                      pl.BlockSpec((t, HEAD_W), lambda b, h, l: (rb0 + b, 0)),
                      pl.BlockSpec((t, HEAD_W), lambda b, h, l: (rb0 + b, 2 * h + 1)),
                      pl.BlockSpec(memory_space=pl.ANY)],
            out_specs=pl.BlockSpec((t, HEAD_W), lambda b, h, l: (rb0 + b, h))),
        out_shape=jax.ShapeDtypeStruct(merged.shape, BF16),
        input_output_aliases={8: 0},
        compiler_params=_params(blk), name="mla_sample",
    )(lidx, qcat, kv_past, kr_cache, kv_past, kv_new, kr_new128, kv_new, merged)


def _gate_merge_kernel(l_ref, xn_ref, oa_ref, ob_ref, oc_ref, wga_ref, wgb_ref, wgc_ref,
                       wa_ref, wb_ref, wc_ref, o_ref):
    del l_ref
    xn = xn_ref[...]

    def branch(wg_ref, mix_ref, w_ref):
        return jax.nn.sigmoid(_dot(xn, wg_ref[...])) * _dot(mix_ref[...], w_ref[...])

    merged = branch(wga_ref, oa_ref, wa_ref) + branch(wgb_ref, ob_ref, wb_ref) + branch(wgc_ref, oc_ref, wc_ref)
    o_ref[...] = merged.astype(o_ref.dtype)


def _gate_merge(xn, oa, ob, oc, w_gate, w_a, w_b, w_c, lidx):
    m, d = xn.shape
    tm = _tile(m, 640, V7X_BF16_SUBLANES)
    tn = _tile(d, 256, V7X_LANES)
    nj = d // tn

    def rows(width):
        return pl.BlockSpec((tm, width), lambda i, j, l: (i, 0))

    def gate(branch):
        return pl.BlockSpec((None, d, tn), lambda i, j, l: (l[0], 0, branch * nj + j))

    def proj(kdim):
        return pl.BlockSpec((None, kdim, tn), lambda i, j, l: (l[0], 0, j))

    wa, wb, wc = oa.shape[1], ob.shape[1], oc.shape[1]
    blk = (_nbytes((tm, d + wa + wb + wc), BF16) + _nbytes((3 * d + wa + wb + wc, tn), BF16)
           + 4 * _nbytes((tm, tn), F32))
    return pl.pallas_call(
        _gate_merge_kernel,
        grid_spec=pltpu.PrefetchScalarGridSpec(
            num_scalar_prefetch=1, grid=(m // tm, nj),
            in_specs=[rows(d), rows(wa), rows(wb), rows(wc), gate(0), gate(1), gate(2),
                      proj(wa), proj(wb), proj(wc)],
            out_specs=pl.BlockSpec((tm, tn), lambda i, j, l: (i, j))),
        out_shape=jax.ShapeDtypeStruct((m, d), BF16),
        compiler_params=_params(blk), name="gate_merge",
    )(lidx, xn, oa, ob, oc, w_gate, w_gate, w_gate, w_a, w_b, w_c)


def _swiglu_kernel(l_ref, x_ref, w1_ref, w3_ref, o_ref):
    del l_ref
    x = x_ref[...]
    o_ref[...] = (jax.nn.silu(_dot(x, w1_ref[...].astype(BF16)))
                  * _dot(x, w3_ref[...].astype(BF16))).astype(o_ref.dtype)


def _swiglu(xn, w1, w3, lidx):
    m, d = xn.shape
    f = w1.shape[2]
    tm = _tile(m, 1040, V7X_BF16_SUBLANES)
    tn = _tile(f, 256, V7X_LANES)
    wspec = pl.BlockSpec((None, d, tn), lambda i, j, l: (l[0], 0, j))
    blk = _nbytes((tm, d), BF16) + 3 * _nbytes((d, tn), w1.dtype) + 4 * _nbytes((tm, tn), F32)
    return pl.pallas_call(
        _swiglu_kernel,
        grid_spec=pltpu.PrefetchScalarGridSpec(
            num_scalar_prefetch=1, grid=(m // tm, f // tn),
            in_specs=[pl.BlockSpec((tm, d), lambda i, j, l: (i, 0)), wspec, wspec],
            out_specs=pl.BlockSpec((tm, tn), lambda i, j, l: (i, j))),
        out_shape=jax.ShapeDtypeStruct((m, f), BF16),
        compiler_params=_params(blk), name="swiglu",
    )(lidx, xn, w1, w3)


def _rope_tables(pos):
    inv_freq = ROPE_THETA ** (-jnp.arange(0, MLA_ROPE, 2, dtype=F32) / MLA_ROPE)
    ang = pos.astype(F32)[:, None] * inv_freq[None, :]
    c, s = jnp.cos(ang), jnp.sin(ang)
    one, zero = jnp.ones_like(c), jnp.zeros_like(c)
    cos128 = jnp.concatenate([c, c, c, c], axis=-1)
    sin128 = jnp.concatenate([-s, s, -s, s], axis=-1)
    cos256 = jnp.concatenate([one, one, one, one, c, c, one, one], axis=-1)
    sin256 = jnp.concatenate([zero, zero, zero, zero, -s, s, zero, zero], axis=-1)
    return cos128, sin128, cos256, sin256


def kernel(x_prompt, x_sample, cache_diff_k, cache_diff_v, cache_sb_k, cache_sb_v, cache_mla_latent,
           cache_mla_krope, attn_norm, w_in, diff_lambda, diff_subln, mla_q_norm, mla_w_uq, mla_kv_norm,
           mla_w_ukv, w_gate, w_branch_a, w_branch_b, w_branch_c, w_out, ffn_norm, ffn_w1, ffn_w3, ffn_w2,
           final_norm):
    nbp, tp, d = x_prompt.shape
    nbs, ts = x_sample.shape[:2]
    past = cache_diff_k.shape[3]
    depth = w_in.shape[0]
    mp, ms = nbp * tp, nbs * ts
    m = mp + ms

    h0 = jnp.concatenate([x_prompt.reshape(mp, d), x_sample.reshape(ms, d)], axis=0)
    pos = jnp.concatenate([jnp.tile(jnp.arange(tp, dtype=jnp.int32), nbp),
                           jnp.tile(past + jnp.arange(ts, dtype=jnp.int32), nbs)])
    cos128, sin128, cos256, sin256 = _rope_tables(pos)

    w_kr_pad = jnp.pad(w_in[:, :, COL_KR:], ((0, 0), (0, 0), (0, HEAD_W - MLA_ROPE))).astype(BF16)
    uq = mla_w_uq.astype(BF16).reshape(depth, MLA_Q_LORA, MLA_HEADS, HEAD_W + MLA_ROPE)
    w_uq_pad = jnp.pad(uq, ((0, 0), (0, 0), (0, 0), (0, MLA_QK_PAD - HEAD_W - MLA_ROPE))).reshape(
        depth, MLA_Q_LORA, MLA_HEADS * MLA_QK_PAD)
    w_gate_b, w_a, w_b, w_c = (_to_bf16(w) for w in (w_gate, w_branch_a, w_branch_b, w_branch_c))
    w2 = _to_bf16(ffn_w2)
    lat_cache = cache_mla_latent.reshape(depth, nbs * past, MLA_KV_LORA)

    attn_g = attn_norm.reshape(depth, 1, d)
    ffn_g = ffn_norm.reshape(depth, 1, d)
    gq = mla_q_norm.reshape(depth, 1, MLA_Q_LORA)
    gkv = mla_kv_norm.reshape(depth, 1, MLA_KV_LORA)
    subln = diff_subln.reshape(depth, 1, HEAD_W)
    lam_init = jnp.asarray([0.8 - 0.6 * math.exp(-0.3 * l) for l in range(depth)], F32)
    lam_init = jnp.broadcast_to(lam_init[:, None, None], (depth, 1, HEAD_W))

    def head_bufs(nb, t):
        return tuple(jnp.zeros((depth, nb, DIFF_HEADS, t, HEAD_W), F32) for _ in range(4))

    def layer(l, carry):
        h, p_bufs, s_bufs, lat_buf, kr_buf = carry
        lidx = jnp.reshape(l, (1,)).astype(jnp.int32)
        xn = _rmsnorm_rows(h, attn_g, lidx, BF16)
        qa_p, qb_p, *p_bufs, rest = _in_proj(xn, w_in, cos128, sin128, p_bufs, lidx, row0=0, nb=nbp, t=tp,
                                             rows_total=m)
        qa_s, qb_s, *s_bufs, rest = _in_proj(xn, w_in, cos128, sin128, s_bufs, lidx, row0=mp, nb=nbs, t=ts,
                                             rows_total=m, rest_prev=rest)
        kr_buf, kr128 = _shared_rope_key(xn, w_kr_pad, cos128, sin128, kr_buf, lidx)
        cqn, ckv_b, lat_buf = _post_mla(rest, gq, gkv, lat_buf, lidx)
        qcat = _matmul(cqn, w_uq_pad, lidx, out_dtype=BF16, tm_cap=1040, tn_cap=1024,
                       rope=(cos256, sin256), name="mla_q_up")
        kv = _matmul(ckv_b, mla_w_ukv, lidx, out_dtype=BF16, tm_cap=1040, tn_cap=1024, name="mla_kv_up")
        kv_past = _matmul(lat_cache, mla_w_ukv, lidx, out_dtype=BF16, tm_cap=1024, tn_cap=1024,
                          x_stacked=True, name="mla_kv_up_cache")
        oa = _diff_prompt(qa_p, p_bufs[0], p_bufs[1], diff_lambda, subln, lam_init, lidx, nb=nbp, t=tp, rows_total=m)
        oa = _diff_sample(qa_s, cache_diff_k, cache_diff_v, s_bufs[0], s_bufs[1], diff_lambda, subln,
                          lam_init, oa, lidx, nb=nbs, t=ts, past=past, row0=mp)
        ob = _sb_prompt(qb_p, p_bufs[2], p_bufs[3], lidx, nb=nbp, t=tp, rows_total=m)
        ob = _sb_sample(qb_s, cache_sb_k, cache_sb_v, s_bufs[2], s_bufs[3], ob, lidx, nb=nbs, t=ts, past=past,
                        row0=mp)
        oc = _mla_prompt(qcat, kv, kr128, nb=nbp, t=tp, rows_total=m)
        oc = _mla_sample(qcat, kv_past, cache_mla_krope, kv, kr128, oc, lidx, nb=nbs, t=ts, past=past, row0=mp)
        merged = _gate_merge(xn, oa, ob, oc, w_gate_b, w_a, w_b, w_c, lidx)
        h = _matmul(merged, w_out, lidx, out_dtype=F32, tm_cap=1040, tn_cap=512, res=h, name="out_proj")
        xn2 = _rmsnorm_rows(h, ffn_g, lidx, BF16)
        hid = _swiglu(xn2, ffn_w1, ffn_w3, lidx)
        h = _matmul(hid, w2, lidx, out_dtype=F32, tm_cap=640, tn_cap=256, res=h, name="ffn_down")
        return h, tuple(p_bufs), tuple(s_bufs), lat_buf, kr_buf

    carry = (h0, head_bufs(nbp, tp), head_bufs(nbs, ts),
             jnp.zeros((depth, m, MLA_KV_LORA), F32), jnp.zeros((depth, m, MLA_ROPE), F32))
    h, p_bufs, s_bufs, lat_buf, kr_buf = lax.fori_loop(0, depth, layer, carry)

    final_g, l0 = final_norm.reshape(1, 1, d), jnp.zeros((1,), jnp.int32)
    y_prompt = _rmsnorm_rows(h, final_g, l0, F32, row0=0, rows=mp).reshape(nbp, tp, d)
    y_sample = _rmsnorm_rows(h, final_g, l0, F32, row0=mp, rows=ms).reshape(nbs, ts, d)
    p_lat = lat_buf[:, :mp].reshape(depth, nbp, tp, MLA_KV_LORA)
    s_lat = lat_buf[:, mp:].reshape(depth, nbs, ts, MLA_KV_LORA)
    p_kr = kr_buf[:, :mp].reshape(depth, nbp, tp, MLA_ROPE)
    s_kr = kr_buf[:, mp:].reshape(depth, nbs, ts, MLA_ROPE)
    return (y_prompt, y_sample, *p_bufs, p_lat, p_kr, *s_bufs, s_lat, s_kr)
```

```python
import functools
import math

import jax
import jax.numpy as jnp
from jax import lax
from jax.experimental import pallas as pl
from jax.experimental.pallas import tpu as pltpu

D_MODEL = 4096
BATCH = 2
SEQ = 4096
DEPTH = 4
DEC_BATCH = 8
DEC_SEQ = 16
PAST_LEN = 2048

CHUNK = 64
ROPE_THETA = 10000.0
NORM_EPS = 1e-6
NEG_INF = -1e30

DIFF_HEADS = 8
DIFF_DH = 64
SB_HEADS = 8
MLA_HEADS = 16
MLA_ROPE = 64
MLA_Q_LORA = 1024
MLA_KV_LORA = 512
HEAD_W = 128
MLA_QK_PAD = 2 * HEAD_W
N_BRANCHES = 3
COL_CQ, COL_KR = 6144, 7680
FFN_HIDDEN = -(-8 * D_MODEL // 768) * 256

F32 = jnp.float32
BF16 = jnp.bfloat16

V7X_LANES = 128
V7X_BF16_SUBLANES = 16
V7X_VMEM_LIMIT_CAP = 60 * 1024 * 1024


def _tile(n, cap, mult):
    best = None
    for t in range(mult, min(n, cap) + 1, mult):
        if n % t == 0:
            best = t
    if best is None:
        raise ValueError(f"no tile for {n} (cap {cap}, multiple of {mult})")
    return best


def _params(block_bytes):
    need = 2 * block_bytes + (8 << 20)
    return pltpu.CompilerParams(vmem_limit_bytes=int(min(max(need, 32 << 20), V7X_VMEM_LIMIT_CAP)))


def _nbytes(shape, dtype):
    return math.prod(shape) * jnp.dtype(dtype).itemsize


def _dot(a, b):
    return jnp.dot(a, b, preferred_element_type=F32)


def _dot_nt(a, b):
    return lax.dot_general(a, b, (((1,), (1,)), ((), ())), preferred_element_type=F32)


def _rms(x, g):
    return x * lax.rsqrt(jnp.mean(x * x, axis=-1, keepdims=True) + NORM_EPS) * g


def _rope_lanes(x, cos, sin):
    lane = lax.broadcasted_iota(jnp.int32, x.shape, 1)
    first_half = (lane % 64) < 32
    partner = jnp.where(first_half, pltpu.roll(x, x.shape[1] - 32, 1), pltpu.roll(x, 32, 1))
    return x * cos + partner * sin


def _rmsnorm_kernel(l_ref, x_ref, g_ref, o_ref):
    del l_ref
    o_ref[...] = _rms(x_ref[...], g_ref[...]).astype(o_ref.dtype)


def _rmsnorm_rows(x, g_stack, lidx, out_dtype, row0=0, rows=None):
    d = x.shape[1]
    m = x.shape[0] if rows is None else rows
    tr = _tile(math.gcd(m, row0) if row0 else m, 320, V7X_BF16_SUBLANES)
    rb0 = row0 // tr
    blk = _nbytes((tr, d), F32) + _nbytes((tr, d), out_dtype)
    return pl.pallas_call(
        _rmsnorm_kernel,
        grid_spec=pltpu.PrefetchScalarGridSpec(
            num_scalar_prefetch=1, grid=(m // tr,),
            in_specs=[pl.BlockSpec((tr, d), lambda i, l: (rb0 + i, 0)),
                      pl.BlockSpec((None, 1, d), lambda i, l: (l[0], 0, 0))],
            out_specs=pl.BlockSpec((tr, d), lambda i, l: (i, 0))),
        out_shape=jax.ShapeDtypeStruct((m, d), out_dtype),
        compiler_params=_params(blk), name="rmsnorm_rows",
    )(lidx, x, g_stack)


def _cast_kernel(x_ref, o_ref):
    o_ref[...] = x_ref[...].astype(o_ref.dtype)


def _to_bf16(w):
    depth, kdim, n = w.shape
    rows = depth * kdim
    tr = _tile(rows, max(V7X_BF16_SUBLANES, (8 << 20) // (4 * n)), V7X_BF16_SUBLANES)
    out = pl.pallas_call(
        _cast_kernel, grid=(rows // tr,),
        in_specs=[pl.BlockSpec((tr, n), lambda i: (i, 0))],
        out_specs=pl.BlockSpec((tr, n), lambda i: (i, 0)),
        out_shape=jax.ShapeDtypeStruct((rows, n), BF16),
        compiler_params=_params(_nbytes((tr, n), F32) + _nbytes((tr, n), BF16)), name="to_bf16",
    )(w.reshape(rows, n))
    return out.reshape(depth, kdim, n)


def _mm_kernel(l_ref, x_ref, w_ref, *rest, nk, has_res, rope):
    del l_ref
    rest = list(rest)
    r_ref = rest.pop(0) if has_res else None
    cos_ref, sin_ref = (rest.pop(0), rest.pop(0)) if rope else (None, None)
    o_ref = rest.pop(0)
    part = _dot(x_ref[...].astype(BF16), w_ref[...].astype(BF16))

    def finish(acc):
        if has_res:
            acc = acc + r_ref[...]
        if rope:
            cos, sin = cos_ref[...], sin_ref[...]
            gw = cos.shape[1]
            for g in range(acc.shape[1] // gw):
                seg = acc[:, g * gw:(g + 1) * gw]
                o_ref[:, g * gw:(g + 1) * gw] = _rope_lanes(seg, cos, sin).astype(o_ref.dtype)
        else:
            o_ref[...] = acc.astype(o_ref.dtype)

    if nk == 1:
        finish(part)
    else:
        acc_ref, = rest
        k = pl.program_id(2)

        @pl.when(k == 0)
        def _():
            acc_ref[...] = part

        @pl.when(k > 0)
        def _():
            acc_ref[...] += part

        @pl.when(k == nk - 1)
        def _():
            finish(acc_ref[...])


def _matmul(x, w_stack, lidx, *, out_dtype, tm_cap, tn_cap, tk=None, res=None, rope=None,
            x_stacked=False, name="matmul"):
    m, kdim = x.shape[-2:]
    n = w_stack.shape[2]
    tm = _tile(m, tm_cap, V7X_BF16_SUBLANES)
    tn = _tile(n, tn_cap, V7X_LANES) if n % V7X_LANES == 0 else n
    tk = kdim if tk is None else tk
    nk = kdim // tk
    if x_stacked:
        x_spec = pl.BlockSpec((None, tm, tk), lambda i, j, k, l: (l[0], i, k))
    else:
        x_spec = pl.BlockSpec((tm, tk), lambda i, j, k, l: (i, k))
    in_specs = [x_spec, pl.BlockSpec((None, tk, tn), lambda i, j, k, l: (l[0], k, j))]
    args = [x, w_stack]
    blk = _nbytes((tm, tk), x.dtype) + _nbytes((tk, tn), w_stack.dtype) + _nbytes((tm, tn), F32)
    if res is not None:
        in_specs.append(pl.BlockSpec((tm, tn), lambda i, j, k, l: (i, j)))
        args.append(res)
        blk += _nbytes((tm, tn), F32)
    if rope is not None:
        gw = rope[0].shape[1]
        assert tn % gw == 0
        in_specs += [pl.BlockSpec((tm, gw), lambda i, j, k, l: (i, 0))] * 2
        args += list(rope)
        blk += 2 * _nbytes((tm, gw), F32)
    return pl.pallas_call(
        functools.partial(_mm_kernel, nk=nk, has_res=res is not None, rope=rope is not None),
        grid_spec=pltpu.PrefetchScalarGridSpec(
            num_scalar_prefetch=1, grid=(m // tm, n // tn, nk),
            in_specs=in_specs,
            out_specs=pl.BlockSpec((tm, tn), lambda i, j, k, l: (i, j)),
            scratch_shapes=[pltpu.VMEM((tm, tn), F32)] if nk > 1 else []),
        out_shape=jax.ShapeDtypeStruct((m, n), out_dtype),
        compiler_params=_params(blk), name=name,
    )(lidx, *args)


def _kr_kernel(l_ref, xn_ref, w_ref, cos_ref, sin_ref, kr_in, kr_o, kr128_o):
    del l_ref, kr_in
    r = _rope_lanes(_dot(xn_ref[...], w_ref[...]), cos_ref[...], sin_ref[...])
    kr_o[...] = r[:, :MLA_ROPE]
    kr128_o[...] = r.astype(BF16)


def _shared_rope_key(xn, w_kr_pad, cos, sin, kr_buf, lidx):
    m, d = xn.shape
    tm = _tile(m, 640, V7X_BF16_SUBLANES)
    blk = _nbytes((tm, d), BF16) + _nbytes((d, HEAD_W), BF16) + 4 * _nbytes((tm, HEAD_W), F32)
    return pl.pallas_call(
        _kr_kernel,
        grid_spec=pltpu.PrefetchScalarGridSpec(
            num_scalar_prefetch=1, grid=(m // tm,),
            in_specs=[pl.BlockSpec((tm, d), lambda i, l: (i, 0)),
                      pl.BlockSpec((None, d, HEAD_W), lambda i, l: (l[0], 0, 0)),
                      pl.BlockSpec((tm, HEAD_W), lambda i, l: (i, 0)),
                      pl.BlockSpec((tm, HEAD_W), lambda i, l: (i, 0)),
                      pl.BlockSpec(memory_space=pl.ANY)],
            out_specs=[pl.BlockSpec((None, tm, MLA_ROPE), lambda i, l: (l[0], i, 0)),
                       pl.BlockSpec((tm, HEAD_W), lambda i, l: (i, 0))]),
        out_shape=[jax.ShapeDtypeStruct(kr_buf.shape, F32), jax.ShapeDtypeStruct((m, HEAD_W), BF16)],
        input_output_aliases={5: 0},
        compiler_params=_params(blk), name="shared_rope_key",
    )(lidx, xn, w_kr_pad, cos, sin, kr_buf)


IN_SECTION_W = DIFF_HEADS * HEAD_W
IN_REST_W = COL_KR - COL_CQ
MIN_SUBTILE_ROWS = 256


def _section_kernel(l_ref, x_ref, w_ref, *refs, rope, heads, subtiles):
    del l_ref
    refs = list(refs)
    cos_ref, sin_ref = (refs.pop(0), refs.pop(0)) if rope else (None, None)
    o_ref = refs[-1]
    w = w_ref[...].astype(BF16)
    sub = x_ref.shape[0] // subtiles
    for r in range(subtiles):
        r0 = r * sub
        acc = _dot(x_ref[r0:r0 + sub, :], w)
        for h in range(acc.shape[1] // HEAD_W):
            seg = acc[:, h * HEAD_W:(h + 1) * HEAD_W]
            if rope:
                seg = _rope_lanes(seg, cos_ref[r0:r0 + sub, :], sin_ref[r0:r0 + sub, :])
            if heads is None:
                o_ref[r0:r0 + sub, h * HEAD_W:(h + 1) * HEAD_W] = seg.astype(o_ref.dtype)
            elif heads[0] == 1:
                o_ref[0, h, r0:r0 + sub, :] = seg
            else:
                tt = heads[1]
                for b in range(sub // tt):
                    o_ref[r0 // tt + b, h] = seg[b * tt:(b + 1) * tt, :]


def _section_proj(xn, w_in, lidx, *, col0, n_cols, row0, nb, t, tm_cap, out_dtype=F32, rope=None, buf=None,
                  name="in_proj"):
    d = xn.shape[1]
    rows = nb * t
    tm = _tile(math.gcd(rows, row0) if row0 else rows, tm_cap, V7X_BF16_SUBLANES)
    tn = 4 * HEAD_W
    assert col0 % tn == 0 and n_cols % tn == 0
    rb0, cb0 = row0 // tm, col0 // tn
    subtiles = max(s for s in (1, 2, 4) if tm % (s * MIN_SUBTILE_ROWS) == 0 or s == 1)
    in_specs = [pl.BlockSpec((tm, d), lambda i, j, l: (rb0 + i, 0)),
                pl.BlockSpec((None, d, tn), lambda i, j, l: (l[0], 0, cb0 + j))]
    args = [xn, w_in]
    if rope is not None:
        in_specs += [pl.BlockSpec((tm, HEAD_W), lambda i, j, l: (rb0 + i, 0))] * 2
        args += list(rope)
    aliases, heads = {}, None
    if buf is None:
        out_spec = pl.BlockSpec((tm, tn), lambda i, j, l: (i, j))
        out_shape = jax.ShapeDtypeStruct((rows, n_cols), out_dtype)
    else:
        bpt, tt = (1, tm) if tm <= t else (tm // t, t)
        assert (t % tm == 0) if bpt == 1 else (tm % t == 0 and (tm // subtiles) % t == 0)
        nt = t // tt
        heads = (bpt, tt)
        in_specs.append(pl.BlockSpec(memory_space=pl.ANY))
        args.append(buf)
        aliases = {len(args): 0}
        out_spec = pl.BlockSpec((None, bpt, tn // HEAD_W, tt, HEAD_W), lambda i, j, l: (l[0], i // nt, j, i % nt, 0))
        out_shape = jax.ShapeDtypeStruct(buf.shape, F32)
    blk = (_nbytes((tm, d), BF16) + _nbytes((d, tn), w_in.dtype) + 2 * _nbytes((tm, HEAD_W), F32)
           + 2 * _nbytes((tm, tn), F32))
    return pl.pallas_call(
        functools.partial(_section_kernel, rope=rope is not None, heads=heads, subtiles=subtiles),
        grid_spec=pltpu.PrefetchScalarGridSpec(
            num_scalar_prefetch=1, grid=(rows // tm, n_cols // tn), in_specs=in_specs, out_specs=out_spec),
        out_shape=out_shape, input_output_aliases=aliases,
        compiler_params=_params(blk), name=name,
    )(lidx, *args)


def _in_proj_heads(xn, w_in, cos, sin, bufs, lidx, *, row0, nb, t):
    common = dict(row0=row0, nb=nb, t=t, tm_cap=1024, n_cols=IN_SECTION_W)
    sec = lambda k: k * IN_SECTION_W
    qa = _section_proj(xn, w_in, lidx, col0=sec(0), out_dtype=BF16, rope=(cos, sin), **common)
    dk = _section_proj(xn, w_in, lidx, col0=sec(1), rope=(cos, sin), buf=bufs[0], **common)
    dv = _section_proj(xn, w_in, lidx, col0=sec(2), buf=bufs[1], **common)
    qb = _section_proj(xn, w_in, lidx, col0=sec(3), out_dtype=BF16, **common)
    sk = _section_proj(xn, w_in, lidx, col0=sec(4), buf=bufs[2], **common)
    sv = _section_proj(xn, w_in, lidx, col0=sec(5), buf=bufs[3], **common)
    return qa, qb, dk, dv, sk, sv


def _post_mla_kernel(l_ref, cq_ref, ckv_ref, gq_ref, gkv_ref, lat_in, cqn_o, ckvb_o, lat_o):
    del l_ref, lat_in
    cqn_o[...] = _rms(cq_ref[...], gq_ref[...]).astype(BF16)
    c = _rms(ckv_ref[...], gkv_ref[...])
    lat_o[...] = c
    ckvb_o[...] = c.astype(BF16)


def _post_mla(proj, gq, gkv, lat_buf, lidx):
    m = proj.shape[0]
    tt = _tile(m, 640, V7X_BF16_SUBLANES)
    blk = 3 * _nbytes((tt, MLA_Q_LORA + MLA_KV_LORA), F32)
    return pl.pallas_call(
        _post_mla_kernel,
        grid_spec=pltpu.PrefetchScalarGridSpec(
            num_scalar_prefetch=1, grid=(m // tt,),
            in_specs=[pl.BlockSpec((tt, MLA_Q_LORA), lambda i, l: (i, 0)),
                      pl.BlockSpec((tt, MLA_KV_LORA), lambda i, l: (i, MLA_Q_LORA // MLA_KV_LORA)),
                      pl.BlockSpec((None, 1, MLA_Q_LORA), lambda i, l: (l[0], 0, 0)),
                      pl.BlockSpec((None, 1, MLA_KV_LORA), lambda i, l: (l[0], 0, 0)),
                      pl.BlockSpec(memory_space=pl.ANY)],
            out_specs=[pl.BlockSpec((tt, MLA_Q_LORA), lambda i, l: (i, 0)),
                       pl.BlockSpec((tt, MLA_KV_LORA), lambda i, l: (i, 0)),
                       pl.BlockSpec((None, tt, MLA_KV_LORA), lambda i, l: (l[0], i, 0))]),
        out_shape=[jax.ShapeDtypeStruct((m, MLA_Q_LORA), BF16),
                   jax.ShapeDtypeStruct((m, MLA_KV_LORA), BF16),
                   jax.ShapeDtypeStruct(lat_buf.shape, F32)],
        input_output_aliases={5: 2},
        compiler_params=_params(blk), name="post_mla",
    )(lidx, proj, proj, gq, gkv, lat_buf)


LOG2E = math.log2(math.e)
SB_DEAD_LOG = -104.0


def _with_ones(v):
    return jnp.concatenate([v, jnp.ones(v.shape, v.dtype)], axis=1)


def _softmax_step(s2, v_ext, m_ref, acc_ref):
    m_prev = m_ref[...]
    m_new = jnp.maximum(m_prev, jnp.max(s2, axis=-1, keepdims=True))
    alpha = jnp.exp2(m_prev - m_new)
    lanes = m_prev.shape[1]
    p = jnp.concatenate([jnp.exp2(s2[:, c * lanes:(c + 1) * lanes] - m_new).astype(BF16)
                         for c in range(s2.shape[1] // lanes)], axis=1)
    acc_ref[...] = jnp.concatenate([alpha, alpha], axis=1) * acc_ref[...] + _dot(p, v_ext)
    m_ref[...] = m_new


def _softmax_init(m_ref, acc_ref):
    m_ref[...] = jnp.full(m_ref.shape, NEG_INF, F32)
    acc_ref[...] = jnp.zeros(acc_ref.shape, F32)


def _softmax_result(acc_ref):
    acc = acc_ref[...]
    return acc[:, :HEAD_W] / acc[:, HEAD_W:]


def _positions(shape, q0, k0):
    qpos = q0 + lax.broadcasted_iota(jnp.int32, shape, 0)
    kpos = k0 + lax.broadcasted_iota(jnp.int32, shape, 1)
    return qpos, kpos


def _chunk_mask(shape, q0, k0):
    qpos, kpos = _positions(shape, q0, k0)
    return (kpos // CHUNK) <= (qpos // CHUNK)


def _diff_lambda(lam_ref, li_ref):
    lv = lam_ref[...]
    lam_init = li_ref[:, 0:1]
    d1 = jnp.sum(lv[0:1, :] * lv[1:2, :], axis=-1, keepdims=True)
    d2 = jnp.sum(lv[2:3, :] * lv[3:4, :], axis=-1, keepdims=True)
    return jnp.exp(d1) - jnp.exp(d2) + lam_init, lam_init


def _split_components(q):
    lane = lax.broadcasted_iota(jnp.int32, q.shape, 1)
    zero = jnp.zeros_like(q)
    return jnp.where(lane < DIFF_DH, q, zero), jnp.where(lane >= DIFF_DH, q, zero)


def _softplus(z):
    return jnp.maximum(z, 0.0) + jnp.log1p(jnp.exp(-jnp.abs(z)))


def _upper_ones(n):
    r = lax.broadcasted_iota(jnp.int32, (n, n), 0)
    c = lax.broadcasted_iota(jnp.int32, (n, n), 1)
    return jnp.where(r > c, 1.0, 0.0).astype(BF16)


def _sb_block(z, mask, carry, v, tri):
    sp = _softplus(z)
    log_stay = -sp if mask is None else jnp.where(mask, -sp, 0.0)
    hi = log_stay.astype(BF16)
    lo = (log_stay - hi.astype(F32)).astype(BF16)
    between = _dot(hi, tri) + _dot(lo, tri) + carry
    a = jnp.exp(z - sp + between)
    if mask is not None:
        a = jnp.where(mask, a, 0.0)
    return _dot(a.astype(BF16), v), carry + jnp.sum(log_stay, axis=-1, keepdims=True)


def _attn_tiles(t):
    tq = _tile(t, 512, CHUNK)
    return tq, tq


def _diff_prompt_kernel(l_ref, q_ref, k_ref, v_ref, lam_ref, g_ref, li_ref, bias_ref, o_ref, m_ref, acc_ref, s_ref,
                        *, tq, tk, hp):
    del l_ref
    assert tq == tk
    qi = pl.program_id(2)
    scale2 = DIFF_DH ** -0.5 * LOG2E
    for c in range(2 * hp):
        _softmax_init(m_ref.at[c], acc_ref.at[c])

    def rows(j):
        return pl.ds(pl.multiple_of(j * tk, tk), tk)

    def store_logits(j, hh):
        kb = k_ref[hh, rows(j), :].astype(BF16)
        q1, q2 = _split_components(q_ref[:, hh * HEAD_W:(hh + 1) * HEAD_W])
        s_ref[2 * hh] = _dot_nt(q1, kb) * scale2
        s_ref[2 * hh + 1] = _dot_nt(q2, kb) * scale2

    def step(j, hh, bias):
        v_ext = _with_ones(v_ref[hh, rows(j), :].astype(BF16))
        for c in (2 * hh, 2 * hh + 1):
            s2 = s_ref[c] if bias is None else s_ref[c] + bias
            _softmax_step(s2, v_ext, m_ref.at[c], acc_ref.at[c])

    for hh in range(hp):
        store_logits(0, hh)

    def body(j, carry):
        for hh in range(hp):
            step(j, hh, None)
            store_logits(j + 1, hh)
        return carry

    lax.fori_loop(0, qi, body, 0)
    lam, lam_init = _diff_lambda(lam_ref, li_ref)
    for hh in range(hp):
        step(qi, hh, bias_ref[...])
        o = _softmax_result(acc_ref.at[2 * hh]) - lam * _softmax_result(acc_ref.at[2 * hh + 1])
        o_ref[:, hh * HEAD_W:(hh + 1) * HEAD_W] = (_rms(o, g_ref[...]) * (1.0 - lam_init)).astype(o_ref.dtype)


def _diag_bias(tq, tk):
    r = jnp.arange(tq, dtype=jnp.int32)[:, None] // CHUNK
    c = jnp.arange(tk, dtype=jnp.int32)[None, :] // CHUNK
    return jnp.where(c <= r, 0.0, NEG_INF).astype(F32)


def _diff_prompt(q, k_buf, v_buf, lam, g, li, lidx, *, nb, t, rows_total):
    tq, tk = _attn_tiles(t)
    nq = t // tq
    hp = 2
    kv_spec = pl.BlockSpec((None, None, hp, t, HEAD_W), lambda b, h, i, l: (l[0], b, h, 0, 0))
    blk = 2 * hp * _nbytes((t, HEAD_W), F32) + (1 + 10 * hp) * _nbytes((tq, tk), F32)
    return pl.pallas_call(
        functools.partial(_diff_prompt_kernel, tq=tq, tk=tk, hp=hp),
        grid_spec=pltpu.PrefetchScalarGridSpec(
            num_scalar_prefetch=1, grid=(nb, DIFF_HEADS // hp, nq),
            in_specs=[pl.BlockSpec((tq, hp * HEAD_W), lambda b, h, i, l: (b * nq + i, h)),
                      kv_spec, kv_spec,
                      pl.BlockSpec((None, 4, DIFF_DH), lambda b, h, i, l: (l[0], 0, 0)),
                      pl.BlockSpec((None, 1, HEAD_W), lambda b, h, i, l: (l[0], 0, 0)),
                      pl.BlockSpec((None, 1, HEAD_W), lambda b, h, i, l: (l[0], 0, 0)),
                      pl.BlockSpec((tq, tk), lambda b, h, i, l: (0, 0))],
            out_specs=pl.BlockSpec((tq, hp * HEAD_W), lambda b, h, i, l: (b * nq + i, h)),
            scratch_shapes=[pltpu.VMEM((2 * hp, tq, HEAD_W), F32), pltpu.VMEM((2 * hp, tq, 2 * HEAD_W), F32),
                            pltpu.VMEM((2 * hp, tq, tk), F32)]),
        out_shape=jax.ShapeDtypeStruct((rows_total, DIFF_HEADS * HEAD_W), BF16),
        compiler_params=_params(blk), name="diff_prompt",
    )(lidx, q, k_buf, v_buf, lam, g, li, _diag_bias(tq, tk))


def _sb_prompt_kernel(l_ref, q_ref, k_ref, v_ref, o_ref, c_ref, acc_ref, *, tq, tk, hp):
    del l_ref
    qi = pl.program_id(2)
    ratio = tq // tk
    scale = HEAD_W ** -0.5
    tri = _upper_ones(tk)
    c_ref[...] = jnp.zeros(c_ref.shape, F32)
    acc_ref[...] = jnp.zeros(acc_ref.shape, F32)

    def block(j, masked):
        start = pl.multiple_of(j * tk, tk)
        mask = None
        if masked:
            qpos, kpos = _positions((tq, tk), qi * tq, start)
            mask = kpos < qpos
        for hh in range(hp):
            kb = k_ref[hh, pl.ds(start, tk), :].astype(BF16)
            vb = v_ref[hh, pl.ds(start, tk), :].astype(BF16)
            z = _dot_nt(q_ref[:, hh * HEAD_W:(hh + 1) * HEAD_W], kb) * scale
            out, carry = _sb_block(z, mask, c_ref[hh], vb, tri)
            acc_ref[hh] += out
            c_ref[hh] = carry

    for u in reversed(range(ratio)):
        block(qi * ratio + u, True)

    def alive():
        return jnp.max(c_ref[...]) > SB_DEAD_LOG

    def cond(state):
        j, live = state
        return jnp.logical_and(j >= 0, live)

    def body(state):
        j, _ = state
        block(j, False)
        return j - 1, alive()

    lax.while_loop(cond, body, (qi * ratio - 1, alive()))
    for hh in range(hp):
        o_ref[:, hh * HEAD_W:(hh + 1) * HEAD_W] = acc_ref[hh].astype(o_ref.dtype)


def _sb_prompt(q, k_buf, v_buf, lidx, *, nb, t, rows_total):
    tq, _ = _attn_tiles(t)
    tk = _tile(tq, 256, CHUNK)
    nq = t // tq
    hp = 2
    kv_spec = pl.BlockSpec((None, None, hp, t, HEAD_W), lambda b, h, i, l: (l[0], b, h, 0, 0))
    blk = 2 * hp * _nbytes((t, HEAD_W), F32) + 16 * hp * _nbytes((tq, tk), F32)
    return pl.pallas_call(
        functools.partial(_sb_prompt_kernel, tq=tq, tk=tk, hp=hp),
        grid_spec=pltpu.PrefetchScalarGridSpec(
            num_scalar_prefetch=1, grid=(nb, SB_HEADS // hp, nq),
            in_specs=[pl.BlockSpec((tq, hp * HEAD_W), lambda b, h, i, l: (b * nq + i, h)), kv_spec, kv_spec],
            out_specs=pl.BlockSpec((tq, hp * HEAD_W), lambda b, h, i, l: (b * nq + i, h)),
            scratch_shapes=[pltpu.VMEM((hp, tq, 1), F32), pltpu.VMEM((hp, tq, HEAD_W), F32)]),
        out_shape=jax.ShapeDtypeStruct((rows_total, SB_HEADS * HEAD_W), BF16),
        compiler_params=_params(blk), name="sb_prompt",
    )(lidx, q, k_buf, v_buf)


def _mla_prompt_kernel(q_ref, kv_ref, kr_ref, bias_ref, o_ref, m_ref, acc_ref, s_ref, *, tq, tk, hp):
    assert tq == tk
    qi = pl.program_id(2)
    scale2 = (HEAD_W + MLA_ROPE) ** -0.5 * LOG2E
    for hh in range(hp):
        _softmax_init(m_ref.at[hh], acc_ref.at[hh])

    def rows(j):
        return pl.ds(pl.multiple_of(j * tk, tk), tk)

    def logits(j, hh):
        kcat = jnp.concatenate([kv_ref[rows(j), 2 * hh * HEAD_W:(2 * hh + 1) * HEAD_W], kr_ref[rows(j), :]], axis=1)
        return _dot_nt(q_ref[:, hh * MLA_QK_PAD:(hh + 1) * MLA_QK_PAD], kcat) * scale2

    def values(j, hh):
        return _with_ones(kv_ref[rows(j), (2 * hh + 1) * HEAD_W:(2 * hh + 2) * HEAD_W])

    for hh in range(hp):
        s_ref[hh] = logits(0, hh)

    def body(j, c):
        for hh in range(hp):
            _softmax_step(s_ref[hh], values(j, hh), m_ref.at[hh], acc_ref.at[hh])
            s_ref[hh] = logits(j + 1, hh)
        return c

    lax.fori_loop(0, qi, body, 0)
    for hh in range(hp):
        _softmax_step(s_ref[hh] + bias_ref[...], values(qi, hh), m_ref.at[hh], acc_ref.at[hh])
        o_ref[:, hh * HEAD_W:(hh + 1) * HEAD_W] = _softmax_result(acc_ref.at[hh]).astype(o_ref.dtype)


def _mla_prompt(qcat, kv, kr128, *, nb, t, rows_total):
    tq, tk = _attn_tiles(t)
    nq = t // tq
    hp = 4
    blk = (2 * hp + 1) * _nbytes((t, HEAD_W), BF16) + (1 + 6 * hp) * _nbytes((tq, tk), F32)
    return pl.pallas_call(
        functools.partial(_mla_prompt_kernel, tq=tq, tk=tk, hp=hp),
        grid=(nb, MLA_HEADS // hp, nq),
        in_specs=[pl.BlockSpec((tq, hp * MLA_QK_PAD), lambda b, h, i: (b * nq + i, h)),
                  pl.BlockSpec((t, 2 * hp * HEAD_W), lambda b, h, i: (b, h)),
                  pl.BlockSpec((t, HEAD_W), lambda b, h, i: (b, 0)),
                  pl.BlockSpec((tq, tk), lambda b, h, i: (0, 0))],
        out_specs=pl.BlockSpec((tq, hp * HEAD_W), lambda b, h, i: (b * nq + i, h)),
        scratch_shapes=[pltpu.VMEM((hp, tq, HEAD_W), F32), pltpu.VMEM((hp, tq, 2 * HEAD_W), F32),
                        pltpu.VMEM((hp, tq, tk), F32)],
        out_shape=jax.ShapeDtypeStruct((rows_total, MLA_HEADS * HEAD_W), BF16),
        compiler_params=_params(blk), name="mla_prompt",
    )(qcat, kv, kr128, _diag_bias(tq, tk))


def _pad_rows(x, rows):
    return jnp.concatenate([x, jnp.zeros((rows - x.shape[0], x.shape[1]), x.dtype)], axis=0)


def _two_part_softmax(sp, sn, vp, vn):
    m = jnp.maximum(jnp.max(sp, axis=-1, keepdims=True), jnp.max(sn, axis=-1, keepdims=True))
    pp, pn = jnp.exp(sp - m), jnp.exp(sn - m)
    denom = jnp.sum(pp, axis=-1, keepdims=True) + jnp.sum(pn, axis=-1, keepdims=True)
    return (_dot(pp.astype(BF16), vp) + _dot(pn.astype(BF16), vn)) / denom


def _diff_sample_kernel(l_ref, q_ref, kp_ref, vp_ref, kn_ref, vn_ref, lam_ref, g_ref, li_ref, prev_ref, o_ref,
                        *, past, t):
    del l_ref, prev_ref
    q12 = jnp.concatenate(_split_components(q_ref[...]), axis=0)
    scale = DIFF_DH ** -0.5
    kp, vp = kp_ref[...].astype(BF16), vp_ref[...].astype(BF16)
    kn = _pad_rows(kn_ref[...], HEAD_W).astype(BF16)
    vn = _pad_rows(vn_ref[...], HEAD_W).astype(BF16)

    def chunk_mask(width, k0):
        qpos = past + lax.broadcasted_iota(jnp.int32, (2 * t, width), 0) % t
        kpos = k0 + lax.broadcasted_iota(jnp.int32, (2 * t, width), 1)
        return ((kpos // CHUNK) <= (qpos // CHUNK)) & (kpos < past + t)

    sp = jnp.where(chunk_mask(past, 0), _dot_nt(q12, kp) * scale, NEG_INF)
    sn = jnp.where(chunk_mask(HEAD_W, past), _dot_nt(q12, kn) * scale, NEG_INF)
    o12 = _two_part_softmax(sp, sn, vp, vn)
    lam, lam_init = _diff_lambda(lam_ref, li_ref)
    o = o12[:t] - lam * o12[t:]
    o_ref[...] = (_rms(o, g_ref[...]) * (1.0 - lam_init)).astype(o_ref.dtype)


def _diff_sample(q, k_cache, v_cache, k_buf, v_buf, lam, g, li, merged, lidx, *, nb, t, past, row0):
    rb0 = row0 // t
    cache_spec = pl.BlockSpec((None, None, None, past, HEAD_W), lambda b, h, l: (l[0], b, h, 0, 0))
    new_spec = pl.BlockSpec((None, None, None, t, HEAD_W), lambda b, h, l: (l[0], b, h, 0, 0))
    blk = 2 * _nbytes((past, HEAD_W), F32) + 8 * _nbytes((t, past), F32)
    return pl.pallas_call(
        functools.partial(_diff_sample_kernel, past=past, t=t),
        grid_spec=pltpu.PrefetchScalarGridSpec(
            num_scalar_prefetch=1, grid=(nb, DIFF_HEADS),
            in_specs=[pl.BlockSpec((t, HEAD_W), lambda b, h, l: (b, h)),
                      cache_spec, cache_spec, new_spec, new_spec,
                      pl.BlockSpec((None, 4, DIFF_DH), lambda b, h, l: (l[0], 0, 0)),
                      pl.BlockSpec((None, 1, HEAD_W), lambda b, h, l: (l[0], 0, 0)),
                      pl.BlockSpec((None, 1, HEAD_W), lambda b, h, l: (l[0], 0, 0)),
                      pl.BlockSpec(memory_space=pl.ANY)],
            out_specs=pl.BlockSpec((t, HEAD_W), lambda b, h, l: (rb0 + b, h))),
        out_shape=jax.ShapeDtypeStruct(merged.shape, BF16),
        input_output_aliases={9: 0},
        compiler_params=_params(blk), name="diff_sample",
    )(lidx, q, k_cache, v_cache, k_buf, v_buf, lam, g, li, merged)


def _sb_sample_kernel(l_ref, q_ref, kp_ref, vp_ref, kn_ref, vn_ref, prev_ref, o_ref, *, past, t, cw):
    del l_ref, prev_ref
    q = q_ref[...]
    scale = HEAD_W ** -0.5
    nc = past // cw
    kn = _pad_rows(kn_ref[...], cw).astype(BF16)
    vn = _pad_rows(vn_ref[...], cw).astype(BF16)
    z_past = _dot_nt(q, kp_ref[...].astype(BF16)) * scale
    z = [z_past[:, c * cw:(c + 1) * cw] for c in range(nc)] + [_dot_nt(q, kn) * scale]
    masks = []
    for c in range(nc + 1):
        qpos, kpos = _positions((t, cw), past, c * cw)
        masks.append((kpos < qpos) & (kpos < past + t))
    sp = [_softplus(zc) for zc in z]
    log_stay = jnp.concatenate([jnp.where(m, -s, 0.0) for m, s in zip(masks, sp)], axis=0)
    hi = log_stay.astype(BF16)
    lo = (log_stay - hi.astype(F32)).astype(BF16)
    cum = _dot(jnp.concatenate([hi, lo], axis=0), _upper_ones(cw))
    rows = (nc + 1) * t
    cum = cum[:rows] + cum[rows:]
    carry = jnp.zeros((t, 1), F32)
    weights = [None] * (nc + 1)
    for c in reversed(range(nc + 1)):
        logw = z[c] - sp[c] + cum[c * t:(c + 1) * t] + carry
        weights[c] = jnp.where(masks[c], jnp.exp(logw), 0.0).astype(BF16)
        carry = carry + jnp.sum(log_stay[c * t:(c + 1) * t], axis=-1, keepdims=True)
    acc = _dot(jnp.concatenate(weights[:nc], axis=1), vp_ref[...].astype(BF16)) + _dot(weights[nc], vn)
    o_ref[...] = acc.astype(o_ref.dtype)


def _sb_sample(q, k_cache, v_cache, k_buf, v_buf, merged, lidx, *, nb, t, past, row0):
    rb0 = row0 // t
    cw = _tile(past, 256, V7X_LANES)
    cache_spec = pl.BlockSpec((None, None, None, past, HEAD_W), lambda b, h, l: (l[0], b, h, 0, 0))
    new_spec = pl.BlockSpec((None, None, None, t, HEAD_W), lambda b, h, l: (l[0], b, h, 0, 0))
    blk = 2 * _nbytes((past, HEAD_W), F32) + 8 * _nbytes((t, past), F32)
    return pl.pallas_call(
        functools.partial(_sb_sample_kernel, past=past, t=t, cw=cw),
        grid_spec=pltpu.PrefetchScalarGridSpec(
            num_scalar_prefetch=1, grid=(nb, SB_HEADS),
            in_specs=[pl.BlockSpec((t, HEAD_W), lambda b, h, l: (b, h)),
                      cache_spec, cache_spec, new_spec, new_spec, pl.BlockSpec(memory_space=pl.ANY)],
            out_specs=pl.BlockSpec((t, HEAD_W), lambda b, h, l: (rb0 + b, h))),
        out_shape=jax.ShapeDtypeStruct(merged.shape, BF16),
        input_output_aliases={6: 0},
        compiler_params=_params(blk), name="sb_sample",
    )(lidx, q, k_cache, v_cache, k_buf, v_buf, merged)


def _mla_sample_kernel(l_ref, q_ref, knp_ref, krp_ref, vp_ref, knn_ref, krn_ref, vn_ref, prev_ref, o_ref,
                       *, past, t):
    del l_ref, prev_ref
    q = q_ref[...]
    scale = (HEAD_W + MLA_ROPE) ** -0.5
    q_nope, q_rope = q[:, :HEAD_W], q[:, HEAD_W:HEAD_W + MLA_ROPE]
    kn = _pad_rows(jnp.concatenate([knn_ref[...], krn_ref[...]], axis=1), HEAD_W)
    vn = _pad_rows(vn_ref[...], HEAD_W)
    qpos, kpos = _positions((t, HEAD_W), past, past)
    mask_n = ((kpos // CHUNK) <= (qpos // CHUNK)) & (kpos < past + t)
    s_past = _dot_nt(q_nope, knp_ref[...]) + _dot_nt(q_rope, krp_ref[...].astype(BF16))
    sp = jnp.where(_chunk_mask((t, past), past, 0), s_past * scale, NEG_INF)
    sn = jnp.where(mask_n, _dot_nt(q, kn) * scale, NEG_INF)
    o_ref[...] = _two_part_softmax(sp, sn, vp_ref[...], vn).astype(o_ref.dtype)


def _mla_sample(qcat, kv_past, kr_cache, kv_new, kr_new128, merged, lidx, *, nb, t, past, row0):
    rb0 = row0 // t
    blk = 3 * _nbytes((past, HEAD_W), BF16) + 8 * _nbytes((t, past), F32)
    return pl.pallas_call(
        functools.partial(_mla_sample_kernel, past=past, t=t),
        grid_spec=pltpu.PrefetchScalarGridSpec(
            num_scalar_prefetch=1, grid=(nb, MLA_HEADS),
            in_specs=[pl.BlockSpec((t, MLA_QK_PAD), lambda b, h, l: (rb0 + b, h)),
                      pl.BlockSpec((past, HEAD_W), lambda b, h, l: (b, 2 * h)),
                      pl.BlockSpec((None, None, past, MLA_ROPE), lambda b, h, l: (l[0], b, 0, 0)),
                      pl.BlockSpec((past, HEAD_W), lambda b, h, l: (b, 2 * h + 1)),
                      pl.BlockSpec((t, HEAD_W), lambda b, h, l: (rb0 + b, 2 * h)),
                      pl.BlockSpec((t, HEAD_W), lambda b, h, l: (rb0 + b, 0)),
                      pl.BlockSpec((t, HEAD_W), lambda b, h, l: (rb0 + b, 2 * h + 1)),
                      pl.BlockSpec(memory_space=pl.ANY)],
            out_specs=pl.BlockSpec((t, HEAD_W), lambda b, h, l: (rb0 + b, h))),
        out_shape=jax.ShapeDtypeStruct(merged.shape, BF16),
        input_output_aliases={8: 0},
        compiler_params=_params(blk), name="mla_sample",
    )(lidx, qcat, kv_past, kr_cache, kv_past, kv_new, kr_new128, kv_new, merged)


def _gate_merge_kernel(l_ref, xn_ref, oa_ref, ob_ref, oc_ref, wga_ref, wgb_ref, wgc_ref,
                       wa_ref, wb_ref, wc_ref, o_ref):
    del l_ref
    xn = xn_ref[...]

    def branch(wg_ref, mix_ref, w_ref):
        return jax.nn.sigmoid(_dot(xn, wg_ref[...])) * _dot(mix_ref[...], w_ref[...])

    merged = branch(wga_ref, oa_ref, wa_ref) + branch(wgb_ref, ob_ref, wb_ref) + branch(wgc_ref, oc_ref, wc_ref)
    o_ref[...] = merged.astype(o_ref.dtype)


def _gate_merge(xn, oa, ob, oc, w_gate, w_a, w_b, w_c, lidx):
    m, d = xn.shape
    tm = _tile(m, 640, V7X_BF16_SUBLANES)
    tn = _tile(d, 256, V7X_LANES)
    nj = d // tn

    def rows(width):
        return pl.BlockSpec((tm, width), lambda i, j, l: (i, 0))

    def gate(branch):
        return pl.BlockSpec((None, d, tn), lambda i, j, l: (l[0], 0, branch * nj + j))

    def proj(kdim):
        return pl.BlockSpec((None, kdim, tn), lambda i, j, l: (l[0], 0, j))

    wa, wb, wc = oa.shape[1], ob.shape[1], oc.shape[1]
    blk = (_nbytes((tm, d + wa + wb + wc), BF16) + _nbytes((3 * d + wa + wb + wc, tn), BF16)
           + 4 * _nbytes((tm, tn), F32))
    return pl.pallas_call(
        _gate_merge_kernel,
        grid_spec=pltpu.PrefetchScalarGridSpec(
            num_scalar_prefetch=1, grid=(m // tm, nj),
            in_specs=[rows(d), rows(wa), rows(wb), rows(wc), gate(0), gate(1), gate(2),
                      proj(wa), proj(wb), proj(wc)],
            out_specs=pl.BlockSpec((tm, tn), lambda i, j, l: (i, j))),
        out_shape=jax.ShapeDtypeStruct((m, d), BF16),
        compiler_params=_params(blk), name="gate_merge",
    )(lidx, xn, oa, ob, oc, w_gate, w_gate, w_gate, w_a, w_b, w_c)


def _swiglu_kernel(l_ref, x_ref, w1_ref, w3_ref, o_ref):
    del l_ref
    x = x_ref[...]
    o_ref[...] = (jax.nn.silu(_dot(x, w1_ref[...].astype(BF16)))
                  * _dot(x, w3_ref[...].astype(BF16))).astype(o_ref.dtype)


def _swiglu(xn, w1, w3, lidx):
    m, d = xn.shape
    f = w1.shape[2]
    tm = _tile(m, 1040, V7X_BF16_SUBLANES)
    tn = _tile(f, 256, V7X_LANES)
    wspec = pl.BlockSpec((None, d, tn), lambda i, j, l: (l[0], 0, j))
    blk = _nbytes((tm, d), BF16) + 3 * _nbytes((d, tn), w1.dtype) + 4 * _nbytes((tm, tn), F32)
    return pl.pallas_call(
        _swiglu_kernel,
        grid_spec=pltpu.PrefetchScalarGridSpec(
            num_scalar_prefetch=1, grid=(m // tm, f // tn),
            in_specs=[pl.BlockSpec((tm, d), lambda i, j, l: (i, 0)), wspec, wspec],
            out_specs=pl.BlockSpec((tm, tn), lambda i, j, l: (i, j))),
        out_shape=jax.ShapeDtypeStruct((m, f), BF16),
        compiler_params=_params(blk), name="swiglu",
    )(lidx, xn, w1, w3)


def _rope_tables(pos):
    inv_freq = ROPE_THETA ** (-jnp.arange(0, MLA_ROPE, 2, dtype=F32) / MLA_ROPE)
    ang = pos.astype(F32)[:, None] * inv_freq[None, :]
    c, s = jnp.cos(ang), jnp.sin(ang)
    one, zero = jnp.ones_like(c), jnp.zeros_like(c)
    cos128 = jnp.concatenate([c, c, c, c], axis=-1)
    sin128 = jnp.concatenate([-s, s, -s, s], axis=-1)
    cos256 = jnp.concatenate([one, one, one, one, c, c, one, one], axis=-1)
    sin256 = jnp.concatenate([zero, zero, zero, zero, -s, s, zero, zero], axis=-1)
    return cos128, sin128, cos256, sin256


def kernel(x_prompt, x_sample, cache_diff_k, cache_diff_v, cache_sb_k, cache_sb_v, cache_mla_latent,
           cache_mla_krope, attn_norm, w_in, diff_lambda, diff_subln, mla_q_norm, mla_w_uq, mla_kv_norm,
           mla_w_ukv, w_gate, w_branch_a, w_branch_b, w_branch_c, w_out, ffn_norm, ffn_w1, ffn_w3, ffn_w2,
           final_norm):
    nbp, tp, d = x_prompt.shape
    nbs, ts = x_sample.shape[:2]
    past = cache_diff_k.shape[3]
    depth = w_in.shape[0]
    mp, ms = nbp * tp, nbs * ts
    m = mp + ms

    h0 = jnp.concatenate([x_prompt.reshape(mp, d), x_sample.reshape(ms, d)], axis=0)
    pos = jnp.concatenate([jnp.tile(jnp.arange(tp, dtype=jnp.int32), nbp),
                           jnp.tile(past + jnp.arange(ts, dtype=jnp.int32), nbs)])
    cos128, sin128, cos256, sin256 = _rope_tables(pos)

    w_kr_pad = jnp.pad(w_in[:, :, COL_KR:], ((0, 0), (0, 0), (0, HEAD_W - MLA_ROPE))).astype(BF16)
    uq = mla_w_uq.astype(BF16).reshape(depth, MLA_Q_LORA, MLA_HEADS, HEAD_W + MLA_ROPE)
    w_uq_pad = jnp.pad(uq, ((0, 0), (0, 0), (0, 0), (0, MLA_QK_PAD - HEAD_W - MLA_ROPE))).reshape(
        depth, MLA_Q_LORA, MLA_HEADS * MLA_QK_PAD)
    w_gate_b, w_a, w_b, w_c = (_to_bf16(w) for w in (w_gate, w_branch_a, w_branch_b, w_branch_c))
    w2 = _to_bf16(ffn_w2)
    lat_cache = cache_mla_latent.reshape(depth, nbs * past, MLA_KV_LORA)

    attn_g = attn_norm.reshape(depth, 1, d)
    ffn_g = ffn_norm.reshape(depth, 1, d)
    gq = mla_q_norm.reshape(depth, 1, MLA_Q_LORA)
    gkv = mla_kv_norm.reshape(depth, 1, MLA_KV_LORA)
    subln = diff_subln.reshape(depth, 1, HEAD_W)
    lam_init = jnp.asarray([0.8 - 0.6 * math.exp(-0.3 * l) for l in range(depth)], F32)
    lam_init = jnp.broadcast_to(lam_init[:, None, None], (depth, 1, HEAD_W))

    def head_bufs(nb, t):
        return tuple(jnp.zeros((depth, nb, DIFF_HEADS, t, HEAD_W), F32) for _ in range(4))

    def layer(l, carry):
        h, p_bufs, s_bufs, lat_buf, kr_buf = carry
        lidx = jnp.reshape(l, (1,)).astype(jnp.int32)
        xn = _rmsnorm_rows(h, attn_g, lidx, BF16)
        qa_p, qb_p, *p_bufs = _in_proj_heads(xn, w_in, cos128, sin128, p_bufs, lidx, row0=0, nb=nbp, t=tp)
        qa_s, qb_s, *s_bufs = _in_proj_heads(xn, w_in, cos128, sin128, s_bufs, lidx, row0=mp, nb=nbs, t=ts)
        rest = _section_proj(xn, w_in, lidx, col0=COL_CQ, n_cols=IN_REST_W, row0=0, nb=1, t=m, tm_cap=1040,
                             name="in_proj_mla")
        kr_buf, kr128 = _shared_rope_key(xn, w_kr_pad, cos128, sin128, kr_buf, lidx)
        cqn, ckv_b, lat_buf = _post_mla(rest, gq, gkv, lat_buf, lidx)
        qcat = _matmul(cqn, w_uq_pad, lidx, out_dtype=BF16, tm_cap=1040, tn_cap=1024,
                       rope=(cos256, sin256), name="mla_q_up")
        kv = _matmul(ckv_b, mla_w_ukv, lidx, out_dtype=BF16, tm_cap=1040, tn_cap=1024, name="mla_kv_up")
        kv_past = _matmul(lat_cache, mla_w_ukv, lidx, out_dtype=BF16, tm_cap=1024, tn_cap=1024,
                          x_stacked=True, name="mla_kv_up_cache")
        oa = _diff_prompt(qa_p, p_bufs[0], p_bufs[1], diff_lambda, subln, lam_init, lidx, nb=nbp, t=tp, rows_total=m)
        oa = _diff_sample(qa_s, cache_diff_k, cache_diff_v, s_bufs[0], s_bufs[1], diff_lambda, subln,
                          lam_init, oa, lidx, nb=nbs, t=ts, past=past, row0=mp)
        ob = _sb_prompt(qb_p, p_bufs[2], p_bufs[3], lidx, nb=nbp, t=tp, rows_total=m)
        ob = _sb_sample(qb_s, cache_sb_k, cache_sb_v, s_bufs[2], s_bufs[3], ob, lidx, nb=nbs, t=ts, past=past,
                        row0=mp)
        oc = _mla_prompt(qcat, kv, kr128, nb=nbp, t=tp, rows_total=m)
        oc = _mla_sample(qcat, kv_past, cache_mla_krope, kv, kr128, oc, lidx, nb=nbs, t=ts, past=past, row0=mp)
        merged = _gate_merge(xn, oa, ob, oc, w_gate_b, w_a, w_b, w_c, lidx)
        h = _matmul(merged, w_out, lidx, out_dtype=F32, tm_cap=1040, tn_cap=512, res=h, name="out_proj")
        xn2 = _rmsnorm_rows(h, ffn_g, lidx, BF16)
        hid = _swiglu(xn2, ffn_w1, ffn_w3, lidx)
        h = _matmul(hid, w2, lidx, out_dtype=F32, tm_cap=640, tn_cap=256, res=h, name="ffn_down")
        return h, tuple(p_bufs), tuple(s_bufs), lat_buf, kr_buf

    carry = (h0, head_bufs(nbp, tp), head_bufs(nbs, ts),
             jnp.zeros((depth, m, MLA_KV_LORA), F32), jnp.zeros((depth, m, MLA_ROPE), F32))
    h, p_bufs, s_bufs, lat_buf, kr_buf = lax.fori_loop(0, depth, layer, carry)

    final_g, l0 = final_norm.reshape(1, 1, d), jnp.zeros((1,), jnp.int32)
    y_prompt = _rmsnorm_rows(h, final_g, l0, F32, row0=0, rows=mp).reshape(nbp, tp, d)
    y_sample = _rmsnorm_rows(h, final_g, l0, F32, row0=mp, rows=ms).reshape(nbs, ts, d)
    p_lat = lat_buf[:, :mp].reshape(depth, nbp, tp, MLA_KV_LORA)
    s_lat = lat_buf[:, mp:].reshape(depth, nbs, ts, MLA_KV_LORA)
    p_kr = kr_buf[:, :mp].reshape(depth, nbp, tp, MLA_ROPE)
    s_kr = kr_buf[:, mp:].reshape(depth, nbs, ts, MLA_ROPE)
    return (y_prompt, y_sample, *p_bufs, p_lat, p_kr, *s_bufs, s_lat, s_kr)
```

```python
import functools
import math

import jax
import jax.numpy as jnp
from jax import lax
from jax.experimental import pallas as pl
from jax.experimental.pallas import tpu as pltpu

D_MODEL = 4096
BATCH = 2
SEQ = 4096
DEPTH = 4
DEC_BATCH = 8
DEC_SEQ = 16
PAST_LEN = 2048

CHUNK = 64
ROPE_THETA = 10000.0
NORM_EPS = 1e-6
NEG_INF = -1e30

DIFF_HEADS = 8
DIFF_DH = 64
SB_HEADS = 8
MLA_HEADS = 16
MLA_ROPE = 64
MLA_Q_LORA = 1024
MLA_KV_LORA = 512
HEAD_W = 128
MLA_QK_PAD = 2 * HEAD_W
N_BRANCHES = 3
COL_CQ, COL_KR = 6144, 7680
FFN_HIDDEN = -(-8 * D_MODEL // 768) * 256

F32 = jnp.float32
BF16 = jnp.bfloat16

V7X_LANES = 128
V7X_BF16_SUBLANES = 16
V7X_VMEM_LIMIT_CAP = 60 * 1024 * 1024


def _tile(n, cap, mult):
    best = None
    for t in range(mult, min(n, cap) + 1, mult):
        if n % t == 0:
            best = t
    if best is None:
        raise ValueError(f"no tile for {n} (cap {cap}, multiple of {mult})")
    return best


def _params(block_bytes):
    need = 2 * block_bytes + (8 << 20)
    return pltpu.CompilerParams(vmem_limit_bytes=int(min(max(need, 32 << 20), V7X_VMEM_LIMIT_CAP)))


def _nbytes(shape, dtype):
    return math.prod(shape) * jnp.dtype(dtype).itemsize


def _dot(a, b):
    return jnp.dot(a, b, preferred_element_type=F32)


def _dot_nt(a, b):
    return lax.dot_general(a, b, (((1,), (1,)), ((), ())), preferred_element_type=F32)


def _rms(x, g):
    return x * lax.rsqrt(jnp.mean(x * x, axis=-1, keepdims=True) + NORM_EPS) * g


def _rope_lanes(x, cos, sin):
    lane = lax.broadcasted_iota(jnp.int32, x.shape, 1)
    first_half = (lane % 64) < 32
    partner = jnp.where(first_half, pltpu.roll(x, x.shape[1] - 32, 1), pltpu.roll(x, 32, 1))
    return x * cos + partner * sin


def _rmsnorm_kernel(l_ref, x_ref, g_ref, o_ref):
    del l_ref
    o_ref[...] = _rms(x_ref[...], g_ref[...]).astype(o_ref.dtype)


def _rmsnorm_rows(x, g_stack, lidx, out_dtype, row0=0, rows=None):
    d = x.shape[1]
    m = x.shape[0] if rows is None else rows
    tr = _tile(math.gcd(m, row0) if row0 else m, 320, V7X_BF16_SUBLANES)
    rb0 = row0 // tr
    blk = _nbytes((tr, d), F32) + _nbytes((tr, d), out_dtype)
    return pl.pallas_call(
        _rmsnorm_kernel,
        grid_spec=pltpu.PrefetchScalarGridSpec(
            num_scalar_prefetch=1, grid=(m // tr,),
            in_specs=[pl.BlockSpec((tr, d), lambda i, l: (rb0 + i, 0)),
                      pl.BlockSpec((None, 1, d), lambda i, l: (l[0], 0, 0))],
            out_specs=pl.BlockSpec((tr, d), lambda i, l: (i, 0))),
        out_shape=jax.ShapeDtypeStruct((m, d), out_dtype),
        compiler_params=_params(blk), name="rmsnorm_rows",
    )(lidx, x, g_stack)


def _cast_kernel(x_ref, o_ref):
    o_ref[...] = x_ref[...].astype(o_ref.dtype)


def _to_bf16(w):
    depth, kdim, n = w.shape
    rows = depth * kdim
    tr = _tile(rows, max(V7X_BF16_SUBLANES, (8 << 20) // (4 * n)), V7X_BF16_SUBLANES)
    out = pl.pallas_call(
        _cast_kernel, grid=(rows // tr,),
        in_specs=[pl.BlockSpec((tr, n), lambda i: (i, 0))],
        out_specs=pl.BlockSpec((tr, n), lambda i: (i, 0)),
        out_shape=jax.ShapeDtypeStruct((rows, n), BF16),
        compiler_params=_params(_nbytes((tr, n), F32) + _nbytes((tr, n), BF16)), name="to_bf16",
    )(w.reshape(rows, n))
    return out.reshape(depth, kdim, n)


def _mm_kernel(l_ref, x_ref, w_ref, *rest, nk, has_res, rope):
    del l_ref
    rest = list(rest)
    r_ref = rest.pop(0) if has_res else None
    cos_ref, sin_ref = (rest.pop(0), rest.pop(0)) if rope else (None, None)
    o_ref = rest.pop(0)
    part = _dot(x_ref[...].astype(BF16), w_ref[...].astype(BF16))

    def finish(acc):
        if has_res:
            acc = acc + r_ref[...]
        if rope:
            cos, sin = cos_ref[...], sin_ref[...]
            gw = cos.shape[1]
            for g in range(acc.shape[1] // gw):
                seg = acc[:, g * gw:(g + 1) * gw]
                o_ref[:, g * gw:(g + 1) * gw] = _rope_lanes(seg, cos, sin).astype(o_ref.dtype)
        else:
            o_ref[...] = acc.astype(o_ref.dtype)

    if nk == 1:
        finish(part)
    else:
        acc_ref, = rest
        k = pl.program_id(2)

        @pl.when(k == 0)
        def _():
            acc_ref[...] = part

        @pl.when(k > 0)
        def _():
            acc_ref[...] += part

        @pl.when(k == nk - 1)
        def _():
            finish(acc_ref[...])


def _matmul(x, w_stack, lidx, *, out_dtype, tm_cap, tn_cap, tk=None, res=None, rope=None,
            x_stacked=False, name="matmul"):
    m, kdim = x.shape[-2:]
    n = w_stack.shape[2]
    tm = _tile(m, tm_cap, V7X_BF16_SUBLANES)
    tn = _tile(n, tn_cap, V7X_LANES) if n % V7X_LANES == 0 else n
    tk = kdim if tk is None else tk
    nk = kdim // tk
    if x_stacked:
        x_spec = pl.BlockSpec((None, tm, tk), lambda i, j, k, l: (l[0], i, k))
    else:
        x_spec = pl.BlockSpec((tm, tk), lambda i, j, k, l: (i, k))
    in_specs = [x_spec, pl.BlockSpec((None, tk, tn), lambda i, j, k, l: (l[0], k, j))]
    args = [x, w_stack]
    blk = _nbytes((tm, tk), x.dtype) + _nbytes((tk, tn), w_stack.dtype) + _nbytes((tm, tn), F32)
    if res is not None:
        in_specs.append(pl.BlockSpec((tm, tn), lambda i, j, k, l: (i, j)))
        args.append(res)
        blk += _nbytes((tm, tn), F32)
    if rope is not None:
        gw = rope[0].shape[1]
        assert tn % gw == 0
        in_specs += [pl.BlockSpec((tm, gw), lambda i, j, k, l: (i, 0))] * 2
        args += list(rope)
        blk += 2 * _nbytes((tm, gw), F32)
    return pl.pallas_call(
        functools.partial(_mm_kernel, nk=nk, has_res=res is not None, rope=rope is not None),
        grid_spec=pltpu.PrefetchScalarGridSpec(
            num_scalar_prefetch=1, grid=(m // tm, n // tn, nk),
            in_specs=in_specs,
            out_specs=pl.BlockSpec((tm, tn), lambda i, j, k, l: (i, j)),
            scratch_shapes=[pltpu.VMEM((tm, tn), F32)] if nk > 1 else []),
        out_shape=jax.ShapeDtypeStruct((m, n), out_dtype),
        compiler_params=_params(blk), name=name,
    )(lidx, *args)


def _kr_kernel(l_ref, xn_ref, w_ref, cos_ref, sin_ref, kr_in, kr_o, kr128_o):
    del l_ref, kr_in
    r = _rope_lanes(_dot(xn_ref[...], w_ref[...]), cos_ref[...], sin_ref[...])
    kr_o[...] = r[:, :MLA_ROPE]
    kr128_o[...] = r.astype(BF16)


def _shared_rope_key(xn, w_kr_pad, cos, sin, kr_buf, lidx):
    m, d = xn.shape
    tm = _tile(m, 640, V7X_BF16_SUBLANES)
    blk = _nbytes((tm, d), BF16) + _nbytes((d, HEAD_W), BF16) + 4 * _nbytes((tm, HEAD_W), F32)
    return pl.pallas_call(
        _kr_kernel,
        grid_spec=pltpu.PrefetchScalarGridSpec(
            num_scalar_prefetch=1, grid=(m // tm,),
            in_specs=[pl.BlockSpec((tm, d), lambda i, l: (i, 0)),
                      pl.BlockSpec((None, d, HEAD_W), lambda i, l: (l[0], 0, 0)),
                      pl.BlockSpec((tm, HEAD_W), lambda i, l: (i, 0)),
                      pl.BlockSpec((tm, HEAD_W), lambda i, l: (i, 0)),
                      pl.BlockSpec(memory_space=pl.ANY)],
            out_specs=[pl.BlockSpec((None, tm, MLA_ROPE), lambda i, l: (l[0], i, 0)),
                       pl.BlockSpec((tm, HEAD_W), lambda i, l: (i, 0))]),
        out_shape=[jax.ShapeDtypeStruct(kr_buf.shape, F32), jax.ShapeDtypeStruct((m, HEAD_W), BF16)],
        input_output_aliases={5: 0},
        compiler_params=_params(blk), name="shared_rope_key",
    )(lidx, xn, w_kr_pad, cos, sin, kr_buf)


IN_SECTION_W = DIFF_HEADS * HEAD_W
IN_REST_W = COL_KR - COL_CQ
MIN_SUBTILE_ROWS = 256


def _section_kernel(l_ref, x_ref, w_ref, *refs, rope, heads, subtiles):
    del l_ref
    refs = list(refs)
    cos_ref, sin_ref = (refs.pop(0), refs.pop(0)) if rope else (None, None)
    o_ref = refs[-1]
    w = w_ref[...].astype(BF16)
    sub = x_ref.shape[0] // subtiles
    for r in range(subtiles):
        r0 = r * sub
        acc = _dot(x_ref[r0:r0 + sub, :], w)
        for h in range(acc.shape[1] // HEAD_W):
            seg = acc[:, h * HEAD_W:(h + 1) * HEAD_W]
            if rope:
                seg = _rope_lanes(seg, cos_ref[r0:r0 + sub, :], sin_ref[r0:r0 + sub, :])
            if heads is None:
                o_ref[r0:r0 + sub, h * HEAD_W:(h + 1) * HEAD_W] = seg.astype(o_ref.dtype)
            elif heads[0] == 1:
                o_ref[0, h, r0:r0 + sub, :] = seg
            else:
                tt = heads[1]
                for b in range(sub // tt):
                    o_ref[r0 // tt + b, h] = seg[b * tt:(b + 1) * tt, :]


def _section_proj(xn, w_in, lidx, *, col0, n_cols, row0, nb, t, tm_cap, out_dtype=F32, rope=None, buf=None,
                  name="in_proj"):
    d = xn.shape[1]
    rows = nb * t
    tm = _tile(math.gcd(rows, row0) if row0 else rows, tm_cap, V7X_BF16_SUBLANES)
    tn = 4 * HEAD_W
    assert col0 % tn == 0 and n_cols % tn == 0
    rb0, cb0 = row0 // tm, col0 // tn
    subtiles = max(s for s in (1, 2, 4) if tm % (s * MIN_SUBTILE_ROWS) == 0 or s == 1)
    in_specs = [pl.BlockSpec((tm, d), lambda i, j, l: (rb0 + i, 0)),
                pl.BlockSpec((None, d, tn), lambda i, j, l: (l[0], 0, cb0 + j))]
    args = [xn, w_in]
    if rope is not None:
        in_specs += [pl.BlockSpec((tm, HEAD_W), lambda i, j, l: (rb0 + i, 0))] * 2
        args += list(rope)
    aliases, heads = {}, None
    if buf is None:
        out_spec = pl.BlockSpec((tm, tn), lambda i, j, l: (i, j))
        out_shape = jax.ShapeDtypeStruct((rows, n_cols), out_dtype)
    else:
        bpt, tt = (1, tm) if tm <= t else (tm // t, t)
        assert (t % tm == 0) if bpt == 1 else (tm % t == 0 and (tm // subtiles) % t == 0)
        nt = t // tt
        heads = (bpt, tt)
        in_specs.append(pl.BlockSpec(memory_space=pl.ANY))
        args.append(buf)
        aliases = {len(args): 0}
        out_spec = pl.BlockSpec((None, bpt, tn // HEAD_W, tt, HEAD_W), lambda i, j, l: (l[0], i // nt, j, i % nt, 0))
        out_shape = jax.ShapeDtypeStruct(buf.shape, F32)
    blk = (_nbytes((tm, d), BF16) + _nbytes((d, tn), w_in.dtype) + 2 * _nbytes((tm, HEAD_W), F32)
           + 2 * _nbytes((tm, tn), F32))
    return pl.pallas_call(
        functools.partial(_section_kernel, rope=rope is not None, heads=heads, subtiles=subtiles),
        grid_spec=pltpu.PrefetchScalarGridSpec(
            num_scalar_prefetch=1, grid=(rows // tm, n_cols // tn), in_specs=in_specs, out_specs=out_spec),
        out_shape=out_shape, input_output_aliases=aliases,
        compiler_params=_params(blk), name=name,
    )(lidx, *args)


def _in_proj_heads(xn, w_in, cos, sin, bufs, lidx, *, row0, nb, t):
    common = dict(row0=row0, nb=nb, t=t, tm_cap=1024, n_cols=IN_SECTION_W)
    sec = lambda k: k * IN_SECTION_W
    qa = _section_proj(xn, w_in, lidx, col0=sec(0), out_dtype=BF16, rope=(cos, sin), **common)
    dk = _section_proj(xn, w_in, lidx, col0=sec(1), rope=(cos, sin), buf=bufs[0], **common)
    dv = _section_proj(xn, w_in, lidx, col0=sec(2), buf=bufs[1], **common)
    qb = _section_proj(xn, w_in, lidx, col0=sec(3), out_dtype=BF16, **common)
    sk = _section_proj(xn, w_in, lidx, col0=sec(4), buf=bufs[2], **common)
    sv = _section_proj(xn, w_in, lidx, col0=sec(5), buf=bufs[3], **common)
    return qa, qb, dk, dv, sk, sv


def _post_mla_kernel(l_ref, cq_ref, ckv_ref, gq_ref, gkv_ref, lat_in, cqn_o, ckvb_o, lat_o):
    del l_ref, lat_in
    cqn_o[...] = _rms(cq_ref[...], gq_ref[...]).astype(BF16)
    c = _rms(ckv_ref[...], gkv_ref[...])
    lat_o[...] = c
    ckvb_o[...] = c.astype(BF16)


def _post_mla(proj, gq, gkv, lat_buf, lidx):
    m = proj.shape[0]
    tt = _tile(m, 640, V7X_BF16_SUBLANES)
    blk = 3 * _nbytes((tt, MLA_Q_LORA + MLA_KV_LORA), F32)
    return pl.pallas_call(
        _post_mla_kernel,
        grid_spec=pltpu.PrefetchScalarGridSpec(
            num_scalar_prefetch=1, grid=(m // tt,),
            in_specs=[pl.BlockSpec((tt, MLA_Q_LORA), lambda i, l: (i, 0)),
                      pl.BlockSpec((tt, MLA_KV_LORA), lambda i, l: (i, MLA_Q_LORA // MLA_KV_LORA)),
                      pl.BlockSpec((None, 1, MLA_Q_LORA), lambda i, l: (l[0], 0, 0)),
                      pl.BlockSpec((None, 1, MLA_KV_LORA), lambda i, l: (l[0], 0, 0)),
                      pl.BlockSpec(memory_space=pl.ANY)],
            out_specs=[pl.BlockSpec((tt, MLA_Q_LORA), lambda i, l: (i, 0)),
                       pl.BlockSpec((tt, MLA_KV_LORA), lambda i, l: (i, 0)),
                       pl.BlockSpec((None, tt, MLA_KV_LORA), lambda i, l: (l[0], i, 0))]),
        out_shape=[jax.ShapeDtypeStruct((m, MLA_Q_LORA), BF16),
                   jax.ShapeDtypeStruct((m, MLA_KV_LORA), BF16),
                   jax.ShapeDtypeStruct(lat_buf.shape, F32)],
        input_output_aliases={5: 2},
        compiler_params=_params(blk), name="post_mla",
    )(lidx, proj, proj, gq, gkv, lat_buf)


LOG2E = math.log2(math.e)
SB_DEAD_LOG = -104.0


def _with_ones(v):
    return jnp.concatenate([v, jnp.ones(v.shape, v.dtype)], axis=1)


def _softmax_step(s2, v_ext, m_ref, acc_ref):
    m_prev = m_ref[...]
    m_new = jnp.maximum(m_prev, jnp.max(s2, axis=-1, keepdims=True))
    alpha = jnp.exp2(m_prev - m_new)
    lanes = m_prev.shape[1]
    p = jnp.concatenate([jnp.exp2(s2[:, c * lanes:(c + 1) * lanes] - m_new).astype(BF16)
                         for c in range(s2.shape[1] // lanes)], axis=1)
    acc_ref[...] = jnp.concatenate([alpha, alpha], axis=1) * acc_ref[...] + _dot(p, v_ext)
    m_ref[...] = m_new


def _softmax_init(m_ref, acc_ref):
    m_ref[...] = jnp.full(m_ref.shape, NEG_INF, F32)
    acc_ref[...] = jnp.zeros(acc_ref.shape, F32)


def _softmax_result(acc_ref):
    acc = acc_ref[...]
    return acc[:, :HEAD_W] / acc[:, HEAD_W:]


def _positions(shape, q0, k0):
    qpos = q0 + lax.broadcasted_iota(jnp.int32, shape, 0)
    kpos = k0 + lax.broadcasted_iota(jnp.int32, shape, 1)
    return qpos, kpos


def _chunk_mask(shape, q0, k0):
    qpos, kpos = _positions(shape, q0, k0)
    return (kpos // CHUNK) <= (qpos // CHUNK)


def _diff_lambda(lam_ref, li_ref):
    lv = lam_ref[...]
    lam_init = li_ref[:, 0:1]
    d1 = jnp.sum(lv[0:1, :] * lv[1:2, :], axis=-1, keepdims=True)
    d2 = jnp.sum(lv[2:3, :] * lv[3:4, :], axis=-1, keepdims=True)
    return jnp.exp(d1) - jnp.exp(d2) + lam_init, lam_init


def _split_components(q):
    lane = lax.broadcasted_iota(jnp.int32, q.shape, 1)
    zero = jnp.zeros_like(q)
    return jnp.where(lane < DIFF_DH, q, zero), jnp.where(lane >= DIFF_DH, q, zero)


def _softplus(z):
    return jnp.maximum(z, 0.0) + jnp.log1p(jnp.exp(-jnp.abs(z)))


def _upper_ones(n):
    r = lax.broadcasted_iota(jnp.int32, (n, n), 0)
    c = lax.broadcasted_iota(jnp.int32, (n, n), 1)
    return jnp.where(r > c, 1.0, 0.0).astype(BF16)


def _sb_block(z, mask, carry, v, tri):
    sp = _softplus(z)
    log_stay = -sp if mask is None else jnp.where(mask, -sp, 0.0)
    hi = log_stay.astype(BF16)
    lo = (log_stay - hi.astype(F32)).astype(BF16)
    between = _dot(hi, tri) + _dot(lo, tri) + carry
    a = jnp.exp(z - sp + between)
    if mask is not None:
        a = jnp.where(mask, a, 0.0)
    return _dot(a.astype(BF16), v), carry + jnp.sum(log_stay, axis=-1, keepdims=True)


def _attn_tiles(t):
    tq = _tile(t, 512, CHUNK)
    return tq, tq


def _diff_prompt_kernel(l_ref, q_ref, k_ref, v_ref, lam_ref, g_ref, li_ref, bias_ref, o_ref, m_ref, acc_ref, s_ref,
                        *, tq, tk, hp):
    del l_ref
    assert tq == tk
    qi = pl.program_id(2)
    scale2 = DIFF_DH ** -0.5 * LOG2E
    for c in range(2 * hp):
        _softmax_init(m_ref.at[c], acc_ref.at[c])

    def rows(j):
        return pl.ds(pl.multiple_of(j * tk, tk), tk)

    def store_logits(j, hh):
        kb = k_ref[hh, rows(j), :].astype(BF16)
        q1, q2 = _split_components(q_ref[:, hh * HEAD_W:(hh + 1) * HEAD_W])
        s_ref[2 * hh] = _dot_nt(q1, kb) * scale2
        s_ref[2 * hh + 1] = _dot_nt(q2, kb) * scale2

    def step(j, hh, bias):
        v_ext = _with_ones(v_ref[hh, rows(j), :].astype(BF16))
        for c in (2 * hh, 2 * hh + 1):
            s2 = s_ref[c] if bias is None else s_ref[c] + bias
            _softmax_step(s2, v_ext, m_ref.at[c], acc_ref.at[c])

    for hh in range(hp):
        store_logits(0, hh)

    def body(j, carry):
        for hh in range(hp):
            step(j, hh, None)
            store_logits(j + 1, hh)
        return carry

    lax.fori_loop(0, qi, body, 0)
    lam, lam_init = _diff_lambda(lam_ref, li_ref)
    for hh in range(hp):
        step(qi, hh, bias_ref[...])
        o = _softmax_result(acc_ref.at[2 * hh]) - lam * _softmax_result(acc_ref.at[2 * hh + 1])
        o_ref[:, hh * HEAD_W:(hh + 1) * HEAD_W] = (_rms(o, g_ref[...]) * (1.0 - lam_init)).astype(o_ref.dtype)


def _diag_bias(tq, tk):
    r = jnp.arange(tq, dtype=jnp.int32)[:, None] // CHUNK
    c = jnp.arange(tk, dtype=jnp.int32)[None, :] // CHUNK
    return jnp.where(c <= r, 0.0, NEG_INF).astype(F32)


def _diff_prompt(q, k_buf, v_buf, lam, g, li, lidx, *, nb, t, rows_total):
    tq, tk = _attn_tiles(t)
    nq = t // tq
    hp = 2
    kv_spec = pl.BlockSpec((None, None, hp, t, HEAD_W), lambda b, h, i, l: (l[0], b, h, 0, 0))
    blk = 2 * hp * _nbytes((t, HEAD_W), F32) + (1 + 10 * hp) * _nbytes((tq, tk), F32)
    return pl.pallas_call(
        functools.partial(_diff_prompt_kernel, tq=tq, tk=tk, hp=hp),
        grid_spec=pltpu.PrefetchScalarGridSpec(
            num_scalar_prefetch=1, grid=(nb, DIFF_HEADS // hp, nq),
            in_specs=[pl.BlockSpec((tq, hp * HEAD_W), lambda b, h, i, l: (b * nq + i, h)),
                      kv_spec, kv_spec,
                      pl.BlockSpec((None, 4, DIFF_DH), lambda b, h, i, l: (l[0], 0, 0)),
                      pl.BlockSpec((None, 1, HEAD_W), lambda b, h, i, l: (l[0], 0, 0)),
                      pl.BlockSpec((None, 1, HEAD_W), lambda b, h, i, l: (l[0], 0, 0)),
                      pl.BlockSpec((tq, tk), lambda b, h, i, l: (0, 0))],
            out_specs=pl.BlockSpec((tq, hp * HEAD_W), lambda b, h, i, l: (b * nq + i, h)),
            scratch_shapes=[pltpu.VMEM((2 * hp, tq, HEAD_W), F32), pltpu.VMEM((2 * hp, tq, 2 * HEAD_W), F32),
                            pltpu.VMEM((2 * hp, tq, tk), F32)]),
        out_shape=jax.ShapeDtypeStruct((rows_total, DIFF_HEADS * HEAD_W), BF16),
        compiler_params=_params(blk), name="diff_prompt",
    )(lidx, q, k_buf, v_buf, lam, g, li, _diag_bias(tq, tk))


def _sb_prompt_kernel(l_ref, q_ref, k_ref, v_ref, o_ref, c_ref, acc_ref, *, tq, tk, hp):
    del l_ref
    qi = pl.program_id(2)
    ratio = tq // tk
    scale = HEAD_W ** -0.5
    tri = _upper_ones(tk)
    c_ref[...] = jnp.zeros(c_ref.shape, F32)
    acc_ref[...] = jnp.zeros(acc_ref.shape, F32)

    def block(j, masked):
        start = pl.multiple_of(j * tk, tk)
        mask = None
        if masked:
            qpos, kpos = _positions((tq, tk), qi * tq, start)
            mask = kpos < qpos
        for hh in range(hp):
            kb = k_ref[hh, pl.ds(start, tk), :].astype(BF16)
            vb = v_ref[hh, pl.ds(start, tk), :].astype(BF16)
            z = _dot_nt(q_ref[:, hh * HEAD_W:(hh + 1) * HEAD_W], kb) * scale
            out, carry = _sb_block(z, mask, c_ref[hh], vb, tri)
            acc_ref[hh] += out
            c_ref[hh] = carry

    for u in reversed(range(ratio)):
        block(qi * ratio + u, True)

    def alive():
        return jnp.max(c_ref[...]) > SB_DEAD_LOG

    def cond(state):
        j, live = state
        return jnp.logical_and(j >= 0, live)

    def body(state):
        j, _ = state
        block(j, False)
        return j - 1, alive()

    lax.while_loop(cond, body, (qi * ratio - 1, alive()))
    for hh in range(hp):
        o_ref[:, hh * HEAD_W:(hh + 1) * HEAD_W] = acc_ref[hh].astype(o_ref.dtype)


def _sb_prompt(q, k_buf, v_buf, lidx, *, nb, t, rows_total):
    tq, _ = _attn_tiles(t)
    tk = _tile(tq, 256, CHUNK)
    nq = t // tq
    hp = 2
    kv_spec = pl.BlockSpec((None, None, hp, t, HEAD_W), lambda b, h, i, l: (l[0], b, h, 0, 0))
    blk = 2 * hp * _nbytes((t, HEAD_W), F32) + 16 * hp * _nbytes((tq, tk), F32)
    return pl.pallas_call(
        functools.partial(_sb_prompt_kernel, tq=tq, tk=tk, hp=hp),
        grid_spec=pltpu.PrefetchScalarGridSpec(
            num_scalar_prefetch=1, grid=(nb, SB_HEADS // hp, nq),
            in_specs=[pl.BlockSpec((tq, hp * HEAD_W), lambda b, h, i, l: (b * nq + i, h)), kv_spec, kv_spec],
            out_specs=pl.BlockSpec((tq, hp * HEAD_W), lambda b, h, i, l: (b * nq + i, h)),
            scratch_shapes=[pltpu.VMEM((hp, tq, 1), F32), pltpu.VMEM((hp, tq, HEAD_W), F32)]),
        out_shape=jax.ShapeDtypeStruct((rows_total, SB_HEADS * HEAD_W), BF16),
        compiler_params=_params(blk), name="sb_prompt",
    )(lidx, q, k_buf, v_buf)


def _mla_prompt_kernel(q_ref, kv_ref, kr_ref, bias_ref, o_ref, m_ref, acc_ref, s_ref, *, tq, tk, hp):
    assert tq == tk
    qi = pl.program_id(2)
    scale2 = (HEAD_W + MLA_ROPE) ** -0.5 * LOG2E
    for hh in range(hp):
        _softmax_init(m_ref.at[hh], acc_ref.at[hh])

    def rows(j):
        return pl.ds(pl.multiple_of(j * tk, tk), tk)

    def logits(j, hh):
        kcat = jnp.concatenate([kv_ref[rows(j), 2 * hh * HEAD_W:(2 * hh + 1) * HEAD_W], kr_ref[rows(j), :]], axis=1)
        return _dot_nt(q_ref[:, hh * MLA_QK_PAD:(hh + 1) * MLA_QK_PAD], kcat) * scale2

    def values(j, hh):
        return _with_ones(kv_ref[rows(j), (2 * hh + 1) * HEAD_W:(2 * hh + 2) * HEAD_W])

    for hh in range(hp):
        s_ref[hh] = logits(0, hh)

    def body(j, c):
        for hh in range(hp):
            _softmax_step(s_ref[hh], values(j, hh), m_ref.at[hh], acc_ref.at[hh])
            s_ref[hh] = logits(j + 1, hh)
        return c

    lax.fori_loop(0, qi, body, 0)
    for hh in range(hp):
        _softmax_step(s_ref[hh] + bias_ref[...], values(qi, hh), m_ref.at[hh], acc_ref.at[hh])
        o_ref[:, hh * HEAD_W:(hh + 1) * HEAD_W] = _softmax_result(acc_ref.at[hh]).astype(o_ref.dtype)


def _mla_prompt(qcat, kv, kr128, *, nb, t, rows_total):
    tq, tk = _attn_tiles(t)
    nq = t // tq
    hp = 4
    blk = (2 * hp + 1) * _nbytes((t, HEAD_W), BF16) + (1 + 6 * hp) * _nbytes((tq, tk), F32)
    return pl.pallas_call(
        functools.partial(_mla_prompt_kernel, tq=tq, tk=tk, hp=hp),
        grid=(nb, MLA_HEADS // hp, nq),
        in_specs=[pl.BlockSpec((tq, hp * MLA_QK_PAD), lambda b, h, i: (b * nq + i, h)),
                  pl.BlockSpec((t, 2 * hp * HEAD_W), lambda b, h, i: (b, h)),
                  pl.BlockSpec((t, HEAD_W), lambda b, h, i: (b, 0)),
                  pl.BlockSpec((tq, tk), lambda b, h, i: (0, 0))],
        out_specs=pl.BlockSpec((tq, hp * HEAD_W), lambda b, h, i: (b * nq + i, h)),
        scratch_shapes=[pltpu.VMEM((hp, tq, HEAD_W), F32), pltpu.VMEM((hp, tq, 2 * HEAD_W), F32),
                        pltpu.VMEM((hp, tq, tk), F32)],
        out_shape=jax.ShapeDtypeStruct((rows_total, MLA_HEADS * HEAD_W), BF16),
        compiler_params=_params(blk), name="mla_prompt",
    )(qcat, kv, kr128, _diag_bias(tq, tk))


def _pad_rows(x, rows):
    return jnp.concatenate([x, jnp.zeros((rows - x.shape[0], x.shape[1]), x.dtype)], axis=0)


def _two_part_softmax(sp, sn, vp, vn):
    m = jnp.maximum(jnp.max(sp, axis=-1, keepdims=True), jnp.max(sn, axis=-1, keepdims=True))
    pp, pn = jnp.exp(sp - m), jnp.exp(sn - m)
    denom = jnp.sum(pp, axis=-1, keepdims=True) + jnp.sum(pn, axis=-1, keepdims=True)
    return (_dot(pp.astype(BF16), vp) + _dot(pn.astype(BF16), vn)) / denom


def _diff_sample_kernel(l_ref, q_ref, kp_ref, vp_ref, kn_ref, vn_ref, lam_ref, g_ref, li_ref, prev_ref, o_ref,
                        *, past, t):
    del l_ref, prev_ref
    q12 = jnp.concatenate(_split_components(q_ref[...]), axis=0)
    scale = DIFF_DH ** -0.5
    kp, vp = kp_ref[...].astype(BF16), vp_ref[...].astype(BF16)
    kn = _pad_rows(kn_ref[...], HEAD_W).astype(BF16)
    vn = _pad_rows(vn_ref[...], HEAD_W).astype(BF16)

    def chunk_mask(width, k0):
        qpos = past + lax.broadcasted_iota(jnp.int32, (2 * t, width), 0) % t
        kpos = k0 + lax.broadcasted_iota(jnp.int32, (2 * t, width), 1)
        return ((kpos // CHUNK) <= (qpos // CHUNK)) & (kpos < past + t)

    sp = jnp.where(chunk_mask(past, 0), _dot_nt(q12, kp) * scale, NEG_INF)
    sn = jnp.where(chunk_mask(HEAD_W, past), _dot_nt(q12, kn) * scale, NEG_INF)
    o12 = _two_part_softmax(sp, sn, vp, vn)
    lam, lam_init = _diff_lambda(lam_ref, li_ref)
    o = o12[:t] - lam * o12[t:]
    o_ref[...] = (_rms(o, g_ref[...]) * (1.0 - lam_init)).astype(o_ref.dtype)


def _diff_sample(q, k_cache, v_cache, k_buf, v_buf, lam, g, li, merged, lidx, *, nb, t, past, row0):
    rb0 = row0 // t
    cache_spec = pl.BlockSpec((None, None, None, past, HEAD_W), lambda b, h, l: (l[0], b, h, 0, 0))
    new_spec = pl.BlockSpec((None, None, None, t, HEAD_W), lambda b, h, l: (l[0], b, h, 0, 0))
    blk = 2 * _nbytes((past, HEAD_W), F32) + 8 * _nbytes((t, past), F32)
    return pl.pallas_call(
        functools.partial(_diff_sample_kernel, past=past, t=t),
        grid_spec=pltpu.PrefetchScalarGridSpec(
            num_scalar_prefetch=1, grid=(nb, DIFF_HEADS),
            in_specs=[pl.BlockSpec((t, HEAD_W), lambda b, h, l: (b, h)),
                      cache_spec, cache_spec, new_spec, new_spec,
                      pl.BlockSpec((None, 4, DIFF_DH), lambda b, h, l: (l[0], 0, 0)),
                      pl.BlockSpec((None, 1, HEAD_W), lambda b, h, l: (l[0], 0, 0)),
                      pl.BlockSpec((None, 1, HEAD_W), lambda b, h, l: (l[0], 0, 0)),
                      pl.BlockSpec(memory_space=pl.ANY)],
            out_specs=pl.BlockSpec((t, HEAD_W), lambda b, h, l: (rb0 + b, h))),
        out_shape=jax.ShapeDtypeStruct(merged.shape, BF16),
        input_output_aliases={9: 0},
        compiler_params=_params(blk), name="diff_sample",
    )(lidx, q, k_cache, v_cache, k_buf, v_buf, lam, g, li, merged)


def _sb_sample_kernel(l_ref, q_ref, kp_ref, vp_ref, kn_ref, vn_ref, prev_ref, o_ref, *, past, t, cw):
    del l_ref, prev_ref
    q = q_ref[...]
    scale = HEAD_W ** -0.5
    nc = past // cw
    kn = _pad_rows(kn_ref[...], cw).astype(BF16)
    vn = _pad_rows(vn_ref[...], cw).astype(BF16)
    z_past = _dot_nt(q, kp_ref[...].astype(BF16)) * scale
    z = [z_past[:, c * cw:(c + 1) * cw] for c in range(nc)] + [_dot_nt(q, kn) * scale]
    masks = []
    for c in range(nc + 1):
        qpos, kpos = _positions((t, cw), past, c * cw)
        masks.append((kpos < qpos) & (kpos < past + t))
    sp = [_softplus(zc) for zc in z]
    log_stay = jnp.concatenate([jnp.where(m, -s, 0.0) for m, s in zip(masks, sp)], axis=0)
    hi = log_stay.astype(BF16)
    lo = (log_stay - hi.astype(F32)).astype(BF16)
    cum = _dot(jnp.concatenate([hi, lo], axis=0), _upper_ones(cw))
    rows = (nc + 1) * t
    cum = cum[:rows] + cum[rows:]
    carry = jnp.zeros((t, 1), F32)
    weights = [None] * (nc + 1)
    for c in reversed(range(nc + 1)):
        logw = z[c] - sp[c] + cum[c * t:(c + 1) * t] + carry
        weights[c] = jnp.where(masks[c], jnp.exp(logw), 0.0).astype(BF16)
        carry = carry + jnp.sum(log_stay[c * t:(c + 1) * t], axis=-1, keepdims=True)
    acc = _dot(jnp.concatenate(weights[:nc], axis=1), vp_ref[...].astype(BF16)) + _dot(weights[nc], vn)
    o_ref[...] = acc.astype(o_ref.dtype)


def _sb_sample(q, k_cache, v_cache, k_buf, v_buf, merged, lidx, *, nb, t, past, row0):
    rb0 = row0 // t
    cw = _tile(past, 256, V7X_LANES)
    cache_spec = pl.BlockSpec((None, None, None, past, HEAD_W), lambda b, h, l: (l[0], b, h, 0, 0))
    new_spec = pl.BlockSpec((None, None, None, t, HEAD_W), lambda b, h, l: (l[0], b, h, 0, 0))
    blk = 2 * _nbytes((past, HEAD_W), F32) + 8 * _nbytes((t, past), F32)
    return pl.pallas_call(
        functools.partial(_sb_sample_kernel, past=past, t=t, cw=cw),
        grid_spec=pltpu.PrefetchScalarGridSpec(
            num_scalar_prefetch=1, grid=(nb, SB_HEADS),
            in_specs=[pl.BlockSpec((t, HEAD_W), lambda b, h, l: (b, h)),
                      cache_spec, cache_spec, new_spec, new_spec, pl.BlockSpec(memory_space=pl.ANY)],
            out_specs=pl.BlockSpec((t, HEAD_W), lambda b, h, l: (rb0 + b, h))),
        out_shape=jax.ShapeDtypeStruct(merged.shape, BF16),
        input_output_aliases={6: 0},
        compiler_params=_params(blk), name="sb_sample",
    )(lidx, q, k_cache, v_cache, k_buf, v_buf, merged)


def _mla_sample_kernel(l_ref, q_ref, cp_ref, krp_ref, cn_ref, krn_ref, w_ref, prev_ref, o_ref, *, past, t):
    del l_ref, prev_ref
    scale = (HEAD_W + MLA_ROPE) ** -0.5
    q = q_ref[...]
    q_abs, q_rope = [], []
    for h in range(MLA_HEADS):
        w_uk = w_ref[:, 2 * h * HEAD_W:(2 * h + 1) * HEAD_W].astype(BF16)
        q_abs.append(_dot_nt(q[:, h * MLA_QK_PAD:h * MLA_QK_PAD + HEAD_W], w_uk).astype(BF16))
        q_rope.append(q[:, h * MLA_QK_PAD + HEAD_W:(h + 1) * MLA_QK_PAD])
    q_abs = jnp.concatenate(q_abs, axis=0)
    q_rope = jnp.concatenate(q_rope, axis=0)
    rows = MLA_HEADS * t
    c_p = cp_ref[...].astype(BF16)
    c_n = _pad_rows(cn_ref[...], HEAD_W)
    kr_n = _pad_rows(krn_ref[...], HEAD_W)

    def chunk_mask(width, k0):
        qpos = past + lax.broadcasted_iota(jnp.int32, (rows, width), 0) % t
        kpos = k0 + lax.broadcasted_iota(jnp.int32, (rows, width), 1)
        return ((kpos // CHUNK) <= (qpos // CHUNK)) & (kpos < past + t)

    s_past = _dot_nt(q_abs, c_p) + _dot_nt(q_rope[:, :MLA_ROPE], krp_ref[...].astype(BF16))
    s_new = _dot_nt(q_abs, c_n) + _dot_nt(q_rope, kr_n)
    sp = jnp.where(chunk_mask(past, 0), s_past * scale, NEG_INF)
    sn = jnp.where(chunk_mask(HEAD_W, past), s_new * scale, NEG_INF)
    o_lat = _two_part_softmax(sp, sn, c_p, c_n).astype(BF16)
    for h in range(MLA_HEADS):
        w_uv = w_ref[:, (2 * h + 1) * HEAD_W:(2 * h + 2) * HEAD_W].astype(BF16)
        o_ref[:, h * HEAD_W:(h + 1) * HEAD_W] = _dot(o_lat[h * t:(h + 1) * t, :], w_uv).astype(o_ref.dtype)


def _mla_sample(qcat, lat_cache, kr_cache, ckv_new, kr_new128, w_ukv, merged, lidx, *, nb, t, past, row0):
    rb0 = row0 // t
    kv_lora = lat_cache.shape[3]
    blk = (_nbytes((past, kv_lora), F32) + _nbytes(w_ukv.shape[1:], F32)
           + 6 * _nbytes((MLA_HEADS * t, past), F32))
    return pl.pallas_call(
        functools.partial(_mla_sample_kernel, past=past, t=t),
        grid_spec=pltpu.PrefetchScalarGridSpec(
            num_scalar_prefetch=1, grid=(nb,),
            in_specs=[pl.BlockSpec((t, MLA_HEADS * MLA_QK_PAD), lambda b, l: (rb0 + b, 0)),
                      pl.BlockSpec((None, None, past, kv_lora), lambda b, l: (l[0], b, 0, 0)),
                      pl.BlockSpec((None, None, past, MLA_ROPE), lambda b, l: (l[0], b, 0, 0)),
                      pl.BlockSpec((t, kv_lora), lambda b, l: (rb0 + b, 0)),
                      pl.BlockSpec((t, HEAD_W), lambda b, l: (rb0 + b, 0)),
                      pl.BlockSpec((None,) + w_ukv.shape[1:], lambda b, l: (l[0], 0, 0)),
                      pl.BlockSpec(memory_space=pl.ANY)],
            out_specs=pl.BlockSpec((t, MLA_HEADS * HEAD_W), lambda b, l: (rb0 + b, 0))),
        out_shape=jax.ShapeDtypeStruct(merged.shape, BF16),
        input_output_aliases={7: 0},
        compiler_params=_params(blk), name="mla_sample",
    )(lidx, qcat, lat_cache, kr_cache, ckv_new, kr_new128, w_ukv, merged)


def _gate_merge_kernel(l_ref, xn_ref, oa_ref, ob_ref, oc_ref, wga_ref, wgb_ref, wgc_ref,
                       wa_ref, wb_ref, wc_ref, o_ref):
    del l_ref
    xn = xn_ref[...]

    def branch(wg_ref, mix_ref, w_ref):
        return jax.nn.sigmoid(_dot(xn, wg_ref[...])) * _dot(mix_ref[...], w_ref[...])

    merged = branch(wga_ref, oa_ref, wa_ref) + branch(wgb_ref, ob_ref, wb_ref) + branch(wgc_ref, oc_ref, wc_ref)
    o_ref[...] = merged.astype(o_ref.dtype)


def _gate_merge(xn, oa, ob, oc, w_gate, w_a, w_b, w_c, lidx):
    m, d = xn.shape
    tm = _tile(m, 640, V7X_BF16_SUBLANES)
    tn = _tile(d, 256, V7X_LANES)
    nj = d // tn

    def rows(width):
        return pl.BlockSpec((tm, width), lambda i, j, l: (i, 0))

    def gate(branch):
        return pl.BlockSpec((None, d, tn), lambda i, j, l: (l[0], 0, branch * nj + j))

    def proj(kdim):
        return pl.BlockSpec((None, kdim, tn), lambda i, j, l: (l[0], 0, j))

    wa, wb, wc = oa.shape[1], ob.shape[1], oc.shape[1]
    blk = (_nbytes((tm, d + wa + wb + wc), BF16) + _nbytes((3 * d + wa + wb + wc, tn), BF16)
           + 4 * _nbytes((tm, tn), F32))
    return pl.pallas_call(
        _gate_merge_kernel,
        grid_spec=pltpu.PrefetchScalarGridSpec(
            num_scalar_prefetch=1, grid=(m // tm, nj),
            in_specs=[rows(d), rows(wa), rows(wb), rows(wc), gate(0), gate(1), gate(2),
                      proj(wa), proj(wb), proj(wc)],
            out_specs=pl.BlockSpec((tm, tn), lambda i, j, l: (i, j))),
        out_shape=jax.ShapeDtypeStruct((m, d), BF16),
        compiler_params=_params(blk), name="gate_merge",
    )(lidx, xn, oa, ob, oc, w_gate, w_gate, w_gate, w_a, w_b, w_c)


def _swiglu_kernel(l_ref, x_ref, w1_ref, w3_ref, o_ref):
    del l_ref
    x = x_ref[...]
    o_ref[...] = (jax.nn.silu(_dot(x, w1_ref[...].astype(BF16)))
                  * _dot(x, w3_ref[...].astype(BF16))).astype(o_ref.dtype)


def _swiglu(xn, w1, w3, lidx):
    m, d = xn.shape
    f = w1.shape[2]
    tm = _tile(m, 1040, V7X_BF16_SUBLANES)
    tn = _tile(f, 256, V7X_LANES)
    wspec = pl.BlockSpec((None, d, tn), lambda i, j, l: (l[0], 0, j))
    blk = _nbytes((tm, d), BF16) + 3 * _nbytes((d, tn), w1.dtype) + 4 * _nbytes((tm, tn), F32)
    return pl.pallas_call(
        _swiglu_kernel,
        grid_spec=pltpu.PrefetchScalarGridSpec(
            num_scalar_prefetch=1, grid=(m // tm, f // tn),
            in_specs=[pl.BlockSpec((tm, d), lambda i, j, l: (i, 0)), wspec, wspec],
            out_specs=pl.BlockSpec((tm, tn), lambda i, j, l: (i, j))),
        out_shape=jax.ShapeDtypeStruct((m, f), BF16),
        compiler_params=_params(blk), name="swiglu",
    )(lidx, xn, w1, w3)


def _rope_tables(pos):
    inv_freq = ROPE_THETA ** (-jnp.arange(0, MLA_ROPE, 2, dtype=F32) / MLA_ROPE)
    ang = pos.astype(F32)[:, None] * inv_freq[None, :]
    c, s = jnp.cos(ang), jnp.sin(ang)
    one, zero = jnp.ones_like(c), jnp.zeros_like(c)
    cos128 = jnp.concatenate([c, c, c, c], axis=-1)
    sin128 = jnp.concatenate([-s, s, -s, s], axis=-1)
    cos256 = jnp.concatenate([one, one, one, one, c, c, one, one], axis=-1)
    sin256 = jnp.concatenate([zero, zero, zero, zero, -s, s, zero, zero], axis=-1)
    return cos128, sin128, cos256, sin256


def kernel(x_prompt, x_sample, cache_diff_k, cache_diff_v, cache_sb_k, cache_sb_v, cache_mla_latent,
           cache_mla_krope, attn_norm, w_in, diff_lambda, diff_subln, mla_q_norm, mla_w_uq, mla_kv_norm,
           mla_w_ukv, w_gate, w_branch_a, w_branch_b, w_branch_c, w_out, ffn_norm, ffn_w1, ffn_w3, ffn_w2,
           final_norm):
    nbp, tp, d = x_prompt.shape
    nbs, ts = x_sample.shape[:2]
    past = cache_diff_k.shape[3]
    depth = w_in.shape[0]
    mp, ms = nbp * tp, nbs * ts
    m = mp + ms

    h0 = jnp.concatenate([x_prompt.reshape(mp, d), x_sample.reshape(ms, d)], axis=0)
    pos = jnp.concatenate([jnp.tile(jnp.arange(tp, dtype=jnp.int32), nbp),
                           jnp.tile(past + jnp.arange(ts, dtype=jnp.int32), nbs)])
    cos128, sin128, cos256, sin256 = _rope_tables(pos)

    w_kr_pad = jnp.pad(w_in[:, :, COL_KR:], ((0, 0), (0, 0), (0, HEAD_W - MLA_ROPE))).astype(BF16)
    uq = mla_w_uq.astype(BF16).reshape(depth, MLA_Q_LORA, MLA_HEADS, HEAD_W + MLA_ROPE)
    w_uq_pad = jnp.pad(uq, ((0, 0), (0, 0), (0, 0), (0, MLA_QK_PAD - HEAD_W - MLA_ROPE))).reshape(
        depth, MLA_Q_LORA, MLA_HEADS * MLA_QK_PAD)
    w_gate_b, w_a, w_b, w_c = (_to_bf16(w) for w in (w_gate, w_branch_a, w_branch_b, w_branch_c))
    w2 = _to_bf16(ffn_w2)

    attn_g = attn_norm.reshape(depth, 1, d)
    ffn_g = ffn_norm.reshape(depth, 1, d)
    gq = mla_q_norm.reshape(depth, 1, MLA_Q_LORA)
    gkv = mla_kv_norm.reshape(depth, 1, MLA_KV_LORA)
    subln = diff_subln.reshape(depth, 1, HEAD_W)
    lam_init = jnp.asarray([0.8 - 0.6 * math.exp(-0.3 * l) for l in range(depth)], F32)
    lam_init = jnp.broadcast_to(lam_init[:, None, None], (depth, 1, HEAD_W))

    def head_bufs(nb, t):
        return tuple(jnp.zeros((depth, nb, DIFF_HEADS, t, HEAD_W), F32) for _ in range(4))

    def layer(l, carry):
        h, p_bufs, s_bufs, lat_buf, kr_buf = carry
        lidx = jnp.reshape(l, (1,)).astype(jnp.int32)
        xn = _rmsnorm_rows(h, attn_g, lidx, BF16)
        qa_p, qb_p, *p_bufs = _in_proj_heads(xn, w_in, cos128, sin128, p_bufs, lidx, row0=0, nb=nbp, t=tp)
        qa_s, qb_s, *s_bufs = _in_proj_heads(xn, w_in, cos128, sin128, s_bufs, lidx, row0=mp, nb=nbs, t=ts)
        rest = _section_proj(xn, w_in, lidx, col0=COL_CQ, n_cols=IN_REST_W, row0=0, nb=1, t=m, tm_cap=1040,
                             name="in_proj_mla")
        kr_buf, kr128 = _shared_rope_key(xn, w_kr_pad, cos128, sin128, kr_buf, lidx)
        cqn, ckv_b, lat_buf = _post_mla(rest, gq, gkv, lat_buf, lidx)
        qcat = _matmul(cqn, w_uq_pad, lidx, out_dtype=BF16, tm_cap=1040, tn_cap=1024,
                       rope=(cos256, sin256), name="mla_q_up")
        kv = _matmul(ckv_b, mla_w_ukv, lidx, out_dtype=BF16, tm_cap=1040, tn_cap=1024, name="mla_kv_up")
        oa = _diff_prompt(qa_p, p_bufs[0], p_bufs[1], diff_lambda, subln, lam_init, lidx, nb=nbp, t=tp, rows_total=m)
        oa = _diff_sample(qa_s, cache_diff_k, cache_diff_v, s_bufs[0], s_bufs[1], diff_lambda, subln,
                          lam_init, oa, lidx, nb=nbs, t=ts, past=past, row0=mp)
        ob = _sb_prompt(qb_p, p_bufs[2], p_bufs[3], lidx, nb=nbp, t=tp, rows_total=m)
        ob = _sb_sample(qb_s, cache_sb_k, cache_sb_v, s_bufs[2], s_bufs[3], ob, lidx, nb=nbs, t=ts, past=past,
                        row0=mp)
        oc = _mla_prompt(qcat, kv, kr128, nb=nbp, t=tp, rows_total=m)
        oc = _mla_sample(qcat, cache_mla_latent, cache_mla_krope, ckv_b, kr128, mla_w_ukv, oc, lidx, nb=nbs, t=ts,
                         past=past, row0=mp)
        merged = _gate_merge(xn, oa, ob, oc, w_gate_b, w_a, w_b, w_c, lidx)
        h = _matmul(merged, w_out, lidx, out_dtype=F32, tm_cap=1040, tn_cap=512, res=h, name="out_proj")
        xn2 = _rmsnorm_rows(h, ffn_g, lidx, BF16)
        hid = _swiglu(xn2, ffn_w1, ffn_w3, lidx)
        h = _matmul(hid, w2, lidx, out_dtype=F32, tm_cap=640, tn_cap=256, res=h, name="ffn_down")
        return h, tuple(p_bufs), tuple(s_bufs), lat_buf, kr_buf

    carry = (h0, head_bufs(nbp, tp), head_bufs(nbs, ts),
             jnp.zeros((depth, m, MLA_KV_LORA), F32), jnp.zeros((depth, m, MLA_ROPE), F32))
    h, p_bufs, s_bufs, lat_buf, kr_buf = lax.fori_loop(0, depth, layer, carry)

    final_g, l0 = final_norm.reshape(1, 1, d), jnp.zeros((1,), jnp.int32)
    y_prompt = _rmsnorm_rows(h, final_g, l0, F32, row0=0, rows=mp).reshape(nbp, tp, d)
    y_sample = _rmsnorm_rows(h, final_g, l0, F32, row0=mp, rows=ms).reshape(nbs, ts, d)
    p_lat = lat_buf[:, :mp].reshape(depth, nbp, tp, MLA_KV_LORA)
    s_lat = lat_buf[:, mp:].reshape(depth, nbs, ts, MLA_KV_LORA)
    p_kr = kr_buf[:, :mp].reshape(depth, nbp, tp, MLA_ROPE)
    s_kr = kr_buf[:, mp:].reshape(depth, nbs, ts, MLA_ROPE)
    return (y_prompt, y_sample, *p_bufs, p_lat, p_kr, *s_bufs, s_lat, s_kr)
```

```python
import functools
import math

import jax
import jax.numpy as jnp
from jax import lax
from jax.experimental import pallas as pl
from jax.experimental.pallas import tpu as pltpu

D_MODEL = 4096
BATCH = 2
SEQ = 4096
DEPTH = 4
DEC_BATCH = 8
DEC_SEQ = 16
PAST_LEN = 2048

CHUNK = 64
ROPE_THETA = 10000.0
NORM_EPS = 1e-6
NEG_INF = -1e30

DIFF_HEADS = 8
DIFF_DH = 64
SB_HEADS = 8
MLA_HEADS = 16
MLA_ROPE = 64
MLA_Q_LORA = 1024
MLA_KV_LORA = 512
HEAD_W = 128
MLA_QK_PAD = 2 * HEAD_W
N_BRANCHES = 3
COL_CQ, COL_KR = 6144, 7680
FFN_HIDDEN = -(-8 * D_MODEL // 768) * 256

F32 = jnp.float32
BF16 = jnp.bfloat16

V7X_LANES = 128
V7X_BF16_SUBLANES = 16
V7X_VMEM_LIMIT_CAP = 60 * 1024 * 1024


def _tile(n, cap, mult):
    best = None
    for t in range(mult, min(n, cap) + 1, mult):
        if n % t == 0:
            best = t
    if best is None:
        raise ValueError(f"no tile for {n} (cap {cap}, multiple of {mult})")
    return best


def _params(block_bytes):
    need = 2 * block_bytes + (8 << 20)
    return pltpu.CompilerParams(vmem_limit_bytes=int(min(max(need, 32 << 20), V7X_VMEM_LIMIT_CAP)))


def _nbytes(shape, dtype):
    return math.prod(shape) * jnp.dtype(dtype).itemsize


def _dot(a, b):
    return jnp.dot(a, b, preferred_element_type=F32)


def _dot_nt(a, b):
    return lax.dot_general(a, b, (((1,), (1,)), ((), ())), preferred_element_type=F32)


def _rms(x, g):
    return x * lax.rsqrt(jnp.mean(x * x, axis=-1, keepdims=True) + NORM_EPS) * g


def _rope_lanes(x, cos, sin):
    lane = lax.broadcasted_iota(jnp.int32, x.shape, 1)
    first_half = (lane % 64) < 32
    partner = jnp.where(first_half, pltpu.roll(x, x.shape[1] - 32, 1), pltpu.roll(x, 32, 1))
    return x * cos + partner * sin


def _rmsnorm_kernel(l_ref, x_ref, g_ref, o_ref):
    del l_ref
    o_ref[...] = _rms(x_ref[...], g_ref[...]).astype(o_ref.dtype)


def _rmsnorm_rows(x, g_stack, lidx, out_dtype, row0=0, rows=None):
    d = x.shape[1]
    m = x.shape[0] if rows is None else rows
    tr = _tile(math.gcd(m, row0) if row0 else m, 320, V7X_BF16_SUBLANES)
    rb0 = row0 // tr
    blk = _nbytes((tr, d), F32) + _nbytes((tr, d), out_dtype)
    return pl.pallas_call(
        _rmsnorm_kernel,
        grid_spec=pltpu.PrefetchScalarGridSpec(
            num_scalar_prefetch=1, grid=(m // tr,),
            in_specs=[pl.BlockSpec((tr, d), lambda i, l: (rb0 + i, 0)),
                      pl.BlockSpec((None, 1, d), lambda i, l: (l[0], 0, 0))],
            out_specs=pl.BlockSpec((tr, d), lambda i, l: (i, 0))),
        out_shape=jax.ShapeDtypeStruct((m, d), out_dtype),
        compiler_params=_params(blk), name="rmsnorm_rows",
    )(lidx, x, g_stack)


def _cast_kernel(x_ref, o_ref):
    o_ref[...] = x_ref[...].astype(o_ref.dtype)


def _to_bf16(w):
    depth, kdim, n = w.shape
    rows = depth * kdim
    tr = _tile(rows, max(V7X_BF16_SUBLANES, (8 << 20) // (4 * n)), V7X_BF16_SUBLANES)
    out = pl.pallas_call(
        _cast_kernel, grid=(rows // tr,),
        in_specs=[pl.BlockSpec((tr, n), lambda i: (i, 0))],
        out_specs=pl.BlockSpec((tr, n), lambda i: (i, 0)),
        out_shape=jax.ShapeDtypeStruct((rows, n), BF16),
        compiler_params=_params(_nbytes((tr, n), F32) + _nbytes((tr, n), BF16)), name="to_bf16",
    )(w.reshape(rows, n))
    return out.reshape(depth, kdim, n)


def _mm_kernel(l_ref, x_ref, w_ref, *rest, nk, has_res, rope):
    del l_ref
    rest = list(rest)
    r_ref = rest.pop(0) if has_res else None
    cos_ref, sin_ref = (rest.pop(0), rest.pop(0)) if rope else (None, None)
    o_ref = rest.pop(0)
    part = _dot(x_ref[...].astype(BF16), w_ref[...].astype(BF16))

    def finish(acc):
        if has_res:
            acc = acc + r_ref[...]
        if rope:
            cos, sin = cos_ref[...], sin_ref[...]
            gw = cos.shape[1]
            for g in range(acc.shape[1] // gw):
                seg = acc[:, g * gw:(g + 1) * gw]
                o_ref[:, g * gw:(g + 1) * gw] = _rope_lanes(seg, cos, sin).astype(o_ref.dtype)
        else:
            o_ref[...] = acc.astype(o_ref.dtype)

    if nk == 1:
        finish(part)
    else:
        acc_ref, = rest
        k = pl.program_id(2)

        @pl.when(k == 0)
        def _():
            acc_ref[...] = part

        @pl.when(k > 0)
        def _():
            acc_ref[...] += part

        @pl.when(k == nk - 1)
        def _():
            finish(acc_ref[...])


def _matmul(x, w_stack, lidx, *, out_dtype, tm_cap, tn_cap, tk=None, res=None, rope=None,
            x_stacked=False, name="matmul"):
    m, kdim = x.shape[-2:]
    n = w_stack.shape[2]
    tm = _tile(m, tm_cap, V7X_BF16_SUBLANES)
    tn = _tile(n, tn_cap, V7X_LANES) if n % V7X_LANES == 0 else n
    tk = kdim if tk is None else tk
    nk = kdim // tk
    if x_stacked:
        x_spec = pl.BlockSpec((None, tm, tk), lambda i, j, k, l: (l[0], i, k))
    else:
        x_spec = pl.BlockSpec((tm, tk), lambda i, j, k, l: (i, k))
    in_specs = [x_spec, pl.BlockSpec((None, tk, tn), lambda i, j, k, l: (l[0], k, j))]
    args = [x, w_stack]
    blk = _nbytes((tm, tk), x.dtype) + _nbytes((tk, tn), w_stack.dtype) + _nbytes((tm, tn), F32)
    if res is not None:
        in_specs.append(pl.BlockSpec((tm, tn), lambda i, j, k, l: (i, j)))
        args.append(res)
        blk += _nbytes((tm, tn), F32)
    if rope is not None:
        gw = rope[0].shape[1]
        assert tn % gw == 0
        in_specs += [pl.BlockSpec((tm, gw), lambda i, j, k, l: (i, 0))] * 2
        args += list(rope)
        blk += 2 * _nbytes((tm, gw), F32)
    return pl.pallas_call(
        functools.partial(_mm_kernel, nk=nk, has_res=res is not None, rope=rope is not None),
        grid_spec=pltpu.PrefetchScalarGridSpec(
            num_scalar_prefetch=1, grid=(m // tm, n // tn, nk),
            in_specs=in_specs,
            out_specs=pl.BlockSpec((tm, tn), lambda i, j, k, l: (i, j)),
            scratch_shapes=[pltpu.VMEM((tm, tn), F32)] if nk > 1 else []),
        out_shape=jax.ShapeDtypeStruct((m, n), out_dtype),
        compiler_params=_params(blk), name=name,
    )(lidx, *args)


def _kr_kernel(l_ref, xn_ref, w_ref, cos_ref, sin_ref, kr_in, kr_o, kr128_o):
    del l_ref, kr_in
    r = _rope_lanes(_dot(xn_ref[...], w_ref[...]), cos_ref[...], sin_ref[...])
    kr_o[...] = r[:, :MLA_ROPE]
    kr128_o[...] = r.astype(BF16)


def _shared_rope_key(xn, w_kr_pad, cos, sin, kr_buf, lidx):
    m, d = xn.shape
    tm = _tile(m, 640, V7X_BF16_SUBLANES)
    blk = _nbytes((tm, d), BF16) + _nbytes((d, HEAD_W), BF16) + 4 * _nbytes((tm, HEAD_W), F32)
    return pl.pallas_call(
        _kr_kernel,
        grid_spec=pltpu.PrefetchScalarGridSpec(
            num_scalar_prefetch=1, grid=(m // tm,),
            in_specs=[pl.BlockSpec((tm, d), lambda i, l: (i, 0)),
                      pl.BlockSpec((None, d, HEAD_W), lambda i, l: (l[0], 0, 0)),
                      pl.BlockSpec((tm, HEAD_W), lambda i, l: (i, 0)),
                      pl.BlockSpec((tm, HEAD_W), lambda i, l: (i, 0)),
                      pl.BlockSpec(memory_space=pl.ANY)],
            out_specs=[pl.BlockSpec((None, tm, MLA_ROPE), lambda i, l: (l[0], i, 0)),
                       pl.BlockSpec((tm, HEAD_W), lambda i, l: (i, 0))]),
        out_shape=[jax.ShapeDtypeStruct(kr_buf.shape, F32), jax.ShapeDtypeStruct((m, HEAD_W), BF16)],
        input_output_aliases={5: 0},
        compiler_params=_params(blk), name="shared_rope_key",
    )(lidx, xn, w_kr_pad, cos, sin, kr_buf)


IN_SECTION_W = DIFF_HEADS * HEAD_W
IN_REST_W = COL_KR - COL_CQ
MIN_SUBTILE_ROWS = 256


def _sections_kernel(l_ref, x_ref, *refs, kinds, subtiles):
    del l_ref
    n = len(kinds)
    w_refs, rest = refs[:n], list(refs[n:])
    cos_ref, sin_ref = (rest.pop(0), rest.pop(0)) if any(rope for rope, _ in kinds) else (None, None)
    o_refs = rest[len(rest) - n:]
    sub = x_ref.shape[0] // subtiles
    for (rope, heads), w_ref, o_ref in zip(kinds, w_refs, o_refs):
        w = w_ref[...].astype(BF16)
        for r in range(subtiles):
            r0 = r * sub
            acc = _dot(x_ref[r0:r0 + sub, :], w)
            for h in range(acc.shape[1] // HEAD_W):
                seg = acc[:, h * HEAD_W:(h + 1) * HEAD_W]
                if rope:
                    seg = _rope_lanes(seg, cos_ref[r0:r0 + sub, :], sin_ref[r0:r0 + sub, :])
                if heads is None:
                    o_ref[r0:r0 + sub, h * HEAD_W:(h + 1) * HEAD_W] = seg.astype(o_ref.dtype)
                elif heads[0] == 1:
                    o_ref[0, h, r0:r0 + sub, :] = seg
                else:
                    tt = heads[1]
                    for b in range(sub // tt):
                        o_ref[r0 // tt + b, h] = seg[b * tt:(b + 1) * tt, :]


def _section_proj(xn, w_in, lidx, sections, *, n_cols, row0, nb, t, tm_cap, tn, rope=None, name="in_proj"):
    d = xn.shape[1]
    rows = nb * t
    tm = _tile(math.gcd(rows, row0) if row0 else rows, tm_cap, V7X_BF16_SUBLANES)
    assert n_cols % tn == 0 and all(col0 % tn == 0 for col0, _, _ in sections)
    rb0 = row0 // tm
    subtiles = max(k for k in (1, 2, 4) if tm % (k * MIN_SUBTILE_ROWS) == 0 or k == 1)
    bpt, tt = (1, tm) if tm <= t else (tm // t, t)
    nt = t // tt

    def w_spec(col0):
        return pl.BlockSpec((None, d, tn), lambda i, j, l: (l[0], 0, col0 // tn + j))

    in_specs = [pl.BlockSpec((tm, d), lambda i, j, l: (rb0 + i, 0))] + [w_spec(col0) for col0, _, _ in sections]
    args = [xn] + [w_in] * len(sections)
    if any(use_rope for _, use_rope, _ in sections):
        in_specs += [pl.BlockSpec((tm, HEAD_W), lambda i, j, l: (rb0 + i, 0))] * 2
        args += list(rope)
    kinds, out_specs, out_shapes, aliases = [], [], [], {}
    for k, (_, use_rope, out) in enumerate(sections):
        if isinstance(out, jax.Array):
            assert (t % tm == 0) if bpt == 1 else (tm % t == 0 and (tm // subtiles) % t == 0)
            kinds.append((use_rope, (bpt, tt)))
            in_specs.append(pl.BlockSpec(memory_space=pl.ANY))
            args.append(out)
            aliases[len(args)] = k
            out_specs.append(pl.BlockSpec((None, bpt, tn // HEAD_W, tt, HEAD_W),
                                          lambda i, j, l: (l[0], i // nt, j, i % nt, 0)))
            out_shapes.append(jax.ShapeDtypeStruct(out.shape, F32))
        else:
            kinds.append((use_rope, None))
            out_specs.append(pl.BlockSpec((tm, tn), lambda i, j, l: (i, j)))
            out_shapes.append(jax.ShapeDtypeStruct((rows, n_cols), out))
    blk = (_nbytes((tm, d), BF16) + 2 * _nbytes((tm, HEAD_W), F32)
           + len(sections) * (_nbytes((d, tn), w_in.dtype) + 2 * _nbytes((tm, tn), F32)))
    return pl.pallas_call(
        functools.partial(_sections_kernel, kinds=tuple(kinds), subtiles=subtiles),
        grid_spec=pltpu.PrefetchScalarGridSpec(
            num_scalar_prefetch=1, grid=(rows // tm, n_cols // tn), in_specs=in_specs, out_specs=out_specs),
        out_shape=out_shapes, input_output_aliases=aliases,
        compiler_params=_params(blk), name=name,
    )(lidx, *args)


def _in_proj_heads(xn, w_in, cos, sin, bufs, lidx, *, row0, nb, t):
    common = dict(row0=row0, nb=nb, t=t, tm_cap=1024, tn=2 * HEAD_W, n_cols=IN_SECTION_W, rope=(cos, sin))
    sec = lambda k: k * IN_SECTION_W
    qa, dk, qb = _section_proj(xn, w_in, lidx, [(sec(0), True, BF16), (sec(1), True, bufs[0]),
                                                (sec(3), False, BF16)], **common)
    dv, sk, sv = _section_proj(xn, w_in, lidx, [(sec(2), False, bufs[1]), (sec(4), False, bufs[2]),
                                                (sec(5), False, bufs[3])], **common)
    return qa, qb, dk, dv, sk, sv


def _post_mla_kernel(l_ref, cq_ref, ckv_ref, gq_ref, gkv_ref, lat_in, cqn_o, ckvb_o, lat_o):
    del l_ref, lat_in
    cqn_o[...] = _rms(cq_ref[...], gq_ref[...]).astype(BF16)
    c = _rms(ckv_ref[...], gkv_ref[...])
    lat_o[...] = c
    ckvb_o[...] = c.astype(BF16)


def _post_mla(proj, gq, gkv, lat_buf, lidx):
    m = proj.shape[0]
    tt = _tile(m, 640, V7X_BF16_SUBLANES)
    blk = 3 * _nbytes((tt, MLA_Q_LORA + MLA_KV_LORA), F32)
    return pl.pallas_call(
        _post_mla_kernel,
        grid_spec=pltpu.PrefetchScalarGridSpec(
            num_scalar_prefetch=1, grid=(m // tt,),
            in_specs=[pl.BlockSpec((tt, MLA_Q_LORA), lambda i, l: (i, 0)),
                      pl.BlockSpec((tt, MLA_KV_LORA), lambda i, l: (i, MLA_Q_LORA // MLA_KV_LORA)),
                      pl.BlockSpec((None, 1, MLA_Q_LORA), lambda i, l: (l[0], 0, 0)),
                      pl.BlockSpec((None, 1, MLA_KV_LORA), lambda i, l: (l[0], 0, 0)),
                      pl.BlockSpec(memory_space=pl.ANY)],
            out_specs=[pl.BlockSpec((tt, MLA_Q_LORA), lambda i, l: (i, 0)),
                       pl.BlockSpec((tt, MLA_KV_LORA), lambda i, l: (i, 0)),
                       pl.BlockSpec((None, tt, MLA_KV_LORA), lambda i, l: (l[0], i, 0))]),
        out_shape=[jax.ShapeDtypeStruct((m, MLA_Q_LORA), BF16),
                   jax.ShapeDtypeStruct((m, MLA_KV_LORA), BF16),
                   jax.ShapeDtypeStruct(lat_buf.shape, F32)],
        input_output_aliases={5: 2},
        compiler_params=_params(blk), name="post_mla",
    )(lidx, proj, proj, gq, gkv, lat_buf)


LOG2E = math.log2(math.e)
SB_DEAD_LOG = -104.0


def _with_ones(v):
    return jnp.concatenate([v, jnp.ones(v.shape, v.dtype)], axis=1)


def _softmax_step(s2, v_ext, m_ref, acc_ref):
    m_prev = m_ref[...]
    m_new = jnp.maximum(m_prev, jnp.max(s2, axis=-1, keepdims=True))
    alpha = jnp.exp2(m_prev - m_new)
    lanes = m_prev.shape[1]
    p = jnp.concatenate([jnp.exp2(s2[:, c * lanes:(c + 1) * lanes] - m_new).astype(BF16)
                         for c in range(s2.shape[1] // lanes)], axis=1)
    acc_ref[...] = jnp.concatenate([alpha, alpha], axis=1) * acc_ref[...] + _dot(p, v_ext)
    m_ref[...] = m_new


def _softmax_init(m_ref, acc_ref):
    m_ref[...] = jnp.full(m_ref.shape, NEG_INF, F32)
    acc_ref[...] = jnp.zeros(acc_ref.shape, F32)


def _softmax_result(acc_ref):
    acc = acc_ref[...]
    return acc[:, :HEAD_W] / acc[:, HEAD_W:]


def _positions(shape, q0, k0):
    qpos = q0 + lax.broadcasted_iota(jnp.int32, shape, 0)
    kpos = k0 + lax.broadcasted_iota(jnp.int32, shape, 1)
    return qpos, kpos


def _chunk_mask(shape, q0, k0):
    qpos, kpos = _positions(shape, q0, k0)
    return (kpos // CHUNK) <= (qpos // CHUNK)


def _diff_lambda(lam_ref, li_ref):
    lv = lam_ref[...]
    lam_init = li_ref[:, 0:1]
    d1 = jnp.sum(lv[0:1, :] * lv[1:2, :], axis=-1, keepdims=True)
    d2 = jnp.sum(lv[2:3, :] * lv[3:4, :], axis=-1, keepdims=True)
    return jnp.exp(d1) - jnp.exp(d2) + lam_init, lam_init


def _split_components(q):
    lane = lax.broadcasted_iota(jnp.int32, q.shape, 1)
    zero = jnp.zeros_like(q)
    return jnp.where(lane < DIFF_DH, q, zero), jnp.where(lane >= DIFF_DH, q, zero)


def _softplus(z):
    return jnp.maximum(z, 0.0) + jnp.log1p(jnp.exp(-jnp.abs(z)))


def _upper_ones(n):
    r = lax.broadcasted_iota(jnp.int32, (n, n), 0)
    c = lax.broadcasted_iota(jnp.int32, (n, n), 1)
    return jnp.where(r > c, 1.0, 0.0).astype(BF16)


def _sb_block(z, mask, carry, v, tri):
    sp = _softplus(z)
    log_stay = -sp if mask is None else jnp.where(mask, -sp, 0.0)
    hi = log_stay.astype(BF16)
    lo = (log_stay - hi.astype(F32)).astype(BF16)
    between = _dot(hi, tri) + _dot(lo, tri) + carry
    a = jnp.exp(z - sp + between)
    if mask is not None:
        a = jnp.where(mask, a, 0.0)
    return _dot(a.astype(BF16), v), carry + jnp.sum(log_stay, axis=-1, keepdims=True)


def _attn_tiles(t):
    tq = _tile(t, 512, CHUNK)
    return tq, tq


def _diff_prompt_kernel(l_ref, q_ref, k_ref, v_ref, lam_ref, g_ref, li_ref, bias_ref, o_ref, m_ref, acc_ref, s_ref,
                        *, tq, tk, hp):
    del l_ref
    assert tq == tk
    qi = pl.program_id(2)
    scale2 = DIFF_DH ** -0.5 * LOG2E
    for c in range(2 * hp):
        _softmax_init(m_ref.at[c], acc_ref.at[c])

    def rows(j):
        return pl.ds(pl.multiple_of(j * tk, tk), tk)

    def store_logits(j, hh):
        kb = k_ref[hh, rows(j), :].astype(BF16)
        q1, q2 = _split_components(q_ref[:, hh * HEAD_W:(hh + 1) * HEAD_W])
        s_ref[2 * hh] = _dot_nt(q1, kb) * scale2
        s_ref[2 * hh + 1] = _dot_nt(q2, kb) * scale2

    def step(j, hh, bias):
        v_ext = _with_ones(v_ref[hh, rows(j), :].astype(BF16))
        for c in (2 * hh, 2 * hh + 1):
            s2 = s_ref[c] if bias is None else s_ref[c] + bias
            _softmax_step(s2, v_ext, m_ref.at[c], acc_ref.at[c])

    for hh in range(hp):
        store_logits(0, hh)

    def body(j, carry):
        for hh in range(hp):
            step(j, hh, None)
            store_logits(j + 1, hh)
        return carry

    lax.fori_loop(0, qi, body, 0)
    lam, lam_init = _diff_lambda(lam_ref, li_ref)
    for hh in range(hp):
        step(qi, hh, bias_ref[...])
        o = _softmax_result(acc_ref.at[2 * hh]) - lam * _softmax_result(acc_ref.at[2 * hh + 1])
        o_ref[:, hh * HEAD_W:(hh + 1) * HEAD_W] = (_rms(o, g_ref[...]) * (1.0 - lam_init)).astype(o_ref.dtype)


def _diag_bias(tq, tk):
    r = jnp.arange(tq, dtype=jnp.int32)[:, None] // CHUNK
    c = jnp.arange(tk, dtype=jnp.int32)[None, :] // CHUNK
    return jnp.where(c <= r, 0.0, NEG_INF).astype(F32)


def _diff_prompt(q, k_buf, v_buf, lam, g, li, lidx, *, nb, t, rows_total):
    tq, tk = _attn_tiles(t)
    nq = t // tq
    hp = 2
    kv_spec = pl.BlockSpec((None, None, hp, t, HEAD_W), lambda b, h, i, l: (l[0], b, h, 0, 0))
    blk = 2 * hp * _nbytes((t, HEAD_W), F32) + (1 + 10 * hp) * _nbytes((tq, tk), F32)
    return pl.pallas_call(
        functools.partial(_diff_prompt_kernel, tq=tq, tk=tk, hp=hp),
        grid_spec=pltpu.PrefetchScalarGridSpec(
            num_scalar_prefetch=1, grid=(nb, DIFF_HEADS // hp, nq),
            in_specs=[pl.BlockSpec((tq, hp * HEAD_W), lambda b, h, i, l: (b * nq + i, h)),
                      kv_spec, kv_spec,
                      pl.BlockSpec((None, 4, DIFF_DH), lambda b, h, i, l: (l[0], 0, 0)),
                      pl.BlockSpec((None, 1, HEAD_W), lambda b, h, i, l: (l[0], 0, 0)),
                      pl.BlockSpec((None, 1, HEAD_W), lambda b, h, i, l: (l[0], 0, 0)),
                      pl.BlockSpec((tq, tk), lambda b, h, i, l: (0, 0))],
            out_specs=pl.BlockSpec((tq, hp * HEAD_W), lambda b, h, i, l: (b * nq + i, h)),
            scratch_shapes=[pltpu.VMEM((2 * hp, tq, HEAD_W), F32), pltpu.VMEM((2 * hp, tq, 2 * HEAD_W), F32),
                            pltpu.VMEM((2 * hp, tq, tk), F32)]),
        out_shape=jax.ShapeDtypeStruct((rows_total, DIFF_HEADS * HEAD_W), BF16),
        compiler_params=_params(blk), name="diff_prompt",
    )(lidx, q, k_buf, v_buf, lam, g, li, _diag_bias(tq, tk))


def _sb_prompt_kernel(l_ref, q_ref, k_ref, v_ref, o_ref, c_ref, acc_ref, *, tq, tk, hp):
    del l_ref
    qi = pl.program_id(2)
    ratio = tq // tk
    scale = HEAD_W ** -0.5
    tri = _upper_ones(tk)
    c_ref[...] = jnp.zeros(c_ref.shape, F32)
    acc_ref[...] = jnp.zeros(acc_ref.shape, F32)

    def block(j, masked):
        start = pl.multiple_of(j * tk, tk)
        mask = None
        if masked:
            qpos, kpos = _positions((tq, tk), qi * tq, start)
            mask = kpos < qpos
        for hh in range(hp):
            kb = k_ref[hh, pl.ds(start, tk), :].astype(BF16)
            vb = v_ref[hh, pl.ds(start, tk), :].astype(BF16)
            z = _dot_nt(q_ref[:, hh * HEAD_W:(hh + 1) * HEAD_W], kb) * scale
            out, carry = _sb_block(z, mask, c_ref[hh], vb, tri)
            acc_ref[hh] += out
            c_ref[hh] = carry

    for u in reversed(range(ratio)):
        block(qi * ratio + u, True)

    def alive():
        return jnp.max(c_ref[...]) > SB_DEAD_LOG

    def cond(state):
        j, live = state
        return jnp.logical_and(j >= 0, live)

    def body(state):
        j, _ = state
        block(j, False)
        return j - 1, alive()

    lax.while_loop(cond, body, (qi * ratio - 1, alive()))
    for hh in range(hp):
        o_ref[:, hh * HEAD_W:(hh + 1) * HEAD_W] = acc_ref[hh].astype(o_ref.dtype)


def _sb_prompt(q, k_buf, v_buf, lidx, *, nb, t, rows_total):
    tq, _ = _attn_tiles(t)
    tk = _tile(tq, 256, CHUNK)
    nq = t // tq
    hp = 2
    kv_spec = pl.BlockSpec((None, None, hp, t, HEAD_W), lambda b, h, i, l: (l[0], b, h, 0, 0))
    blk = 2 * hp * _nbytes((t, HEAD_W), F32) + 16 * hp * _nbytes((tq, tk), F32)
    return pl.pallas_call(
        functools.partial(_sb_prompt_kernel, tq=tq, tk=tk, hp=hp),
        grid_spec=pltpu.PrefetchScalarGridSpec(
            num_scalar_prefetch=1, grid=(nb, SB_HEADS // hp, nq),
            in_specs=[pl.BlockSpec((tq, hp * HEAD_W), lambda b, h, i, l: (b * nq + i, h)), kv_spec, kv_spec],
            out_specs=pl.BlockSpec((tq, hp * HEAD_W), lambda b, h, i, l: (b * nq + i, h)),
            scratch_shapes=[pltpu.VMEM((hp, tq, 1), F32), pltpu.VMEM((hp, tq, HEAD_W), F32)]),
        out_shape=jax.ShapeDtypeStruct((rows_total, SB_HEADS * HEAD_W), BF16),
        compiler_params=_params(blk), name="sb_prompt",
    )(lidx, q, k_buf, v_buf)


def _mla_prompt_kernel(q_ref, kv_ref, kr_ref, bias_ref, o_ref, m_ref, acc_ref, s_ref, *, tq, tk, hp):
    assert tq == tk
    qi = pl.program_id(2)
    scale2 = (HEAD_W + MLA_ROPE) ** -0.5 * LOG2E
    for hh in range(hp):
        _softmax_init(m_ref.at[hh], acc_ref.at[hh])

    def rows(j):
        return pl.ds(pl.multiple_of(j * tk, tk), tk)

    def logits(j, hh):
        kcat = jnp.concatenate([kv_ref[rows(j), 2 * hh * HEAD_W:(2 * hh + 1) * HEAD_W], kr_ref[rows(j), :]], axis=1)
        return _dot_nt(q_ref[:, hh * MLA_QK_PAD:(hh + 1) * MLA_QK_PAD], kcat) * scale2

    def values(j, hh):
        return _with_ones(kv_ref[rows(j), (2 * hh + 1) * HEAD_W:(2 * hh + 2) * HEAD_W])

    for hh in range(hp):
        s_ref[hh] = logits(0, hh)

    def body(j, c):
        for hh in range(hp):
            _softmax_step(s_ref[hh], values(j, hh), m_ref.at[hh], acc_ref.at[hh])
            s_ref[hh] = logits(j + 1, hh)
        return c

    lax.fori_loop(0, qi, body, 0)
    for hh in range(hp):
        _softmax_step(s_ref[hh] + bias_ref[...], values(qi, hh), m_ref.at[hh], acc_ref.at[hh])
        o_ref[:, hh * HEAD_W:(hh + 1) * HEAD_W] = _softmax_result(acc_ref.at[hh]).astype(o_ref.dtype)


def _mla_prompt(qcat, kv, kr128, *, nb, t, rows_total):
    tq, tk = _attn_tiles(t)
    nq = t // tq
    hp = 4
    blk = (2 * hp + 1) * _nbytes((t, HEAD_W), BF16) + (1 + 6 * hp) * _nbytes((tq, tk), F32)
    return pl.pallas_call(
        functools.partial(_mla_prompt_kernel, tq=tq, tk=tk, hp=hp),
        grid=(nb, MLA_HEADS // hp, nq),
        in_specs=[pl.BlockSpec((tq, hp * MLA_QK_PAD), lambda b, h, i: (b * nq + i, h)),
                  pl.BlockSpec((t, 2 * hp * HEAD_W), lambda b, h, i: (b, h)),
                  pl.BlockSpec((t, HEAD_W), lambda b, h, i: (b, 0)),
                  pl.BlockSpec((tq, tk), lambda b, h, i: (0, 0))],
        out_specs=pl.BlockSpec((tq, hp * HEAD_W), lambda b, h, i: (b * nq + i, h)),
        scratch_shapes=[pltpu.VMEM((hp, tq, HEAD_W), F32), pltpu.VMEM((hp, tq, 2 * HEAD_W), F32),
                        pltpu.VMEM((hp, tq, tk), F32)],
        out_shape=jax.ShapeDtypeStruct((rows_total, MLA_HEADS * HEAD_W), BF16),
        compiler_params=_params(blk), name="mla_prompt",
    )(qcat, kv, kr128, _diag_bias(tq, tk))


def _pad_rows(x, rows):
    return jnp.concatenate([x, jnp.zeros((rows - x.shape[0], x.shape[1]), x.dtype)], axis=0)


def _two_part_softmax(sp, sn, vp, vn):
    m = jnp.maximum(jnp.max(sp, axis=-1, keepdims=True), jnp.max(sn, axis=-1, keepdims=True))
    pp, pn = jnp.exp(sp - m), jnp.exp(sn - m)
    denom = jnp.sum(pp, axis=-1, keepdims=True) + jnp.sum(pn, axis=-1, keepdims=True)
    return (_dot(pp.astype(BF16), vp) + _dot(pn.astype(BF16), vn)) / denom


def _diff_sample_kernel(l_ref, q_ref, kp_ref, vp_ref, kn_ref, vn_ref, lam_ref, g_ref, li_ref, prev_ref, o_ref,
                        *, past, t):
    del l_ref, prev_ref
    q12 = jnp.concatenate(_split_components(q_ref[...]), axis=0)
    scale = DIFF_DH ** -0.5
    kp, vp = kp_ref[...].astype(BF16), vp_ref[...].astype(BF16)
    kn = _pad_rows(kn_ref[...], HEAD_W).astype(BF16)
    vn = _pad_rows(vn_ref[...], HEAD_W).astype(BF16)

    def chunk_mask(width, k0):
        qpos = past + lax.broadcasted_iota(jnp.int32, (2 * t, width), 0) % t
        kpos = k0 + lax.broadcasted_iota(jnp.int32, (2 * t, width), 1)
        return ((kpos // CHUNK) <= (qpos // CHUNK)) & (kpos < past + t)

    sp = jnp.where(chunk_mask(past, 0), _dot_nt(q12, kp) * scale, NEG_INF)
    sn = jnp.where(chunk_mask(HEAD_W, past), _dot_nt(q12, kn) * scale, NEG_INF)
    o12 = _two_part_softmax(sp, sn, vp, vn)
    lam, lam_init = _diff_lambda(lam_ref, li_ref)
    o = o12[:t] - lam * o12[t:]
    o_ref[...] = (_rms(o, g_ref[...]) * (1.0 - lam_init)).astype(o_ref.dtype)


def _diff_sample(q, k_cache, v_cache, k_buf, v_buf, lam, g, li, merged, lidx, *, nb, t, past, row0):
    rb0 = row0 // t
    cache_spec = pl.BlockSpec((None, None, None, past, HEAD_W), lambda b, h, l: (l[0], b, h, 0, 0))
    new_spec = pl.BlockSpec((None, None, None, t, HEAD_W), lambda b, h, l: (l[0], b, h, 0, 0))
    blk = 2 * _nbytes((past, HEAD_W), F32) + 8 * _nbytes((t, past), F32)
    return pl.pallas_call(
        functools.partial(_diff_sample_kernel, past=past, t=t),
        grid_spec=pltpu.PrefetchScalarGridSpec(
            num_scalar_prefetch=1, grid=(nb, DIFF_HEADS),
            in_specs=[pl.BlockSpec((t, HEAD_W), lambda b, h, l: (b, h)),
                      cache_spec, cache_spec, new_spec, new_spec,
                      pl.BlockSpec((None, 4, DIFF_DH), lambda b, h, l: (l[0], 0, 0)),
                      pl.BlockSpec((None, 1, HEAD_W), lambda b, h, l: (l[0], 0, 0)),
                      pl.BlockSpec((None, 1, HEAD_W), lambda b, h, l: (l[0], 0, 0)),
                      pl.BlockSpec(memory_space=pl.ANY)],
            out_specs=pl.BlockSpec((t, HEAD_W), lambda b, h, l: (rb0 + b, h))),
        out_shape=jax.ShapeDtypeStruct(merged.shape, BF16),
        input_output_aliases={9: 0},
        compiler_params=_params(blk), name="diff_sample",
    )(lidx, q, k_cache, v_cache, k_buf, v_buf, lam, g, li, merged)


def _sb_sample_kernel(l_ref, q_ref, kp_ref, vp_ref, kn_ref, vn_ref, prev_ref, o_ref, *, past, t, cw):
    del l_ref, prev_ref
    q = q_ref[...]
    scale = HEAD_W ** -0.5
    nc = past // cw
    kn = _pad_rows(kn_ref[...], cw).astype(BF16)
    vn = _pad_rows(vn_ref[...], cw).astype(BF16)
    z_past = _dot_nt(q, kp_ref[...].astype(BF16)) * scale
    z = [z_past[:, c * cw:(c + 1) * cw] for c in range(nc)] + [_dot_nt(q, kn) * scale]
    masks = []
    for c in range(nc + 1):
        qpos, kpos = _positions((t, cw), past, c * cw)
        masks.append((kpos < qpos) & (kpos < past + t))
    sp = [_softplus(zc) for zc in z]
    log_stay = jnp.concatenate([jnp.where(m, -s, 0.0) for m, s in zip(masks, sp)], axis=0)
    hi = log_stay.astype(BF16)
    lo = (log_stay - hi.astype(F32)).astype(BF16)
    cum = _dot(jnp.concatenate([hi, lo], axis=0), _upper_ones(cw))
    rows = (nc + 1) * t
    cum = cum[:rows] + cum[rows:]
    carry = jnp.zeros((t, 1), F32)
    weights = [None] * (nc + 1)
    for c in reversed(range(nc + 1)):
        logw = z[c] - sp[c] + cum[c * t:(c + 1) * t] + carry
        weights[c] = jnp.where(masks[c], jnp.exp(logw), 0.0).astype(BF16)
        carry = carry + jnp.sum(log_stay[c * t:(c + 1) * t], axis=-1, keepdims=True)
    acc = _dot(jnp.concatenate(weights[:nc], axis=1), vp_ref[...].astype(BF16)) + _dot(weights[nc], vn)
    o_ref[...] = acc.astype(o_ref.dtype)


def _sb_sample(q, k_cache, v_cache, k_buf, v_buf, merged, lidx, *, nb, t, past, row0):
    rb0 = row0 // t
    cw = _tile(past, 256, V7X_LANES)
    cache_spec = pl.BlockSpec((None, None, None, past, HEAD_W), lambda b, h, l: (l[0], b, h, 0, 0))
    new_spec = pl.BlockSpec((None, None, None, t, HEAD_W), lambda b, h, l: (l[0], b, h, 0, 0))
    blk = 2 * _nbytes((past, HEAD_W), F32) + 8 * _nbytes((t, past), F32)
    return pl.pallas_call(
        functools.partial(_sb_sample_kernel, past=past, t=t, cw=cw),
        grid_spec=pltpu.PrefetchScalarGridSpec(
            num_scalar_prefetch=1, grid=(nb, SB_HEADS),
            in_specs=[pl.BlockSpec((t, HEAD_W), lambda b, h, l: (b, h)),
                      cache_spec, cache_spec, new_spec, new_spec, pl.BlockSpec(memory_space=pl.ANY)],
            out_specs=pl.BlockSpec((t, HEAD_W), lambda b, h, l: (rb0 + b, h))),
        out_shape=jax.ShapeDtypeStruct(merged.shape, BF16),
        input_output_aliases={6: 0},
        compiler_params=_params(blk), name="sb_sample",
    )(lidx, q, k_cache, v_cache, k_buf, v_buf, merged)


def _mla_sample_kernel(l_ref, q_ref, cp_ref, krp_ref, cn_ref, krn_ref, w_ref, prev_ref, o_ref, *, past, t):
    del l_ref, prev_ref
    scale = (HEAD_W + MLA_ROPE) ** -0.5
    q = q_ref[...]
    q_abs, q_rope = [], []
    for h in range(MLA_HEADS):
        w_uk = w_ref[:, 2 * h * HEAD_W:(2 * h + 1) * HEAD_W].astype(BF16)
        q_abs.append(_dot_nt(q[:, h * MLA_QK_PAD:h * MLA_QK_PAD + HEAD_W], w_uk).astype(BF16))
        q_rope.append(q[:, h * MLA_QK_PAD + HEAD_W:(h + 1) * MLA_QK_PAD])
    q_abs = jnp.concatenate(q_abs, axis=0)
    q_rope = jnp.concatenate(q_rope, axis=0)
    rows = MLA_HEADS * t
    c_p = cp_ref[...].astype(BF16)
    c_n = _pad_rows(cn_ref[...], HEAD_W)
    kr_n = _pad_rows(krn_ref[...], HEAD_W)

    def chunk_mask(width, k0):
        qpos = past + lax.broadcasted_iota(jnp.int32, (rows, width), 0) % t
        kpos = k0 + lax.broadcasted_iota(jnp.int32, (rows, width), 1)
        return ((kpos // CHUNK) <= (qpos // CHUNK)) & (kpos < past + t)

    s_past = _dot_nt(q_abs, c_p) + _dot_nt(q_rope[:, :MLA_ROPE], krp_ref[...].astype(BF16))
    s_new = _dot_nt(q_abs, c_n) + _dot_nt(q_rope, kr_n)
    sp = jnp.where(chunk_mask(past, 0), s_past * scale, NEG_INF)
    sn = jnp.where(chunk_mask(HEAD_W, past), s_new * scale, NEG_INF)
    o_lat = _two_part_softmax(sp, sn, c_p, c_n).astype(BF16)
    for h in range(MLA_HEADS):
        w_uv = w_ref[:, (2 * h + 1) * HEAD_W:(2 * h + 2) * HEAD_W].astype(BF16)
        o_ref[:, h * HEAD_W:(h + 1) * HEAD_W] = _dot(o_lat[h * t:(h + 1) * t, :], w_uv).astype(o_ref.dtype)


def _mla_sample(qcat, lat_cache, kr_cache, ckv_new, kr_new128, w_ukv, merged, lidx, *, nb, t, past, row0):
    rb0 = row0 // t
    kv_lora = lat_cache.shape[3]
    blk = (_nbytes((past, kv_lora), F32) + _nbytes(w_ukv.shape[1:], F32)
           + 6 * _nbytes((MLA_HEADS * t, past), F32))
    return pl.pallas_call(
        functools.partial(_mla_sample_kernel, past=past, t=t),
        grid_spec=pltpu.PrefetchScalarGridSpec(
            num_scalar_prefetch=1, grid=(nb,),
            in_specs=[pl.BlockSpec((t, MLA_HEADS * MLA_QK_PAD), lambda b, l: (rb0 + b, 0)),
                      pl.BlockSpec((None, None, past, kv_lora), lambda b, l: (l[0], b, 0, 0)),
                      pl.BlockSpec((None, None, past, MLA_ROPE), lambda b, l: (l[0], b, 0, 0)),
                      pl.BlockSpec((t, kv_lora), lambda b, l: (rb0 + b, 0)),
                      pl.BlockSpec((t, HEAD_W), lambda b, l: (rb0 + b, 0)),
                      pl.BlockSpec((None,) + w_ukv.shape[1:], lambda b, l: (l[0], 0, 0)),
                      pl.BlockSpec(memory_space=pl.ANY)],
            out_specs=pl.BlockSpec((t, MLA_HEADS * HEAD_W), lambda b, l: (rb0 + b, 0))),
        out_shape=jax.ShapeDtypeStruct(merged.shape, BF16),
        input_output_aliases={7: 0},
        compiler_params=_params(blk), name="mla_sample",
    )(lidx, qcat, lat_cache, kr_cache, ckv_new, kr_new128, w_ukv, merged)


def _gate_merge_kernel(l_ref, xn_ref, oa_ref, ob_ref, oc_ref, wga_ref, wgb_ref, wgc_ref,
                       wa_ref, wb_ref, wc_ref, o_ref):
    del l_ref
    xn = xn_ref[...]

    def branch(wg_ref, mix_ref, w_ref):
        return jax.nn.sigmoid(_dot(xn, wg_ref[...])) * _dot(mix_ref[...], w_ref[...])

    merged = branch(wga_ref, oa_ref, wa_ref) + branch(wgb_ref, ob_ref, wb_ref) + branch(wgc_ref, oc_ref, wc_ref)
    o_ref[...] = merged.astype(o_ref.dtype)


def _gate_merge(xn, oa, ob, oc, w_gate, w_a, w_b, w_c, lidx):
    m, d = xn.shape
    tm = _tile(m, 640, V7X_BF16_SUBLANES)
    tn = _tile(d, 256, V7X_LANES)
    nj = d // tn

    def rows(width):
        return pl.BlockSpec((tm, width), lambda i, j, l: (i, 0))

    def gate(branch):
        return pl.BlockSpec((None, d, tn), lambda i, j, l: (l[0], 0, branch * nj + j))

    def proj(kdim):
        return pl.BlockSpec((None, kdim, tn), lambda i, j, l: (l[0], 0, j))

    wa, wb, wc = oa.shape[1], ob.shape[1], oc.shape[1]
    blk = (_nbytes((tm, d + wa + wb + wc), BF16) + _nbytes((3 * d + wa + wb + wc, tn), BF16)
           + 4 * _nbytes((tm, tn), F32))
    return pl.pallas_call(
        _gate_merge_kernel,
        grid_spec=pltpu.PrefetchScalarGridSpec(
            num_scalar_prefetch=1, grid=(m // tm, nj),
            in_specs=[rows(d), rows(wa), rows(wb), rows(wc), gate(0), gate(1), gate(2),
                      proj(wa), proj(wb), proj(wc)],
            out_specs=pl.BlockSpec((tm, tn), lambda i, j, l: (i, j))),
        out_shape=jax.ShapeDtypeStruct((m, d), BF16),
        compiler_params=_params(blk), name="gate_merge",
    )(lidx, xn, oa, ob, oc, w_gate, w_gate, w_gate, w_a, w_b, w_c)


def _swiglu_kernel(l_ref, x_ref, w1_ref, w3_ref, o_ref):
    del l_ref
    x = x_ref[...]
    o_ref[...] = (jax.nn.silu(_dot(x, w1_ref[...].astype(BF16)))
                  * _dot(x, w3_ref[...].astype(BF16))).astype(o_ref.dtype)


def _swiglu(xn, w1, w3, lidx):
    m, d = xn.shape
    f = w1.shape[2]
    tm = _tile(m, 1040, V7X_BF16_SUBLANES)
    tn = _tile(f, 256, V7X_LANES)
    wspec = pl.BlockSpec((None, d, tn), lambda i, j, l: (l[0], 0, j))
    blk = _nbytes((tm, d), BF16) + 3 * _nbytes((d, tn), w1.dtype) + 4 * _nbytes((tm, tn), F32)
    return pl.pallas_call(
        _swiglu_kernel,
        grid_spec=pltpu.PrefetchScalarGridSpec(
            num_scalar_prefetch=1, grid=(m // tm, f // tn),
            in_specs=[pl.BlockSpec((tm, d), lambda i, j, l: (i, 0)), wspec, wspec],
            out_specs=pl.BlockSpec((tm, tn), lambda i, j, l: (i, j))),
        out_shape=jax.ShapeDtypeStruct((m, f), BF16),
        compiler_params=_params(blk), name="swiglu",
    )(lidx, xn, w1, w3)


def _rope_tables(pos):
    inv_freq = ROPE_THETA ** (-jnp.arange(0, MLA_ROPE, 2, dtype=F32) / MLA_ROPE)
    ang = pos.astype(F32)[:, None] * inv_freq[None, :]
    c, s = jnp.cos(ang), jnp.sin(ang)
    one, zero = jnp.ones_like(c), jnp.zeros_like(c)
    cos128 = jnp.concatenate([c, c, c, c], axis=-1)
    sin128 = jnp.concatenate([-s, s, -s, s], axis=-1)
    cos256 = jnp.concatenate([one, one, one, one, c, c, one, one], axis=-1)
    sin256 = jnp.concatenate([zero, zero, zero, zero, -s, s, zero, zero], axis=-1)
    return cos128, sin128, cos256, sin256


def kernel(x_prompt, x_sample, cache_diff_k, cache_diff_v, cache_sb_k, cache_sb_v, cache_mla_latent,
           cache_mla_krope, attn_norm, w_in, diff_lambda, diff_subln, mla_q_norm, mla_w_uq, mla_kv_norm,
           mla_w_ukv, w_gate, w_branch_a, w_branch_b, w_branch_c, w_out, ffn_norm, ffn_w1, ffn_w3, ffn_w2,
           final_norm):
    nbp, tp, d = x_prompt.shape
    nbs, ts = x_sample.shape[:2]
    past = cache_diff_k.shape[3]
    depth = w_in.shape[0]
    mp, ms = nbp * tp, nbs * ts
    m = mp + ms

    h0 = jnp.concatenate([x_prompt.reshape(mp, d), x_sample.reshape(ms, d)], axis=0)
    pos = jnp.concatenate([jnp.tile(jnp.arange(tp, dtype=jnp.int32), nbp),
                           jnp.tile(past + jnp.arange(ts, dtype=jnp.int32), nbs)])
    cos128, sin128, cos256, sin256 = _rope_tables(pos)

    w_kr_pad = jnp.pad(w_in[:, :, COL_KR:], ((0, 0), (0, 0), (0, HEAD_W - MLA_ROPE))).astype(BF16)
    uq = mla_w_uq.astype(BF16).reshape(depth, MLA_Q_LORA, MLA_HEADS, HEAD_W + MLA_ROPE)
    w_uq_pad = jnp.pad(uq, ((0, 0), (0, 0), (0, 0), (0, MLA_QK_PAD - HEAD_W - MLA_ROPE))).reshape(
        depth, MLA_Q_LORA, MLA_HEADS * MLA_QK_PAD)
    w_gate_b, w_a, w_b, w_c = (_to_bf16(w) for w in (w_gate, w_branch_a, w_branch_b, w_branch_c))
    w2 = _to_bf16(ffn_w2)
    w_in_b = _to_bf16(w_in)

    attn_g = attn_norm.reshape(depth, 1, d)
    ffn_g = ffn_norm.reshape(depth, 1, d)
    gq = mla_q_norm.reshape(depth, 1, MLA_Q_LORA)
    gkv = mla_kv_norm.reshape(depth, 1, MLA_KV_LORA)
    subln = diff_subln.reshape(depth, 1, HEAD_W)
    lam_init = jnp.asarray([0.8 - 0.6 * math.exp(-0.3 * l) for l in range(depth)], F32)
    lam_init = jnp.broadcast_to(lam_init[:, None, None], (depth, 1, HEAD_W))

    def head_bufs(nb, t):
        return tuple(jnp.zeros((depth, nb, DIFF_HEADS, t, HEAD_W), F32) for _ in range(4))

    def layer(l, carry):
        h, p_bufs, s_bufs, lat_buf, kr_buf = carry
        lidx = jnp.reshape(l, (1,)).astype(jnp.int32)
        xn = _rmsnorm_rows(h, attn_g, lidx, BF16)
        qa_p, qb_p, *p_bufs = _in_proj_heads(xn, w_in_b, cos128, sin128, p_bufs, lidx, row0=0, nb=nbp, t=tp)
        qa_s, qb_s, *s_bufs = _in_proj_heads(xn, w_in_b, cos128, sin128, s_bufs, lidx, row0=mp, nb=nbs, t=ts)
        rest, = _section_proj(xn, w_in_b, lidx, [(COL_CQ, False, F32)], n_cols=IN_REST_W, row0=0, nb=1, t=m,
                              tm_cap=1040, tn=4 * HEAD_W, name="in_proj_mla")
        kr_buf, kr128 = _shared_rope_key(xn, w_kr_pad, cos128, sin128, kr_buf, lidx)
        cqn, ckv_b, lat_buf = _post_mla(rest, gq, gkv, lat_buf, lidx)
        qcat = _matmul(cqn, w_uq_pad, lidx, out_dtype=BF16, tm_cap=1040, tn_cap=1024,
                       rope=(cos256, sin256), name="mla_q_up")
        kv = _matmul(ckv_b, mla_w_ukv, lidx, out_dtype=BF16, tm_cap=1040, tn_cap=1024, name="mla_kv_up")
        oa = _diff_prompt(qa_p, p_bufs[0], p_bufs[1], diff_lambda, subln, lam_init, lidx, nb=nbp, t=tp, rows_total=m)
        oa = _diff_sample(qa_s, cache_diff_k, cache_diff_v, s_bufs[0], s_bufs[1], diff_lambda, subln,
                          lam_init, oa, lidx, nb=nbs, t=ts, past=past, row0=mp)
        ob = _sb_prompt(qb_p, p_bufs[2], p_bufs[3], lidx, nb=nbp, t=tp, rows_total=m)
        ob = _sb_sample(qb_s, cache_sb_k, cache_sb_v, s_bufs[2], s_bufs[3], ob, lidx, nb=nbs, t=ts, past=past,
                        row0=mp)
        oc = _mla_prompt(qcat, kv, kr128, nb=nbp, t=tp, rows_total=m)
        oc = _mla_sample(qcat, cache_mla_latent, cache_mla_krope, ckv_b, kr128, mla_w_ukv, oc, lidx, nb=nbs, t=ts,
                         past=past, row0=mp)
        merged = _gate_merge(xn, oa, ob, oc, w_gate_b, w_a, w_b, w_c, lidx)
        h = _matmul(merged, w_out, lidx, out_dtype=F32, tm_cap=1040, tn_cap=512, res=h, name="out_proj")
        xn2 = _rmsnorm_rows(h, ffn_g, lidx, BF16)
        hid = _swiglu(xn2, ffn_w1, ffn_w3, lidx)
        h = _matmul(hid, w2, lidx, out_dtype=F32, tm_cap=640, tn_cap=256, res=h, name="ffn_down")
        return h, tuple(p_bufs), tuple(s_bufs), lat_buf, kr_buf

    carry = (h0, head_bufs(nbp, tp), head_bufs(nbs, ts),
             jnp.zeros((depth, m, MLA_KV_LORA), F32), jnp.zeros((depth, m, MLA_ROPE), F32))
    h, p_bufs, s_bufs, lat_buf, kr_buf = lax.fori_loop(0, depth, layer, carry)

    final_g, l0 = final_norm.reshape(1, 1, d), jnp.zeros((1,), jnp.int32)
    y_prompt = _rmsnorm_rows(h, final_g, l0, F32, row0=0, rows=mp).reshape(nbp, tp, d)
    y_sample = _rmsnorm_rows(h, final_g, l0, F32, row0=mp, rows=ms).reshape(nbs, ts, d)
    p_lat = lat_buf[:, :mp].reshape(depth, nbp, tp, MLA_KV_LORA)
    s_lat = lat_buf[:, mp:].reshape(depth, nbs, ts, MLA_KV_LORA)
    p_kr = kr_buf[:, :mp].reshape(depth, nbp, tp, MLA_ROPE)
    s_kr = kr_buf[:, mp:].reshape(depth, nbs, ts, MLA_ROPE)
    return (y_prompt, y_sample, *p_bufs, p_lat, p_kr, *s_bufs, s_lat, s_kr)
```

```python
import functools
import math

import jax
import jax.numpy as jnp
from jax import lax
from jax.experimental import pallas as pl
from jax.experimental.pallas import tpu as pltpu

D_MODEL = 4096
BATCH = 2
SEQ = 4096
DEPTH = 4
DEC_BATCH = 8
DEC_SEQ = 16
PAST_LEN = 2048

CHUNK = 64
ROPE_THETA = 10000.0
NORM_EPS = 1e-6
NEG_INF = -1e30

DIFF_HEADS = 8
DIFF_DH = 64
SB_HEADS = 8
MLA_HEADS = 16
MLA_ROPE = 64
MLA_Q_LORA = 1024
MLA_KV_LORA = 512
HEAD_W = 128
MLA_QK_PAD = 2 * HEAD_W
N_BRANCHES = 3
COL_CQ, COL_KR = 6144, 7680
FFN_HIDDEN = -(-8 * D_MODEL // 768) * 256

F32 = jnp.float32
BF16 = jnp.bfloat16

V7X_LANES = 128
V7X_BF16_SUBLANES = 16
V7X_VMEM_LIMIT_CAP = 60 * 1024 * 1024


def _tile(n, cap, mult):
    best = None
    for t in range(mult, min(n, cap) + 1, mult):
        if n % t == 0:
            best = t
    if best is None:
        raise ValueError(f"no tile for {n} (cap {cap}, multiple of {mult})")
    return best


def _params(block_bytes):
    need = 2 * block_bytes + (8 << 20)
    return pltpu.CompilerParams(vmem_limit_bytes=int(min(max(need, 32 << 20), V7X_VMEM_LIMIT_CAP)))


def _nbytes(shape, dtype):
    return math.prod(shape) * jnp.dtype(dtype).itemsize


def _dot(a, b):
    return jnp.dot(a, b, preferred_element_type=F32)


def _dot_nt(a, b):
    return lax.dot_general(a, b, (((1,), (1,)), ((), ())), preferred_element_type=F32)


def _rms(x, g):
    return x * lax.rsqrt(jnp.mean(x * x, axis=-1, keepdims=True) + NORM_EPS) * g


def _rope_lanes(x, cos, sin):
    lane = lax.broadcasted_iota(jnp.int32, x.shape, 1)
    first_half = (lane % 64) < 32
    partner = jnp.where(first_half, pltpu.roll(x, x.shape[1] - 32, 1), pltpu.roll(x, 32, 1))
    return x * cos + partner * sin


def _rmsnorm_kernel(l_ref, x_ref, g_ref, o_ref):
    del l_ref
    o_ref[...] = _rms(x_ref[...], g_ref[...]).astype(o_ref.dtype)


def _rmsnorm_rows(x, g_stack, lidx, out_dtype, row0=0, rows=None):
    d = x.shape[1]
    m = x.shape[0] if rows is None else rows
    tr = _tile(math.gcd(m, row0) if row0 else m, 320, V7X_BF16_SUBLANES)
    rb0 = row0 // tr
    blk = _nbytes((tr, d), F32) + _nbytes((tr, d), out_dtype)
    return pl.pallas_call(
        _rmsnorm_kernel,
        grid_spec=pltpu.PrefetchScalarGridSpec(
            num_scalar_prefetch=1, grid=(m // tr,),
            in_specs=[pl.BlockSpec((tr, d), lambda i, l: (rb0 + i, 0)),
                      pl.BlockSpec((None, 1, d), lambda i, l: (l[0], 0, 0))],
            out_specs=pl.BlockSpec((tr, d), lambda i, l: (i, 0))),
        out_shape=jax.ShapeDtypeStruct((m, d), out_dtype),
        compiler_params=_params(blk), name="rmsnorm_rows",
    )(lidx, x, g_stack)


def _cast_kernel(x_ref, o_ref):
    o_ref[...] = x_ref[...].astype(o_ref.dtype)


def _to_bf16(w):
    depth, kdim, n = w.shape
    tr = _tile(kdim, max(V7X_BF16_SUBLANES, (8 << 20) // (4 * n)), V7X_BF16_SUBLANES)
    return pl.pallas_call(
        _cast_kernel, grid=(depth, kdim // tr),
        in_specs=[pl.BlockSpec((None, tr, n), lambda l, i: (l, i, 0))],
        out_specs=pl.BlockSpec((None, tr, n), lambda l, i: (l, i, 0)),
        out_shape=jax.ShapeDtypeStruct(w.shape, BF16),
        compiler_params=_params(_nbytes((tr, n), F32) + _nbytes((tr, n), BF16)), name="to_bf16",
    )(w)


def _mm_kernel(l_ref, x_ref, w_ref, *rest, nk, has_res, rope):
    del l_ref
    rest = list(rest)
    r_ref = rest.pop(0) if has_res else None
    cos_ref, sin_ref = (rest.pop(0), rest.pop(0)) if rope else (None, None)
    o_ref = rest.pop(0)
    part = _dot(x_ref[...].astype(BF16), w_ref[...].astype(BF16))

    def finish(acc):
        if has_res:
            acc = acc + r_ref[...]
        if rope:
            cos, sin = cos_ref[...], sin_ref[...]
            gw = cos.shape[1]
            for g in range(acc.shape[1] // gw):
                seg = acc[:, g * gw:(g + 1) * gw]
                o_ref[:, g * gw:(g + 1) * gw] = _rope_lanes(seg, cos, sin).astype(o_ref.dtype)
        else:
            o_ref[...] = acc.astype(o_ref.dtype)

    if nk == 1:
        finish(part)
    else:
        acc_ref, = rest
        k = pl.program_id(2)

        @pl.when(k == 0)
        def _():
            acc_ref[...] = part

        @pl.when(k > 0)
        def _():
            acc_ref[...] += part

        @pl.when(k == nk - 1)
        def _():
            finish(acc_ref[...])


def _matmul(x, w_stack, lidx, *, out_dtype, tm_cap, tn_cap, tk=None, res=None, rope=None,
            x_stacked=False, name="matmul"):
    m, kdim = x.shape[-2:]
    n = w_stack.shape[2]
    tm = _tile(m, tm_cap, V7X_BF16_SUBLANES)
    tn = _tile(n, tn_cap, V7X_LANES) if n % V7X_LANES == 0 else n
    tk = kdim if tk is None else tk
    nk = kdim // tk
    if x_stacked:
        x_spec = pl.BlockSpec((None, tm, tk), lambda i, j, k, l: (l[0], i, k))
    else:
        x_spec = pl.BlockSpec((tm, tk), lambda i, j, k, l: (i, k))
    in_specs = [x_spec, pl.BlockSpec((None, tk, tn), lambda i, j, k, l: (l[0], k, j))]
    args = [x, w_stack]
    blk = _nbytes((tm, tk), x.dtype) + _nbytes((tk, tn), w_stack.dtype) + _nbytes((tm, tn), F32)
    if res is not None:
        in_specs.append(pl.BlockSpec((tm, tn), lambda i, j, k, l: (i, j)))
        args.append(res)
        blk += _nbytes((tm, tn), F32)
    if rope is not None:
        gw = rope[0].shape[1]
        assert tn % gw == 0
        in_specs += [pl.BlockSpec((tm, gw), lambda i, j, k, l: (i, 0))] * 2
        args += list(rope)
        blk += 2 * _nbytes((tm, gw), F32)
    return pl.pallas_call(
        functools.partial(_mm_kernel, nk=nk, has_res=res is not None, rope=rope is not None),
        grid_spec=pltpu.PrefetchScalarGridSpec(
            num_scalar_prefetch=1, grid=(m // tm, n // tn, nk),
            in_specs=in_specs,
            out_specs=pl.BlockSpec((tm, tn), lambda i, j, k, l: (i, j)),
            scratch_shapes=[pltpu.VMEM((tm, tn), F32)] if nk > 1 else []),
        out_shape=jax.ShapeDtypeStruct((m, n), out_dtype),
        compiler_params=_params(blk), name=name,
    )(lidx, *args)


def _kr_kernel(l_ref, xn_ref, w_ref, cos_ref, sin_ref, kr_in, kr_o, kr128_o):
    del l_ref, kr_in
    r = _rope_lanes(_dot(xn_ref[...], w_ref[...]), cos_ref[...], sin_ref[...])
    kr_o[...] = r[:, :MLA_ROPE]
    kr128_o[...] = r.astype(BF16)


def _shared_rope_key(xn, w_kr_pad, cos, sin, kr_buf, lidx):
    m, d = xn.shape
    tm = _tile(m, 640, V7X_BF16_SUBLANES)
    blk = _nbytes((tm, d), BF16) + _nbytes((d, HEAD_W), BF16) + 4 * _nbytes((tm, HEAD_W), F32)
    return pl.pallas_call(
        _kr_kernel,
        grid_spec=pltpu.PrefetchScalarGridSpec(
            num_scalar_prefetch=1, grid=(m // tm,),
            in_specs=[pl.BlockSpec((tm, d), lambda i, l: (i, 0)),
                      pl.BlockSpec((None, d, HEAD_W), lambda i, l: (l[0], 0, 0)),
                      pl.BlockSpec((tm, HEAD_W), lambda i, l: (i, 0)),
                      pl.BlockSpec((tm, HEAD_W), lambda i, l: (i, 0)),
                      pl.BlockSpec(memory_space=pl.ANY)],
            out_specs=[pl.BlockSpec((None, tm, MLA_ROPE), lambda i, l: (l[0], i, 0)),
                       pl.BlockSpec((tm, HEAD_W), lambda i, l: (i, 0))]),
        out_shape=[jax.ShapeDtypeStruct(kr_buf.shape, F32), jax.ShapeDtypeStruct((m, HEAD_W), BF16)],
        input_output_aliases={5: 0},
        compiler_params=_params(blk), name="shared_rope_key",
    )(lidx, xn, w_kr_pad, cos, sin, kr_buf)


IN_SECTION_W = DIFF_HEADS * HEAD_W
IN_REST_W = COL_KR - COL_CQ
MIN_SUBTILE_ROWS = 256


def _sections_kernel(l_ref, x_ref, *refs, kinds, subtiles):
    del l_ref
    n = len(kinds)
    w_refs, rest = refs[:n], list(refs[n:])
    cos_ref, sin_ref = (rest.pop(0), rest.pop(0)) if any(rope for rope, _ in kinds) else (None, None)
    o_refs = rest[len(rest) - n:]
    sub = x_ref.shape[0] // subtiles
    for (rope, heads), w_ref, o_ref in zip(kinds, w_refs, o_refs):
        w = w_ref[...].astype(BF16)
        for r in range(subtiles):
            r0 = r * sub
            acc = _dot(x_ref[r0:r0 + sub, :], w)
            for h in range(acc.shape[1] // HEAD_W):
                seg = acc[:, h * HEAD_W:(h + 1) * HEAD_W]
                if rope:
                    seg = _rope_lanes(seg, cos_ref[r0:r0 + sub, :], sin_ref[r0:r0 + sub, :])
                if heads is None:
                    o_ref[r0:r0 + sub, h * HEAD_W:(h + 1) * HEAD_W] = seg.astype(o_ref.dtype)
                elif heads[0] == 1:
                    o_ref[0, h, r0:r0 + sub, :] = seg
                else:
                    tt = heads[1]
                    for b in range(sub // tt):
                        o_ref[r0 // tt + b, h] = seg[b * tt:(b + 1) * tt, :]


def _section_proj(xn, w_in, lidx, sections, *, n_cols, row0, nb, t, tm_cap, tn, rope=None, name="in_proj"):
    d = xn.shape[1]
    rows = nb * t
    tm = _tile(math.gcd(rows, row0) if row0 else rows, tm_cap, V7X_BF16_SUBLANES)
    assert n_cols % tn == 0 and all(col0 % tn == 0 for col0, _, _ in sections)
    rb0 = row0 // tm
    subtiles = max(k for k in (1, 2, 4) if tm % (k * MIN_SUBTILE_ROWS) == 0 or k == 1)
    bpt, tt = (1, tm) if tm <= t else (tm // t, t)
    nt = t // tt

    def w_spec(col0):
        return pl.BlockSpec((None, d, tn), lambda i, j, l: (l[0], 0, col0 // tn + j))

    in_specs = [pl.BlockSpec((tm, d), lambda i, j, l: (rb0 + i, 0))] + [w_spec(col0) for col0, _, _ in sections]
    args = [xn] + [w_in] * len(sections)
    if any(use_rope for _, use_rope, _ in sections):
        in_specs += [pl.BlockSpec((tm, HEAD_W), lambda i, j, l: (rb0 + i, 0))] * 2
        args += list(rope)
    kinds, out_specs, out_shapes, aliases = [], [], [], {}
    for k, (_, use_rope, out) in enumerate(sections):
        if isinstance(out, jax.Array):
            assert (t % tm == 0) if bpt == 1 else (tm % t == 0 and (tm // subtiles) % t == 0)
            kinds.append((use_rope, (bpt, tt)))
            in_specs.append(pl.BlockSpec(memory_space=pl.ANY))
            args.append(out)
            aliases[len(args)] = k
            out_specs.append(pl.BlockSpec((None, bpt, tn // HEAD_W, tt, HEAD_W),
                                          lambda i, j, l: (l[0], i // nt, j, i % nt, 0)))
            out_shapes.append(jax.ShapeDtypeStruct(out.shape, F32))
        else:
            kinds.append((use_rope, None))
            out_specs.append(pl.BlockSpec((tm, tn), lambda i, j, l: (i, j)))
            out_shapes.append(jax.ShapeDtypeStruct((rows, n_cols), out))
    blk = (_nbytes((tm, d), BF16) + 2 * _nbytes((tm, HEAD_W), F32)
           + len(sections) * (_nbytes((d, tn), w_in.dtype) + 2 * _nbytes((tm, tn), F32)))
    return pl.pallas_call(
        functools.partial(_sections_kernel, kinds=tuple(kinds), subtiles=subtiles),
        grid_spec=pltpu.PrefetchScalarGridSpec(
            num_scalar_prefetch=1, grid=(rows // tm, n_cols // tn), in_specs=in_specs, out_specs=out_specs),
        out_shape=out_shapes, input_output_aliases=aliases,
        compiler_params=_params(blk), name=name,
    )(lidx, *args)


def _in_proj_heads(xn, w_in, cos, sin, bufs, lidx, *, row0, nb, t):
    common = dict(row0=row0, nb=nb, t=t, tm_cap=1024, tn=2 * HEAD_W, n_cols=IN_SECTION_W, rope=(cos, sin))
    sec = lambda k: k * IN_SECTION_W
    qa, dk, qb = _section_proj(xn, w_in, lidx, [(sec(0), True, BF16), (sec(1), True, bufs[0]),
                                                (sec(3), False, BF16)], **common)
    dv, sk, sv = _section_proj(xn, w_in, lidx, [(sec(2), False, bufs[1]), (sec(4), False, bufs[2]),
                                                (sec(5), False, bufs[3])], **common)
    return qa, qb, dk, dv, sk, sv


def _post_mla_kernel(l_ref, cq_ref, ckv_ref, gq_ref, gkv_ref, lat_in, cqn_o, ckvb_o, lat_o):
    del l_ref, lat_in
    cqn_o[...] = _rms(cq_ref[...], gq_ref[...]).astype(BF16)
    c = _rms(ckv_ref[...], gkv_ref[...])
    lat_o[...] = c
    ckvb_o[...] = c.astype(BF16)


def _post_mla(proj, gq, gkv, lat_buf, lidx):
    m = proj.shape[0]
    tt = _tile(m, 640, V7X_BF16_SUBLANES)
    blk = 3 * _nbytes((tt, MLA_Q_LORA + MLA_KV_LORA), F32)
    return pl.pallas_call(
        _post_mla_kernel,
        grid_spec=pltpu.PrefetchScalarGridSpec(
            num_scalar_prefetch=1, grid=(m // tt,),
            in_specs=[pl.BlockSpec((tt, MLA_Q_LORA), lambda i, l: (i, 0)),
                      pl.BlockSpec((tt, MLA_KV_LORA), lambda i, l: (i, MLA_Q_LORA // MLA_KV_LORA)),
                      pl.BlockSpec((None, 1, MLA_Q_LORA), lambda i, l: (l[0], 0, 0)),
                      pl.BlockSpec((None, 1, MLA_KV_LORA), lambda i, l: (l[0], 0, 0)),
                      pl.BlockSpec(memory_space=pl.ANY)],
            out_specs=[pl.BlockSpec((tt, MLA_Q_LORA), lambda i, l: (i, 0)),
                       pl.BlockSpec((tt, MLA_KV_LORA), lambda i, l: (i, 0)),
                       pl.BlockSpec((None, tt, MLA_KV_LORA), lambda i, l: (l[0], i, 0))]),
        out_shape=[jax.ShapeDtypeStruct((m, MLA_Q_LORA), BF16),
                   jax.ShapeDtypeStruct((m, MLA_KV_LORA), BF16),
                   jax.ShapeDtypeStruct(lat_buf.shape, F32)],
        input_output_aliases={5: 2},
        compiler_params=_params(blk), name="post_mla",
    )(lidx, proj, proj, gq, gkv, lat_buf)


LOG2E = math.log2(math.e)
SB_DEAD_LOG = -104.0


def _with_ones(v):
    return jnp.concatenate([v, jnp.ones(v.shape, v.dtype)], axis=1)


def _softmax_step(s2, v_ext, m_ref, acc_ref):
    m_prev = m_ref[...]
    m_new = jnp.maximum(m_prev, jnp.max(s2, axis=-1, keepdims=True))
    alpha = jnp.exp2(m_prev - m_new)
    lanes = m_prev.shape[1]
    p = jnp.concatenate([jnp.exp2(s2[:, c * lanes:(c + 1) * lanes] - m_new).astype(BF16)
                         for c in range(s2.shape[1] // lanes)], axis=1)
    acc_ref[...] = jnp.concatenate([alpha, alpha], axis=1) * acc_ref[...] + _dot(p, v_ext)
    m_ref[...] = m_new


def _softmax_init(m_ref, acc_ref):
    m_ref[...] = jnp.full(m_ref.shape, NEG_INF, F32)
    acc_ref[...] = jnp.zeros(acc_ref.shape, F32)


def _softmax_result(acc_ref):
    acc = acc_ref[...]
    return acc[:, :HEAD_W] / acc[:, HEAD_W:]


def _positions(shape, q0, k0):
    qpos = q0 + lax.broadcasted_iota(jnp.int32, shape, 0)
    kpos = k0 + lax.broadcasted_iota(jnp.int32, shape, 1)
    return qpos, kpos


def _chunk_mask(shape, q0, k0):
    qpos, kpos = _positions(shape, q0, k0)
    return (kpos // CHUNK) <= (qpos // CHUNK)


def _diff_lambda(lam_ref, li_ref):
    lv = lam_ref[...]
    lam_init = li_ref[:, 0:1]
    d1 = jnp.sum(lv[0:1, :] * lv[1:2, :], axis=-1, keepdims=True)
    d2 = jnp.sum(lv[2:3, :] * lv[3:4, :], axis=-1, keepdims=True)
    return jnp.exp(d1) - jnp.exp(d2) + lam_init, lam_init


def _split_components(q):
    lane = lax.broadcasted_iota(jnp.int32, q.shape, 1)
    zero = jnp.zeros_like(q)
    return jnp.where(lane < DIFF_DH, q, zero), jnp.where(lane >= DIFF_DH, q, zero)


def _softplus(z):
    return jnp.maximum(z, 0.0) + jnp.log1p(jnp.exp(-jnp.abs(z)))


def _upper_ones(n):
    r = lax.broadcasted_iota(jnp.int32, (n, n), 0)
    c = lax.broadcasted_iota(jnp.int32, (n, n), 1)
    return jnp.where(r > c, 1.0, 0.0).astype(BF16)


def _sb_block(z, mask, carry, v, tri):
    sp = _softplus(z)
    log_stay = -sp if mask is None else jnp.where(mask, -sp, 0.0)
    hi = log_stay.astype(BF16)
    lo = (log_stay - hi.astype(F32)).astype(BF16)
    between = _dot(hi, tri) + _dot(lo, tri) + carry
    a = jnp.exp(z - sp + between)
    if mask is not None:
        a = jnp.where(mask, a, 0.0)
    return _dot(a.astype(BF16), v), carry + jnp.sum(log_stay, axis=-1, keepdims=True)


def _attn_tiles(t):
    tq = _tile(t, 512, CHUNK)
    return tq, tq


def _diff_prompt_kernel(l_ref, q_ref, k_ref, v_ref, lam_ref, g_ref, li_ref, bias_ref, o_ref, m_ref, acc_ref, s_ref,
                        *, tq, tk, hp):
    del l_ref
    assert tq == tk
    qi = pl.program_id(2)
    scale2 = DIFF_DH ** -0.5 * LOG2E
    for c in range(2 * hp):
        _softmax_init(m_ref.at[c], acc_ref.at[c])

    def rows(j):
        return pl.ds(pl.multiple_of(j * tk, tk), tk)

    def store_logits(j, hh):
        kb = k_ref[hh, rows(j), :].astype(BF16)
        q1, q2 = _split_components(q_ref[:, hh * HEAD_W:(hh + 1) * HEAD_W])
        s_ref[2 * hh] = _dot_nt(q1, kb) * scale2
        s_ref[2 * hh + 1] = _dot_nt(q2, kb) * scale2

    def step(j, hh, bias):
        v_ext = _with_ones(v_ref[hh, rows(j), :].astype(BF16))
        for c in (2 * hh, 2 * hh + 1):
            s2 = s_ref[c] if bias is None else s_ref[c] + bias
            _softmax_step(s2, v_ext, m_ref.at[c], acc_ref.at[c])

    for hh in range(hp):
        store_logits(0, hh)

    def body(j, carry):
        for hh in range(hp):
            step(j, hh, None)
            store_logits(j + 1, hh)
        return carry

    lax.fori_loop(0, qi, body, 0)
    lam, lam_init = _diff_lambda(lam_ref, li_ref)
    for hh in range(hp):
        step(qi, hh, bias_ref[...])
        o = _softmax_result(acc_ref.at[2 * hh]) - lam * _softmax_result(acc_ref.at[2 * hh + 1])
        o_ref[:, hh * HEAD_W:(hh + 1) * HEAD_W] = (_rms(o, g_ref[...]) * (1.0 - lam_init)).astype(o_ref.dtype)


def _diag_bias(tq, tk):
    r = jnp.arange(tq, dtype=jnp.int32)[:, None] // CHUNK
    c = jnp.arange(tk, dtype=jnp.int32)[None, :] // CHUNK
    return jnp.where(c <= r, 0.0, NEG_INF).astype(F32)


def _diff_prompt(q, k_buf, v_buf, lam, g, li, lidx, *, nb, t, rows_total):
    tq, tk = _attn_tiles(t)
    nq = t // tq
    hp = 2
    kv_spec = pl.BlockSpec((None, None, hp, t, HEAD_W), lambda b, h, i, l: (l[0], b, h, 0, 0))
    blk = 2 * hp * _nbytes((t, HEAD_W), F32) + (1 + 10 * hp) * _nbytes((tq, tk), F32)
    return pl.pallas_call(
        functools.partial(_diff_prompt_kernel, tq=tq, tk=tk, hp=hp),
        grid_spec=pltpu.PrefetchScalarGridSpec(
            num_scalar_prefetch=1, grid=(nb, DIFF_HEADS // hp, nq),
            in_specs=[pl.BlockSpec((tq, hp * HEAD_W), lambda b, h, i, l: (b * nq + i, h)),
                      kv_spec, kv_spec,
                      pl.BlockSpec((None, 4, DIFF_DH), lambda b, h, i, l: (l[0], 0, 0)),
                      pl.BlockSpec((None, 1, HEAD_W), lambda b, h, i, l: (l[0], 0, 0)),
                      pl.BlockSpec((None, 1, HEAD_W), lambda b, h, i, l: (l[0], 0, 0)),
                      pl.BlockSpec((tq, tk), lambda b, h, i, l: (0, 0))],
            out_specs=pl.BlockSpec((tq, hp * HEAD_W), lambda b, h, i, l: (b * nq + i, h)),
            scratch_shapes=[pltpu.VMEM((2 * hp, tq, HEAD_W), F32), pltpu.VMEM((2 * hp, tq, 2 * HEAD_W), F32),
                            pltpu.VMEM((2 * hp, tq, tk), F32)]),
        out_shape=jax.ShapeDtypeStruct((rows_total, DIFF_HEADS * HEAD_W), BF16),
        compiler_params=_params(blk), name="diff_prompt",
    )(lidx, q, k_buf, v_buf, lam, g, li, _diag_bias(tq, tk))


def _sb_prompt_kernel(l_ref, q_ref, k_ref, v_ref, o_ref, c_ref, acc_ref, *, tq, tk, hp):
    del l_ref
    qi = pl.program_id(2)
    ratio = tq // tk
    scale = HEAD_W ** -0.5
    tri = _upper_ones(tk)
    c_ref[...] = jnp.zeros(c_ref.shape, F32)
    acc_ref[...] = jnp.zeros(acc_ref.shape, F32)

    def block(j, masked):
        start = pl.multiple_of(j * tk, tk)
        mask = None
        if masked:
            qpos, kpos = _positions((tq, tk), qi * tq, start)
            mask = kpos < qpos
        for hh in range(hp):
            kb = k_ref[hh, pl.ds(start, tk), :].astype(BF16)
            vb = v_ref[hh, pl.ds(start, tk), :].astype(BF16)
            z = _dot_nt(q_ref[:, hh * HEAD_W:(hh + 1) * HEAD_W], kb) * scale
            out, carry = _sb_block(z, mask, c_ref[hh], vb, tri)
            acc_ref[hh] += out
            c_ref[hh] = carry

    for u in reversed(range(ratio)):
        block(qi * ratio + u, True)

    def alive():
        return jnp.max(c_ref[...]) > SB_DEAD_LOG

    def cond(state):
        j, live = state
        return jnp.logical_and(j >= 0, live)

    def body(state):
        j, _ = state
        block(j, False)
        return j - 1, alive()

    lax.while_loop(cond, body, (qi * ratio - 1, alive()))
    for hh in range(hp):
        o_ref[:, hh * HEAD_W:(hh + 1) * HEAD_W] = acc_ref[hh].astype(o_ref.dtype)


def _sb_prompt(q, k_buf, v_buf, lidx, *, nb, t, rows_total):
    tq, _ = _attn_tiles(t)
    tk = _tile(tq, 256, CHUNK)
    nq = t // tq
    hp = 2
    kv_spec = pl.BlockSpec((None, None, hp, t, HEAD_W), lambda b, h, i, l: (l[0], b, h, 0, 0))
    blk = 2 * hp * _nbytes((t, HEAD_W), F32) + 16 * hp * _nbytes((tq, tk), F32)
    return pl.pallas_call(
        functools.partial(_sb_prompt_kernel, tq=tq, tk=tk, hp=hp),
        grid_spec=pltpu.PrefetchScalarGridSpec(
            num_scalar_prefetch=1, grid=(nb, SB_HEADS // hp, nq),
            in_specs=[pl.BlockSpec((tq, hp * HEAD_W), lambda b, h, i, l: (b * nq + i, h)), kv_spec, kv_spec],
            out_specs=pl.BlockSpec((tq, hp * HEAD_W), lambda b, h, i, l: (b * nq + i, h)),
            scratch_shapes=[pltpu.VMEM((hp, tq, 1), F32), pltpu.VMEM((hp, tq, HEAD_W), F32)]),
        out_shape=jax.ShapeDtypeStruct((rows_total, SB_HEADS * HEAD_W), BF16),
        compiler_params=_params(blk), name="sb_prompt",
    )(lidx, q, k_buf, v_buf)


def _mla_prompt_kernel(q_ref, kv_ref, kr_ref, bias_ref, o_ref, m_ref, acc_ref, s_ref, *, tq, tk, hp):
    assert tq == tk
    qi = pl.program_id(2)
    scale2 = (HEAD_W + MLA_ROPE) ** -0.5 * LOG2E
    for hh in range(hp):
        _softmax_init(m_ref.at[hh], acc_ref.at[hh])

    def rows(j):
        return pl.ds(pl.multiple_of(j * tk, tk), tk)

    def logits(j, hh):
        kcat = jnp.concatenate([kv_ref[rows(j), 2 * hh * HEAD_W:(2 * hh + 1) * HEAD_W], kr_ref[rows(j), :]], axis=1)
        return _dot_nt(q_ref[:, hh * MLA_QK_PAD:(hh + 1) * MLA_QK_PAD], kcat) * scale2

    def values(j, hh):
        return _with_ones(kv_ref[rows(j), (2 * hh + 1) * HEAD_W:(2 * hh + 2) * HEAD_W])

    for hh in range(hp):
        s_ref[hh] = logits(0, hh)

    def body(j, c):
        for hh in range(hp):
            _softmax_step(s_ref[hh], values(j, hh), m_ref.at[hh], acc_ref.at[hh])
            s_ref[hh] = logits(j + 1, hh)
        return c

    lax.fori_loop(0, qi, body, 0)
    for hh in range(hp):
        _softmax_step(s_ref[hh] + bias_ref[...], values(qi, hh), m_ref.at[hh], acc_ref.at[hh])
        o_ref[:, hh * HEAD_W:(hh + 1) * HEAD_W] = _softmax_result(acc_ref.at[hh]).astype(o_ref.dtype)


def _mla_prompt(qcat, kv, kr128, *, nb, t, rows_total):
    tq, tk = _attn_tiles(t)
    nq = t // tq
    hp = 4
    blk = (2 * hp + 1) * _nbytes((t, HEAD_W), BF16) + (1 + 6 * hp) * _nbytes((tq, tk), F32)
    return pl.pallas_call(
        functools.partial(_mla_prompt_kernel, tq=tq, tk=tk, hp=hp),
        grid=(nb, MLA_HEADS // hp, nq),
        in_specs=[pl.BlockSpec((tq, hp * MLA_QK_PAD), lambda b, h, i: (b * nq + i, h)),
                  pl.BlockSpec((t, 2 * hp * HEAD_W), lambda b, h, i: (b, h)),
                  pl.BlockSpec((t, HEAD_W), lambda b, h, i: (b, 0)),
                  pl.BlockSpec((tq, tk), lambda b, h, i: (0, 0))],
        out_specs=pl.BlockSpec((tq, hp * HEAD_W), lambda b, h, i: (b * nq + i, h)),
        scratch_shapes=[pltpu.VMEM((hp, tq, HEAD_W), F32), pltpu.VMEM((hp, tq, 2 * HEAD_W), F32),
                        pltpu.VMEM((hp, tq, tk), F32)],
        out_shape=jax.ShapeDtypeStruct((rows_total, MLA_HEADS * HEAD_W), BF16),
        compiler_params=_params(blk), name="mla_prompt",
    )(qcat, kv, kr128, _diag_bias(tq, tk))


def _pad_rows(x, rows):
    return jnp.concatenate([x, jnp.zeros((rows - x.shape[0], x.shape[1]), x.dtype)], axis=0)


def _two_part_softmax(sp, sn, vp, vn):
    m = jnp.maximum(jnp.max(sp, axis=-1, keepdims=True), jnp.max(sn, axis=-1, keepdims=True))
    pp, pn = jnp.exp(sp - m), jnp.exp(sn - m)
    denom = jnp.sum(pp, axis=-1, keepdims=True) + jnp.sum(pn, axis=-1, keepdims=True)
    return (_dot(pp.astype(BF16), vp) + _dot(pn.astype(BF16), vn)) / denom


def _diff_sample_kernel(l_ref, q_ref, kp_ref, vp_ref, kn_ref, vn_ref, lam_ref, g_ref, li_ref, prev_ref, o_ref,
                        *, past, t):
    del l_ref, prev_ref
    q12 = jnp.concatenate(_split_components(q_ref[...]), axis=0)
    scale = DIFF_DH ** -0.5
    kp, vp = kp_ref[...].astype(BF16), vp_ref[...].astype(BF16)
    kn = _pad_rows(kn_ref[...], HEAD_W).astype(BF16)
    vn = _pad_rows(vn_ref[...], HEAD_W).astype(BF16)

    def chunk_mask(width, k0):
        qpos = past + lax.broadcasted_iota(jnp.int32, (2 * t, width), 0) % t
        kpos = k0 + lax.broadcasted_iota(jnp.int32, (2 * t, width), 1)
        return ((kpos // CHUNK) <= (qpos // CHUNK)) & (kpos < past + t)

    sp = jnp.where(chunk_mask(past, 0), _dot_nt(q12, kp) * scale, NEG_INF)
    sn = jnp.where(chunk_mask(HEAD_W, past), _dot_nt(q12, kn) * scale, NEG_INF)
    o12 = _two_part_softmax(sp, sn, vp, vn)
    lam, lam_init = _diff_lambda(lam_ref, li_ref)
    o = o12[:t] - lam * o12[t:]
    o_ref[...] = (_rms(o, g_ref[...]) * (1.0 - lam_init)).astype(o_ref.dtype)


def _diff_sample(q, k_cache, v_cache, k_buf, v_buf, lam, g, li, merged, lidx, *, nb, t, past, row0):
    rb0 = row0 // t
    cache_spec = pl.BlockSpec((None, None, None, past, HEAD_W), lambda b, h, l: (l[0], b, h, 0, 0))
    new_spec = pl.BlockSpec((None, None, None, t, HEAD_W), lambda b, h, l: (l[0], b, h, 0, 0))
    blk = 2 * _nbytes((past, HEAD_W), F32) + 8 * _nbytes((t, past), F32)
    return pl.pallas_call(
        functools.partial(_diff_sample_kernel, past=past, t=t),
        grid_spec=pltpu.PrefetchScalarGridSpec(
            num_scalar_prefetch=1, grid=(nb, DIFF_HEADS),
            in_specs=[pl.BlockSpec((t, HEAD_W), lambda b, h, l: (b, h)),
                      cache_spec, cache_spec, new_spec, new_spec,
                      pl.BlockSpec((None, 4, DIFF_DH), lambda b, h, l: (l[0], 0, 0)),
                      pl.BlockSpec((None, 1, HEAD_W), lambda b, h, l: (l[0], 0, 0)),
                      pl.BlockSpec((None, 1, HEAD_W), lambda b, h, l: (l[0], 0, 0)),
                      pl.BlockSpec(memory_space=pl.ANY)],
            out_specs=pl.BlockSpec((t, HEAD_W), lambda b, h, l: (rb0 + b, h))),
        out_shape=jax.ShapeDtypeStruct(merged.shape, BF16),
        input_output_aliases={9: 0},
        compiler_params=_params(blk), name="diff_sample",
    )(lidx, q, k_cache, v_cache, k_buf, v_buf, lam, g, li, merged)


def _sb_sample_kernel(l_ref, q_ref, kp_ref, vp_ref, kn_ref, vn_ref, prev_ref, o_ref, *, past, t, cw):
    del l_ref, prev_ref
    q = q_ref[...]
    scale = HEAD_W ** -0.5
    nc = past // cw
    kn = _pad_rows(kn_ref[...], cw).astype(BF16)
    vn = _pad_rows(vn_ref[...], cw).astype(BF16)
    z_past = _dot_nt(q, kp_ref[...].astype(BF16)) * scale
    z = [z_past[:, c * cw:(c + 1) * cw] for c in range(nc)] + [_dot_nt(q, kn) * scale]
    masks = []
    for c in range(nc + 1):
        qpos, kpos = _positions((t, cw), past, c * cw)
        masks.append((kpos < qpos) & (kpos < past + t))
    sp = [_softplus(zc) for zc in z]
    log_stay = jnp.concatenate([jnp.where(m, -s, 0.0) for m, s in zip(masks, sp)], axis=0)
    hi = log_stay.astype(BF16)
    lo = (log_stay - hi.astype(F32)).astype(BF16)
    cum = _dot(jnp.concatenate([hi, lo], axis=0), _upper_ones(cw))
    rows = (nc + 1) * t
    cum = cum[:rows] + cum[rows:]
    carry = jnp.zeros((t, 1), F32)
    weights = [None] * (nc + 1)
    for c in reversed(range(nc + 1)):
        logw = z[c] - sp[c] + cum[c * t:(c + 1) * t] + carry
        weights[c] = jnp.where(masks[c], jnp.exp(logw), 0.0).astype(BF16)
        carry = carry + jnp.sum(log_stay[c * t:(c + 1) * t], axis=-1, keepdims=True)
    acc = _dot(jnp.concatenate(weights[:nc], axis=1), vp_ref[...].astype(BF16)) + _dot(weights[nc], vn)
    o_ref[...] = acc.astype(o_ref.dtype)


def _sb_sample(q, k_cache, v_cache, k_buf, v_buf, merged, lidx, *, nb, t, past, row0):
    rb0 = row0 // t
    cw = _tile(past, 256, V7X_LANES)
    cache_spec = pl.BlockSpec((None, None, None, past, HEAD_W), lambda b, h, l: (l[0], b, h, 0, 0))
    new_spec = pl.BlockSpec((None, None, None, t, HEAD_W), lambda b, h, l: (l[0], b, h, 0, 0))
    blk = 2 * _nbytes((past, HEAD_W), F32) + 8 * _nbytes((t, past), F32)
    return pl.pallas_call(
        functools.partial(_sb_sample_kernel, past=past, t=t, cw=cw),
        grid_spec=pltpu.PrefetchScalarGridSpec(
            num_scalar_prefetch=1, grid=(nb, SB_HEADS),
            in_specs=[pl.BlockSpec((t, HEAD_W), lambda b, h, l: (b, h)),
                      cache_spec, cache_spec, new_spec, new_spec, pl.BlockSpec(memory_space=pl.ANY)],
            out_specs=pl.BlockSpec((t, HEAD_W), lambda b, h, l: (rb0 + b, h))),
        out_shape=jax.ShapeDtypeStruct(merged.shape, BF16),
        input_output_aliases={6: 0},
        compiler_params=_params(blk), name="sb_sample",
    )(lidx, q, k_cache, v_cache, k_buf, v_buf, merged)


def _mla_sample_kernel(l_ref, q_ref, cp_ref, krp_ref, cn_ref, krn_ref, w_ref, prev_ref, o_ref, *, past, t):
    del l_ref, prev_ref
    scale = (HEAD_W + MLA_ROPE) ** -0.5
    q = q_ref[...]
    q_abs, q_rope = [], []
    for h in range(MLA_HEADS):
        w_uk = w_ref[:, 2 * h * HEAD_W:(2 * h + 1) * HEAD_W].astype(BF16)
        q_abs.append(_dot_nt(q[:, h * MLA_QK_PAD:h * MLA_QK_PAD + HEAD_W], w_uk).astype(BF16))
        q_rope.append(q[:, h * MLA_QK_PAD + HEAD_W:(h + 1) * MLA_QK_PAD])
    q_abs = jnp.concatenate(q_abs, axis=0)
    q_rope = jnp.concatenate(q_rope, axis=0)
    rows = MLA_HEADS * t
    c_p = cp_ref[...].astype(BF16)
    c_n = _pad_rows(cn_ref[...], HEAD_W)
    kr_n = _pad_rows(krn_ref[...], HEAD_W)

    def chunk_mask(width, k0):
        qpos = past + lax.broadcasted_iota(jnp.int32, (rows, width), 0) % t
        kpos = k0 + lax.broadcasted_iota(jnp.int32, (rows, width), 1)
        return ((kpos // CHUNK) <= (qpos // CHUNK)) & (kpos < past + t)

    s_past = _dot_nt(q_abs, c_p) + _dot_nt(q_rope[:, :MLA_ROPE], krp_ref[...].astype(BF16))
    s_new = _dot_nt(q_abs, c_n) + _dot_nt(q_rope, kr_n)
    sp = jnp.where(chunk_mask(past, 0), s_past * scale, NEG_INF)
    sn = jnp.where(chunk_mask(HEAD_W, past), s_new * scale, NEG_INF)
    o_lat = _two_part_softmax(sp, sn, c_p, c_n).astype(BF16)
    for h in range(MLA_HEADS):
        w_uv = w_ref[:, (2 * h + 1) * HEAD_W:(2 * h + 2) * HEAD_W].astype(BF16)
        o_ref[:, h * HEAD_W:(h + 1) * HEAD_W] = _dot(o_lat[h * t:(h + 1) * t, :], w_uv).astype(o_ref.dtype)


def _mla_sample(qcat, lat_cache, kr_cache, ckv_new, kr_new128, w_ukv, merged, lidx, *, nb, t, past, row0):
    rb0 = row0 // t
    kv_lora = lat_cache.shape[3]
    blk = (_nbytes((past, kv_lora), F32) + _nbytes(w_ukv.shape[1:], F32)
           + 6 * _nbytes((MLA_HEADS * t, past), F32))
    return pl.pallas_call(
        functools.partial(_mla_sample_kernel, past=past, t=t),
        grid_spec=pltpu.PrefetchScalarGridSpec(
            num_scalar_prefetch=1, grid=(nb,),
            in_specs=[pl.BlockSpec((t, MLA_HEADS * MLA_QK_PAD), lambda b, l: (rb0 + b, 0)),
                      pl.BlockSpec((None, None, past, kv_lora), lambda b, l: (l[0], b, 0, 0)),
                      pl.BlockSpec((None, None, past, MLA_ROPE), lambda b, l: (l[0], b, 0, 0)),
                      pl.BlockSpec((t, kv_lora), lambda b, l: (rb0 + b, 0)),
                      pl.BlockSpec((t, HEAD_W), lambda b, l: (rb0 + b, 0)),
                      pl.BlockSpec((None,) + w_ukv.shape[1:], lambda b, l: (l[0], 0, 0)),
                      pl.BlockSpec(memory_space=pl.ANY)],
            out_specs=pl.BlockSpec((t, MLA_HEADS * HEAD_W), lambda b, l: (rb0 + b, 0))),
        out_shape=jax.ShapeDtypeStruct(merged.shape, BF16),
        input_output_aliases={7: 0},
        compiler_params=_params(blk), name="mla_sample",
    )(lidx, qcat, lat_cache, kr_cache, ckv_new, kr_new128, w_ukv, merged)


def _gate_merge_kernel(l_ref, xn_ref, oa_ref, ob_ref, oc_ref, wga_ref, wgb_ref, wgc_ref,
                       wa_ref, wb_ref, wc_ref, o_ref):
    del l_ref
    xn = xn_ref[...]

    def branch(wg_ref, mix_ref, w_ref):
        return jax.nn.sigmoid(_dot(xn, wg_ref[...])) * _dot(mix_ref[...], w_ref[...])

    merged = branch(wga_ref, oa_ref, wa_ref) + branch(wgb_ref, ob_ref, wb_ref) + branch(wgc_ref, oc_ref, wc_ref)
    o_ref[...] = merged.astype(o_ref.dtype)


def _gate_merge(xn, oa, ob, oc, w_gate, w_a, w_b, w_c, lidx):
    m, d = xn.shape
    tm = _tile(m, 832, V7X_BF16_SUBLANES)
    tn = _tile(d, 256, V7X_LANES)
    nj = d // tn

    def rows(width):
        return pl.BlockSpec((tm, width), lambda i, j, l: (i, 0))

    def gate(branch):
        return pl.BlockSpec((None, d, tn), lambda i, j, l: (l[0], 0, branch * nj + j))

    def proj(kdim):
        return pl.BlockSpec((None, kdim, tn), lambda i, j, l: (l[0], 0, j))

    wa, wb, wc = oa.shape[1], ob.shape[1], oc.shape[1]
    blk = (_nbytes((tm, d + wa + wb + wc), BF16) + _nbytes((3 * d + wa + wb + wc, tn), BF16)
           + 4 * _nbytes((tm, tn), F32))
    return pl.pallas_call(
        _gate_merge_kernel,
        grid_spec=pltpu.PrefetchScalarGridSpec(
            num_scalar_prefetch=1, grid=(m // tm, nj),
            in_specs=[rows(d), rows(wa), rows(wb), rows(wc), gate(0), gate(1), gate(2),
                      proj(wa), proj(wb), proj(wc)],
            out_specs=pl.BlockSpec((tm, tn), lambda i, j, l: (i, j))),
        out_shape=jax.ShapeDtypeStruct((m, d), BF16),
        compiler_params=_params(blk), name="gate_merge",
    )(lidx, xn, oa, ob, oc, w_gate, w_gate, w_gate, w_a, w_b, w_c)


def _swiglu_kernel(l_ref, x_ref, w1_ref, w3_ref, o_ref):
    del l_ref
    x = x_ref[...]
    o_ref[...] = (jax.nn.silu(_dot(x, w1_ref[...].astype(BF16)))
                  * _dot(x, w3_ref[...].astype(BF16))).astype(o_ref.dtype)


def _swiglu(xn, w1, w3, lidx):
    m, d = xn.shape
    f = w1.shape[2]
    tm = _tile(m, 1040, V7X_BF16_SUBLANES)
    tn = _tile(f, 256, V7X_LANES)
    wspec = pl.BlockSpec((None, d, tn), lambda i, j, l: (l[0], 0, j))
    blk = _nbytes((tm, d), BF16) + 3 * _nbytes((d, tn), w1.dtype) + 4 * _nbytes((tm, tn), F32)
    return pl.pallas_call(
        _swiglu_kernel,
        grid_spec=pltpu.PrefetchScalarGridSpec(
            num_scalar_prefetch=1, grid=(m // tm, f // tn),
            in_specs=[pl.BlockSpec((tm, d), lambda i, j, l: (i, 0)), wspec, wspec],
            out_specs=pl.BlockSpec((tm, tn), lambda i, j, l: (i, j))),
        out_shape=jax.ShapeDtypeStruct((m, f), BF16),
        compiler_params=_params(blk), name="swiglu",
    )(lidx, xn, w1, w3)


def _rope_tables(pos):
    inv_freq = ROPE_THETA ** (-jnp.arange(0, MLA_ROPE, 2, dtype=F32) / MLA_ROPE)
    ang = pos.astype(F32)[:, None] * inv_freq[None, :]
    c, s = jnp.cos(ang), jnp.sin(ang)
    one, zero = jnp.ones_like(c), jnp.zeros_like(c)
    cos128 = jnp.concatenate([c, c, c, c], axis=-1)
    sin128 = jnp.concatenate([-s, s, -s, s], axis=-1)
    cos256 = jnp.concatenate([one, one, one, one, c, c, one, one], axis=-1)
    sin256 = jnp.concatenate([zero, zero, zero, zero, -s, s, zero, zero], axis=-1)
    return cos128, sin128, cos256, sin256


def kernel(x_prompt, x_sample, cache_diff_k, cache_diff_v, cache_sb_k, cache_sb_v, cache_mla_latent,
           cache_mla_krope, attn_norm, w_in, diff_lambda, diff_subln, mla_q_norm, mla_w_uq, mla_kv_norm,
           mla_w_ukv, w_gate, w_branch_a, w_branch_b, w_branch_c, w_out, ffn_norm, ffn_w1, ffn_w3, ffn_w2,
           final_norm):
    nbp, tp, d = x_prompt.shape
    nbs, ts = x_sample.shape[:2]
    past = cache_diff_k.shape[3]
    depth = w_in.shape[0]
    mp, ms = nbp * tp, nbs * ts
    m = mp + ms

    h0 = jnp.concatenate([x_prompt.reshape(mp, d), x_sample.reshape(ms, d)], axis=0)
    pos = jnp.concatenate([jnp.tile(jnp.arange(tp, dtype=jnp.int32), nbp),
                           jnp.tile(past + jnp.arange(ts, dtype=jnp.int32), nbs)])
    cos128, sin128, cos256, sin256 = _rope_tables(pos)

    w_kr_pad = jnp.pad(w_in[:, :, COL_KR:], ((0, 0), (0, 0), (0, HEAD_W - MLA_ROPE))).astype(BF16)
    uq = mla_w_uq.astype(BF16).reshape(depth, MLA_Q_LORA, MLA_HEADS, HEAD_W + MLA_ROPE)
    w_uq_pad = jnp.pad(uq, ((0, 0), (0, 0), (0, 0), (0, MLA_QK_PAD - HEAD_W - MLA_ROPE))).reshape(
        depth, MLA_Q_LORA, MLA_HEADS * MLA_QK_PAD)
    w_gate_b, w_a, w_b, w_c = (_to_bf16(w) for w in (w_gate, w_branch_a, w_branch_b, w_branch_c))
    w2 = _to_bf16(ffn_w2)
    w_in_b = _to_bf16(w_in)

    attn_g = attn_norm.reshape(depth, 1, d)
    ffn_g = ffn_norm.reshape(depth, 1, d)
    gq = mla_q_norm.reshape(depth, 1, MLA_Q_LORA)
    gkv = mla_kv_norm.reshape(depth, 1, MLA_KV_LORA)
    subln = diff_subln.reshape(depth, 1, HEAD_W)
    lam_init = jnp.asarray([0.8 - 0.6 * math.exp(-0.3 * l) for l in range(depth)], F32)
    lam_init = jnp.broadcast_to(lam_init[:, None, None], (depth, 1, HEAD_W))

    def head_bufs(nb, t):
        return tuple(jnp.zeros((depth, nb, DIFF_HEADS, t, HEAD_W), F32) for _ in range(4))

    def layer(l, carry):
        h, p_bufs, s_bufs, lat_buf, kr_buf = carry
        lidx = jnp.reshape(l, (1,)).astype(jnp.int32)
        xn = _rmsnorm_rows(h, attn_g, lidx, BF16)
        qa_p, qb_p, *p_bufs = _in_proj_heads(xn, w_in_b, cos128, sin128, p_bufs, lidx, row0=0, nb=nbp, t=tp)
        qa_s, qb_s, *s_bufs = _in_proj_heads(xn, w_in_b, cos128, sin128, s_bufs, lidx, row0=mp, nb=nbs, t=ts)
        rest, = _section_proj(xn, w_in_b, lidx, [(COL_CQ, False, F32)], n_cols=IN_REST_W, row0=0, nb=1, t=m,
                              tm_cap=1040, tn=4 * HEAD_W, name="in_proj_mla")
        kr_buf, kr128 = _shared_rope_key(xn, w_kr_pad, cos128, sin128, kr_buf, lidx)
        cqn, ckv_b, lat_buf = _post_mla(rest, gq, gkv, lat_buf, lidx)
        qcat = _matmul(cqn, w_uq_pad, lidx, out_dtype=BF16, tm_cap=1040, tn_cap=1024,
                       rope=(cos256, sin256), name="mla_q_up")
        kv = _matmul(ckv_b, mla_w_ukv, lidx, out_dtype=BF16, tm_cap=1040, tn_cap=1024, name="mla_kv_up")
        oa = _diff_prompt(qa_p, p_bufs[0], p_bufs[1], diff_lambda, subln, lam_init, lidx, nb=nbp, t=tp, rows_total=m)
        oa = _diff_sample(qa_s, cache_diff_k, cache_diff_v, s_bufs[0], s_bufs[1], diff_lambda, subln,
                          lam_init, oa, lidx, nb=nbs, t=ts, past=past, row0=mp)
        ob = _sb_prompt(qb_p, p_bufs[2], p_bufs[3], lidx, nb=nbp, t=tp, rows_total=m)
        ob = _sb_sample(qb_s, cache_sb_k, cache_sb_v, s_bufs[2], s_bufs[3], ob, lidx, nb=nbs, t=ts, past=past,
                        row0=mp)
        oc = _mla_prompt(qcat, kv, kr128, nb=nbp, t=tp, rows_total=m)
        oc = _mla_sample(qcat, cache_mla_latent, cache_mla_krope, ckv_b, kr128, mla_w_ukv, oc, lidx, nb=nbs, t=ts,
                         past=past, row0=mp)
        merged = _gate_merge(xn, oa, ob, oc, w_gate_b, w_a, w_b, w_c, lidx)
        h = _matmul(merged, w_out, lidx, out_dtype=F32, tm_cap=1040, tn_cap=512, res=h, name="out_proj")
        xn2 = _rmsnorm_rows(h, ffn_g, lidx, BF16)
        hid = _swiglu(xn2, ffn_w1, ffn_w3, lidx)
        h = _matmul(hid, w2, lidx, out_dtype=F32, tm_cap=640, tn_cap=256, res=h, name="ffn_down")
        return h, tuple(p_bufs), tuple(s_bufs), lat_buf, kr_buf

    carry = (h0, head_bufs(nbp, tp), head_bufs(nbs, ts),
             jnp.zeros((depth, m, MLA_KV_LORA), F32), jnp.zeros((depth, m, MLA_ROPE), F32))
    h, p_bufs, s_bufs, lat_buf, kr_buf = lax.fori_loop(0, depth, layer, carry)

    final_g, l0 = final_norm.reshape(1, 1, d), jnp.zeros((1,), jnp.int32)
    y_prompt = _rmsnorm_rows(h, final_g, l0, F32, row0=0, rows=mp).reshape(nbp, tp, d)
    y_sample = _rmsnorm_rows(h, final_g, l0, F32, row0=mp, rows=ms).reshape(nbs, ts, d)
    p_lat = lat_buf[:, :mp].reshape(depth, nbp, tp, MLA_KV_LORA)
    s_lat = lat_buf[:, mp:].reshape(depth, nbs, ts, MLA_KV_LORA)
    p_kr = kr_buf[:, :mp].reshape(depth, nbp, tp, MLA_ROPE)
    s_kr = kr_buf[:, mp:].reshape(depth, nbs, ts, MLA_ROPE)
    return (y_prompt, y_sample, *p_bufs, p_lat, p_kr, *s_bufs, s_lat, s_kr)
```

```python
import functools
import math

import jax
import jax.numpy as jnp
from jax import lax
from jax.experimental import pallas as pl
from jax.experimental.pallas import tpu as pltpu

D_MODEL = 4096
BATCH = 2
SEQ = 4096
DEPTH = 4
DEC_BATCH = 8
DEC_SEQ = 16
PAST_LEN = 2048

CHUNK = 64
ROPE_THETA = 10000.0
NORM_EPS = 1e-6
NEG_INF = -1e30

DIFF_HEADS = 8
DIFF_DH = 64
SB_HEADS = 8
MLA_HEADS = 16
MLA_ROPE = 64
MLA_Q_LORA = 1024
MLA_KV_LORA = 512
HEAD_W = 128
MLA_QK_PAD = 2 * HEAD_W
N_BRANCHES = 3
COL_CQ, COL_KR = 6144, 7680
FFN_HIDDEN = -(-8 * D_MODEL // 768) * 256

F32 = jnp.float32
BF16 = jnp.bfloat16

V7X_LANES = 128
V7X_BF16_SUBLANES = 16
V7X_VMEM_LIMIT_CAP = 60 * 1024 * 1024


def _tile(n, cap, mult):
    best = None
    for t in range(mult, min(n, cap) + 1, mult):
        if n % t == 0:
            best = t
    if best is None:
        raise ValueError(f"no tile for {n} (cap {cap}, multiple of {mult})")
    return best


def _params(block_bytes):
    need = 2 * block_bytes + (8 << 20)
    return pltpu.CompilerParams(vmem_limit_bytes=int(min(max(need, 32 << 20), V7X_VMEM_LIMIT_CAP)))


def _nbytes(shape, dtype):
    return math.prod(shape) * jnp.dtype(dtype).itemsize


def _dot(a, b):
    return jnp.dot(a, b, preferred_element_type=F32)


def _dot_nt(a, b):
    return lax.dot_general(a, b, (((1,), (1,)), ((), ())), preferred_element_type=F32)


def _rms(x, g):
    return x * lax.rsqrt(jnp.mean(x * x, axis=-1, keepdims=True) + NORM_EPS) * g


def _rope_lanes(x, cos, sin):
    lane = lax.broadcasted_iota(jnp.int32, x.shape, 1)
    first_half = (lane % 64) < 32
    partner = jnp.where(first_half, pltpu.roll(x, x.shape[1] - 32, 1), pltpu.roll(x, 32, 1))
    return x * cos + partner * sin


def _rmsnorm_kernel(l_ref, x_ref, g_ref, o_ref):
    del l_ref
    o_ref[...] = _rms(x_ref[...], g_ref[...]).astype(o_ref.dtype)


def _rmsnorm_rows(x, g_stack, lidx, out_dtype, row0=0, rows=None):
    d = x.shape[1]
    m = x.shape[0] if rows is None else rows
    tr = _tile(math.gcd(m, row0) if row0 else m, 320, V7X_BF16_SUBLANES)
    rb0 = row0 // tr
    blk = _nbytes((tr, d), F32) + _nbytes((tr, d), out_dtype)
    return pl.pallas_call(
        _rmsnorm_kernel,
        grid_spec=pltpu.PrefetchScalarGridSpec(
            num_scalar_prefetch=1, grid=(m // tr,),
            in_specs=[pl.BlockSpec((tr, d), lambda i, l: (rb0 + i, 0)),
                      pl.BlockSpec((None, 1, d), lambda i, l: (l[0], 0, 0))],
            out_specs=pl.BlockSpec((tr, d), lambda i, l: (i, 0))),
        out_shape=jax.ShapeDtypeStruct((m, d), out_dtype),
        compiler_params=_params(blk), name="rmsnorm_rows",
    )(lidx, x, g_stack)


def _cast_kernel(x_ref, o_ref):
    o_ref[...] = x_ref[...].astype(o_ref.dtype)


def _to_bf16(w):
    depth, kdim, n = w.shape
    tr =_tile(kdim, max(V7X_BF16_SUBLANES, (8 << 20) // (4 * n)), V7X_BF16_SUBLANES)
    return pl.pallas_call(
        _cast_kernel, grid=(depth, kdim // tr),
        in_specs=[pl.BlockSpec((None, tr, n), lambda l, i: (l, i, 0))],
        out_specs=pl.BlockSpec((None, tr, n), lambda l, i: (l, i, 0)),
        out_shape=jax.ShapeDtypeStruct((depth, kdim, n), BF16),
        compiler_params=_params(_nbytes((tr, n), F32) + _nbytes((tr, n), BF16)), name="to_bf16",
    )(w)


def _mm_kernel(l_ref, x_ref, w_ref, *rest, nk, has_res, rope):
    del l_ref
    rest = list(rest)
    r_ref = rest.pop(0) if has_res else None
    cos_ref, sin_ref = (rest.pop(0), rest.pop(0)) if rope else (None, None)
    o_ref = rest.pop(0)
    part = _dot(x_ref[...].astype(BF16), w_ref[...].astype(BF16))

    def finish(acc):
        if has_res:
            acc = acc + r_ref[...]
        if rope:
            cos, sin = cos_ref[...], sin_ref[...]
            gw = cos.shape[1]
            for g in range(acc.shape[1] // gw):
                seg = acc[:, g * gw:(g + 1) * gw]
                o_ref[:, g * gw:(g + 1) * gw] = _rope_lanes(seg, cos, sin).astype(o_ref.dtype)
        else:
            o_ref[...] = acc.astype(o_ref.dtype)

    if nk == 1:
        finish(part)
    else:
        acc_ref, = rest
        k = pl.program_id(2)

        @pl.when(k == 0)
        def _():
            acc_ref[...] = part

        @pl.when(k > 0)
        def _():
            acc_ref[...] += part

        @pl.when(k == nk - 1)
        def _():
            finish(acc_ref[...])


def _matmul(x, w_stack, lidx, *, out_dtype, tm_cap, tn_cap, tk=None, res=None, rope=None,
            x_stacked=False, name="matmul"):
    m, kdim = x.shape[-2:]
    n = w_stack.shape[2]
    tm = _tile(m, tm_cap, V7X_BF16_SUBLANES)
    tn = _tile(n, tn_cap, V7X_LANES) if n % V7X_LANES == 0 else n
    tk = kdim if tk is None else tk
    nk = kdim // tk
    if x_stacked:
        x_spec = pl.BlockSpec((None, tm, tk), lambda i, j, k, l: (l[0], i, k))
    else:
        x_spec = pl.BlockSpec((tm, tk), lambda i, j, k, l: (i, k))
    in_specs = [x_spec, pl.BlockSpec((None, tk, tn), lambda i, j, k, l: (l[0], k, j))]
    args = [x, w_stack]
    blk = _nbytes((tm, tk), x.dtype) + _nbytes((tk, tn), w_stack.dtype) + _nbytes((tm, tn), F32)
    if res is not None:
        in_specs.append(pl.BlockSpec((tm, tn), lambda i, j, k, l: (i, j)))
        args.append(res)
        blk += _nbytes((tm, tn), F32)
    if rope is not None:
        gw = rope[0].shape[1]
        assert tn % gw == 0
        in_specs += [pl.BlockSpec((tm, gw), lambda i, j, k, l: (i, 0))] * 2
        args += list(rope)
        blk += 2 * _nbytes((tm, gw), F32)
    return pl.pallas_call(
        functools.partial(_mm_kernel, nk=nk, has_res=res is not None, rope=rope is not None),
        grid_spec=pltpu.PrefetchScalarGridSpec(
            num_scalar_prefetch=1, grid=(m // tm, n // tn, nk),
            in_specs=in_specs,
            out_specs=pl.BlockSpec((tm, tn), lambda i, j, k, l: (i, j)),
            scratch_shapes=[pltpu.VMEM((tm, tn), F32)] if nk > 1 else []),
        out_shape=jax.ShapeDtypeStruct((m, n), out_dtype),
        compiler_params=_params(blk), name=name,
    )(lidx, *args)


def _kr_kernel(l_ref, xn_ref, w_ref, cos_ref, sin_ref, kr_in, kr_o, kr128_o):
    del l_ref, kr_in
    r = _rope_lanes(_dot_nt(xn_ref[...], w_ref[...]), cos_ref[...], sin_ref[...])
    kr_o[...] = r[:, :MLA_ROPE]
    kr128_o[...] = r.astype(BF16)


def _shared_rope_key(xn, w_kr_pad, cos, sin, kr_buf, lidx):
    m, d = xn.shape
    tm = _tile(m, 640, V7X_BF16_SUBLANES)
    blk = _nbytes((tm, d), BF16) + _nbytes((d, HEAD_W), BF16) + 4 * _nbytes((tm, HEAD_W), F32)
    return pl.pallas_call(
        _kr_kernel,
        grid_spec=pltpu.PrefetchScalarGridSpec(
            num_scalar_prefetch=1, grid=(m // tm,),
            in_specs=[pl.BlockSpec((tm, d), lambda i, l: (i, 0)),
                      pl.BlockSpec((None, HEAD_W, d), lambda i, l: (l[0], 0, 0)),
                      pl.BlockSpec((tm, HEAD_W), lambda i, l: (i, 0)),
                      pl.BlockSpec((tm, HEAD_W), lambda i, l: (i, 0)),
                      pl.BlockSpec(memory_space=pl.ANY)],
            out_specs=[pl.BlockSpec((None, tm, MLA_ROPE), lambda i, l: (l[0], i, 0)),
                       pl.BlockSpec((tm, HEAD_W), lambda i, l: (i, 0))]),
        out_shape=[jax.ShapeDtypeStruct(kr_buf.shape, F32), jax.ShapeDtypeStruct((m, HEAD_W), BF16)],
        input_output_aliases={5: 0},
        compiler_params=_params(blk), name="shared_rope_key",
    )(lidx, xn, w_kr_pad, cos, sin, kr_buf)


IN_SECTION_W = DIFF_HEADS * HEAD_W
IN_REST_W = COL_KR - COL_CQ
MIN_SUBTILE_ROWS = 256


def _sections_kernel(l_ref, x_ref, *refs, kinds, subtiles):
    del l_ref
    n = len(kinds)
    w_refs, rest = refs[:n], list(refs[n:])
    cos_ref, sin_ref = (rest.pop(0), rest.pop(0)) if any(rope for rope, _ in kinds) else (None, None)
    o_refs = rest[len(rest) - n:]
    sub = x_ref.shape[0] // subtiles
    for (rope, heads), w_ref, o_ref in zip(kinds, w_refs, o_refs):
        w = w_ref[...].astype(BF16)
        for r in range(subtiles):
            r0 = r * sub
            acc = _dot_nt(x_ref[r0:r0 + sub, :], w)
            for h in range(acc.shape[1] // HEAD_W):
                seg = acc[:, h * HEAD_W:(h + 1) * HEAD_W]
                if rope:
                    seg = _rope_lanes(seg, cos_ref[r0:r0 + sub, :], sin_ref[r0:r0 + sub, :])
                if heads is None:
                    o_ref[r0:r0 + sub, h * HEAD_W:(h + 1) * HEAD_W] = seg.astype(o_ref.dtype)
                elif heads[0] == 1:
                    o_ref[0, h, r0:r0 + sub, :] = seg
                else:
                    tt = heads[1]
                    for b in range(sub // tt):
                        o_ref[r0 // tt + b, h] = seg[b * tt:(b + 1) * tt, :]


def _section_proj(xn, w_in, lidx, sections, *, n_cols, row0, nb, t, tm_cap, tn, rope=None, name="in_proj"):
    d = xn.shape[1]
    rows = nb * t
    tm = _tile(math.gcd(rows, row0) if row0 else rows, tm_cap, V7X_BF16_SUBLANES)
    assert n_cols % tn == 0 and all(col0 % tn == 0 for col0, _, _ in sections)
    rb0 = row0 // tm
    subtiles = max(k for k in (1, 2, 4) if tm % (k * MIN_SUBTILE_ROWS) == 0 or k == 1)
    bpt, tt = (1, tm) if tm <= t else (tm // t, t)
    nt = t // tt

    def w_spec(col0):
        return pl.BlockSpec((None, tn, d), lambda i, j, l: (l[0], col0 // tn + j, 0))

    in_specs = [pl.BlockSpec((tm, d), lambda i, j, l: (rb0 + i, 0))] + [w_spec(col0) for col0, _, _ in sections]
    args = [xn] + [w_in] * len(sections)
    if any(use_rope for _, use_rope, _ in sections):
        in_specs += [pl.BlockSpec((tm, HEAD_W), lambda i, j, l: (rb0 + i, 0))] * 2
        args += list(rope)
    kinds, out_specs, out_shapes, aliases = [], [], [], {}
    for k, (_, use_rope, out) in enumerate(sections):
        if isinstance(out, jax.Array):
            assert (t % tm == 0) if bpt == 1 else (tm % t == 0 and (tm // subtiles) % t == 0)
            kinds.append((use_rope, (bpt, tt)))
            in_specs.append(pl.BlockSpec(memory_space=pl.ANY))
            args.append(out)
            aliases[len(args)] = k
            out_specs.append(pl.BlockSpec((None, bpt, tn // HEAD_W, tt, HEAD_W),
                                          lambda i, j, l: (l[0], i // nt, j, i % nt, 0)))
            out_shapes.append(jax.ShapeDtypeStruct(out.shape, F32))
        else:
            kinds.append((use_rope, None))
            out_specs.append(pl.BlockSpec((tm, tn), lambda i, j, l: (i, j)))
            out_shapes.append(jax.ShapeDtypeStruct((rows, n_cols), out))
    blk = (_nbytes((tm, d), BF16) + 2 * _nbytes((tm, HEAD_W), F32)
           + len(sections) * (_nbytes((d, tn), w_in.dtype) + 2 * _nbytes((tm, tn), F32)))
    return pl.pallas_call(
        functools.partial(_sections_kernel, kinds=tuple(kinds), subtiles=subtiles),
        grid_spec=pltpu.PrefetchScalarGridSpec(
            num_scalar_prefetch=1, grid=(rows // tm, n_cols // tn), in_specs=in_specs, out_specs=out_specs),
        out_shape=out_shapes, input_output_aliases=aliases,
        compiler_params=_params(blk), name=name,
    )(lidx, *args)


def _in_proj_heads(xn, w_in, cos, sin, bufs, lidx, *, row0, nb, t):
    common = dict(row0=row0, nb=nb, t=t, tm_cap=1024, tn=2 * HEAD_W, n_cols=IN_SECTION_W, rope=(cos, sin))
    sec = lambda k: k * IN_SECTION_W
    qa, dk, qb = _section_proj(xn, w_in, lidx, [(sec(0), True, BF16), (sec(1), True, bufs[0]),
                                                (sec(3), False, BF16)], **common)
    dv, sk, sv = _section_proj(xn, w_in, lidx, [(sec(2), False, bufs[1]), (sec(4), False, bufs[2]),
                                                (sec(5), False, bufs[3])], **common)
    return qa, qb, dk, dv, sk, sv


def _post_mla_kernel(l_ref, cq_ref, ckv_ref, gq_ref, gkv_ref, lat_in, cqn_o, ckvb_o, lat_o):
    del l_ref, lat_in
    cqn_o[...] = _rms(cq_ref[...], gq_ref[...]).astype(BF16)
    c = _rms(ckv_ref[...], gkv_ref[...])
    lat_o[...] = c
    ckvb_o[...] = c.astype(BF16)


def _post_mla(proj, gq, gkv, lat_buf, lidx):
    m = proj.shape[0]
    tt = _tile(m, 640, V7X_BF16_SUBLANES)
    blk = 3 * _nbytes((tt, MLA_Q_LORA + MLA_KV_LORA), F32)
    return pl.pallas_call(
        _post_mla_kernel,
        grid_spec=pltpu.PrefetchScalarGridSpec(
            num_scalar_prefetch=1, grid=(m // tt,),
            in_specs=[pl.BlockSpec((tt, MLA_Q_LORA), lambda i, l: (i, 0)),
                      pl.BlockSpec((tt, MLA_KV_LORA), lambda i, l: (i, MLA_Q_LORA // MLA_KV_LORA)),
                      pl.BlockSpec((None, 1, MLA_Q_LORA), lambda i, l: (l[0], 0, 0)),
                      pl.BlockSpec((None, 1, MLA_KV_LORA), lambda i, l: (l[0], 0, 0)),
                      pl.BlockSpec(memory_space=pl.ANY)],
            out_specs=[pl.BlockSpec((tt, MLA_Q_LORA), lambda i, l: (i, 0)),
                       pl.BlockSpec((tt, MLA_KV_LORA), lambda i, l: (i, 0)),
                       pl.BlockSpec((None, tt, MLA_KV_LORA), lambda i, l: (l[0], i, 0))]),
        out_shape=[jax.ShapeDtypeStruct((m, MLA_Q_LORA), BF16),
                   jax.ShapeDtypeStruct((m, MLA_KV_LORA), BF16),
                   jax.ShapeDtypeStruct(lat_buf.shape, F32)],
        input_output_aliases={5: 2},
        compiler_params=_params(blk), name="post_mla",
    )(lidx, proj, proj, gq, gkv, lat_buf)


LOG2E = math.log2(math.e)
SB_DEAD_LOG = -104.0


def _with_ones(v):
    return jnp.concatenate([v, jnp.ones(v.shape, v.dtype)], axis=1)


def _softmax_step(s2, v_ext, m_ref, acc_ref):
    m_prev = m_ref[...]
    m_new = jnp.maximum(m_prev, jnp.max(s2, axis=-1, keepdims=True))
    alpha = jnp.exp2(m_prev - m_new)
    lanes = m_prev.shape[1]
    p = jnp.concatenate([jnp.exp2(s2[:, c * lanes:(c + 1) * lanes] - m_new).astype(BF16)
                         for c in range(s2.shape[1] // lanes)], axis=1)
    acc_ref[...] = jnp.concatenate([alpha, alpha], axis=1) * acc_ref[...] + _dot(p, v_ext)
    m_ref[...] = m_new


def _softmax_init(m_ref, acc_ref):
    m_ref[...] = jnp.full(m_ref.shape, NEG_INF, F32)
    acc_ref[...] = jnp.zeros(acc_ref.shape, F32)


def _softmax_result(acc_ref):
    acc = acc_ref[...]
    return acc[:, :HEAD_W] / acc[:, HEAD_W:]


def _positions(shape, q0, k0):
    qpos = q0 + lax.broadcasted_iota(jnp.int32, shape, 0)
    kpos = k0 + lax.broadcasted_iota(jnp.int32, shape, 1)
    return qpos, kpos


def _chunk_mask(shape, q0, k0):
    qpos, kpos = _positions(shape, q0, k0)
    return (kpos // CHUNK) <= (qpos // CHUNK)


def _diff_lambda(lam_ref, li_ref):
    lv = lam_ref[...]
    lam_init = li_ref[:, 0:1]
    d1 = jnp.sum(lv[0:1, :] * lv[1:2, :], axis=-1, keepdims=True)
    d2 = jnp.sum(lv[2:3, :] * lv[3:4, :], axis=-1, keepdims=True)
    return jnp.exp(d1) - jnp.exp(d2) + lam_init, lam_init


def _split_components(q):
    lane = lax.broadcasted_iota(jnp.int32, q.shape, 1)
    zero = jnp.zeros_like(q)
    return jnp.where(lane < DIFF_DH, q, zero), jnp.where(lane >= DIFF_DH, q, zero)


def _softplus(z):
    return jnp.maximum(z, 0.0) + jnp.log1p(jnp.exp(-jnp.abs(z)))


def _upper_ones(n):
    r = lax.broadcasted_iota(jnp.int32, (n, n), 0)
    c = lax.broadcasted_iota(jnp.int32, (n, n), 1)
    return jnp.where(r > c, 1.0, 0.0).astype(BF16)


def _sb_block(z, mask, carry, v, tri):
    sp = _softplus(z)
    log_stay = -sp if mask is None else jnp.where(mask, -sp, 0.0)
    hi = log_stay.astype(BF16)
    lo = (log_stay - hi.astype(F32)).astype(BF16)
    between = _dot(hi, tri) + _dot(lo, tri) + carry
    a = jnp.exp(z - sp + between)
    if mask is not None:
        a = jnp.where(mask, a, 0.0)
    return _dot(a.astype(BF16), v), carry + jnp.sum(log_stay, axis=-1, keepdims=True)


def _attn_tiles(t):
    tq = _tile(t, 512, CHUNK)
    return tq, tq


def _diff_prompt_kernel(l_ref, q_ref, k_ref, v_ref, lam_ref, g_ref, li_ref, bias_ref, o_ref, m_ref, acc_ref, s_ref,
                        *, tq, tk, hp):
    del l_ref
    assert tq == tk
    qi = pl.program_id(2)
    scale2 = DIFF_DH ** -0.5 * LOG2E
    for c in range(2 * hp):
        _softmax_init(m_ref.at[c], acc_ref.at[c])

    def rows(j):
        return pl.ds(pl.multiple_of(j * tk, tk), tk)

    def store_logits(j, hh):
        kb = k_ref[hh, rows(j), :].astype(BF16)
        q1, q2 = _split_components(q_ref[:, hh * HEAD_W:(hh + 1) * HEAD_W])
        s_ref[2 * hh] = _dot_nt(q1, kb) * scale2
        s_ref[2 * hh + 1] = _dot_nt(q2, kb) * scale2

    def step(j, hh, bias):
        v_ext = _with_ones(v_ref[hh, rows(j), :].astype(BF16))
        for c in (2 * hh, 2 * hh + 1):
            s2 = s_ref[c] if bias is None else s_ref[c] + bias
            _softmax_step(s2, v_ext, m_ref.at[c], acc_ref.at[c])

    for hh in range(hp):
        store_logits(0, hh)

    def body(j, carry):
        for hh in range(hp):
            step(j, hh, None)
            store_logits(j + 1, hh)
        return carry

    lax.fori_loop(0, qi, body, 0)
    lam, lam_init = _diff_lambda(lam_ref, li_ref)
    for hh in range(hp):
        step(qi, hh, bias_ref[...])
        o = _softmax_result(acc_ref.at[2 * hh]) - lam * _softmax_result(acc_ref.at[2 * hh + 1])
        o_ref[:, hh * HEAD_W:(hh + 1) * HEAD_W] = (_rms(o, g_ref[...]) * (1.0 - lam_init)).astype(o_ref.dtype)


def _diag_bias(tq, tk):
    r = jnp.arange(tq, dtype=jnp.int32)[:, None] // CHUNK
    c = jnp.arange(tk, dtype=jnp.int32)[None, :] // CHUNK
    return jnp.where(c <= r, 0.0, NEG_INF).astype(F32)


def _diff_prompt(q, k_buf, v_buf, lam, g, li, lidx, *, nb, t, rows_total):
    tq, tk = _attn_tiles(t)
    nq = t // tq
    hp = 2
    kv_spec = pl.BlockSpec((None, None, hp, t, HEAD_W), lambda b, h, i, l: (l[0], b, h, 0, 0))
    blk = 2 * hp * _nbytes((t, HEAD_W), F32) + (1 + 10 * hp) * _nbytes((tq, tk), F32)
    return pl.pallas_call(
        functools.partial(_diff_prompt_kernel, tq=tq, tk=tk, hp=hp),
        grid_spec=pltpu.PrefetchScalarGridSpec(
            num_scalar_prefetch=1, grid=(nb, DIFF_HEADS // hp, nq),
            in_specs=[pl.BlockSpec((tq, hp * HEAD_W), lambda b, h, i, l: (b * nq + i, h)),
                      kv_spec, kv_spec,
                      pl.BlockSpec((None, 4, DIFF_DH), lambda b, h, i, l: (l[0], 0, 0)),
                      pl.BlockSpec((None, 1, HEAD_W), lambda b, h, i, l: (l[0], 0, 0)),
                      pl.BlockSpec((None, 1, HEAD_W), lambda b, h, i, l: (l[0], 0, 0)),
                      pl.BlockSpec((tq, tk), lambda b, h, i, l: (0, 0))],
            out_specs=pl.BlockSpec((tq, hp * HEAD_W), lambda b, h, i, l: (b * nq + i, h)),
            scratch_shapes=[pltpu.VMEM((2 * hp, tq, HEAD_W), F32), pltpu.VMEM((2 * hp, tq, 2 * HEAD_W), F32),
                            pltpu.VMEM((2 * hp, tq, tk), F32)]),
        out_shape=jax.ShapeDtypeStruct((rows_total, DIFF_HEADS * HEAD_W), BF16),
        compiler_params=_params(blk), name="diff_prompt",
    )(lidx, q, k_buf, v_buf, lam, g, li, _diag_bias(tq, tk))


def _sb_prompt_kernel(l_ref, q_ref, k_ref, v_ref, o_ref, c_ref, acc_ref, *, tq, tk, hp):
    del l_ref
    qi = pl.program_id(2)
    ratio = tq // tk
    scale = HEAD_W ** -0.5
    tri = _upper_ones(tk)
    c_ref[...] = jnp.zeros(c_ref.shape, F32)
    acc_ref[...] = jnp.zeros(acc_ref.shape, F32)

    def block(j, masked):
        start = pl.multiple_of(j * tk, tk)
        mask = None
        if masked:
            qpos, kpos = _positions((tq, tk), qi * tq, start)
            mask = kpos < qpos
        for hh in range(hp):
            kb = k_ref[hh, pl.ds(start, tk), :].astype(BF16)
            vb = v_ref[hh, pl.ds(start, tk), :].astype(BF16)
            z = _dot_nt(q_ref[:, hh * HEAD_W:(hh + 1) * HEAD_W], kb) * scale
            out, carry = _sb_block(z, mask, c_ref[hh], vb, tri)
            acc_ref[hh] += out
            c_ref[hh] = carry

    for u in reversed(range(ratio)):
        block(qi * ratio + u, True)

    def alive():
        return jnp.max(c_ref[...]) > SB_DEAD_LOG

    def cond(state):
        j, live = state
        return jnp.logical_and(j >= 0, live)

    def body(state):
        j, _ = state
        block(j, False)
        return j - 1, alive()

    lax.while_loop(cond, body, (qi * ratio - 1, alive()))
    for hh in range(hp):
        o_ref[:, hh * HEAD_W:(hh + 1) * HEAD_W] = acc_ref[hh].astype(o_ref.dtype)


def _sb_prompt(q, k_buf, v_buf, lidx, *, nb, t, rows_total):
    tq, _ = _attn_tiles(t)
    tk = _tile(tq, 256, CHUNK)
    nq = t // tq
    hp = 2
    kv_spec = pl.BlockSpec((None, None, hp, t, HEAD_W), lambda b, h, i, l: (l[0], b, h, 0, 0))
    blk = 2 * hp * _nbytes((t, HEAD_W), F32) + 16 * hp * _nbytes((tq, tk), F32)
    return pl.pallas_call(
        functools.partial(_sb_prompt_kernel, tq=tq, tk=tk, hp=hp),
        grid_spec=pltpu.PrefetchScalarGridSpec(
            num_scalar_prefetch=1, grid=(nb, SB_HEADS // hp, nq),
            in_specs=[pl.BlockSpec((tq, hp * HEAD_W), lambda b, h, i, l: (b * nq + i, h)), kv_spec, kv_spec],
            out_specs=pl.BlockSpec((tq, hp * HEAD_W), lambda b, h, i, l: (b * nq + i, h)),
            scratch_shapes=[pltpu.VMEM((hp, tq, 1), F32), pltpu.VMEM((hp, tq, HEAD_W), F32)]),
        out_shape=jax.ShapeDtypeStruct((rows_total, SB_HEADS * HEAD_W), BF16),
        compiler_params=_params(blk), name="sb_prompt",
    )(lidx, q, k_buf, v_buf)


def _mla_prompt_kernel(q_ref, kv_ref, kr_ref, bias_ref, o_ref, m_ref, acc_ref, s_ref, *, tq, tk, hp):
    assert tq == tk
    qi = pl.program_id(2)
    scale2 = (HEAD_W + MLA_ROPE) ** -0.5 * LOG2E
    for hh in range(hp):
        _softmax_init(m_ref.at[hh], acc_ref.at[hh])

    def rows(j):
        return pl.ds(pl.multiple_of(j * tk, tk), tk)

    def logits(j, hh):
        kcat = jnp.concatenate([kv_ref[rows(j), 2 * hh * HEAD_W:(2 * hh + 1) * HEAD_W], kr_ref[rows(j), :]], axis=1)
        return _dot_nt(q_ref[:, hh * MLA_QK_PAD:(hh + 1) * MLA_QK_PAD], kcat) * scale2

    def values(j, hh):
        return _with_ones(kv_ref[rows(j), (2 * hh + 1) * HEAD_W:(2 * hh + 2) * HEAD_W])

    for hh in range(hp):
        s_ref[hh] = logits(0, hh)

    def body(j, c):
        for hh in range(hp):
            _softmax_step(s_ref[hh], values(j, hh), m_ref.at[hh], acc_ref.at[hh])
            s_ref[hh] = logits(j + 1, hh)
        return c

    lax.fori_loop(0, qi, body, 0)
    for hh in range(hp):
        _softmax_step(s_ref[hh] + bias_ref[...], values(qi, hh), m_ref.at[hh], acc_ref.at[hh])
        o_ref[:, hh * HEAD_W:(hh + 1) * HEAD_W] = _softmax_result(acc_ref.at[hh]).astype(o_ref.dtype)


def _mla_prompt(qcat, kv, kr128, *, nb, t, rows_total):
    tq, tk = _attn_tiles(t)
    nq = t // tq
    hp = 4
    blk = (2 * hp + 1) * _nbytes((t, HEAD_W), BF16) + (1 + 6 * hp) * _nbytes((tq, tk), F32)
    return pl.pallas_call(
        functools.partial(_mla_prompt_kernel, tq=tq, tk=tk, hp=hp),
        grid=(nb, MLA_HEADS // hp, nq),
        in_specs=[pl.BlockSpec((tq, hp * MLA_QK_PAD), lambda b, h, i: (b * nq + i, h)),
                  pl.BlockSpec((t, 2 * hp * HEAD_W), lambda b, h, i: (b, h)),
                  pl.BlockSpec((t, HEAD_W), lambda b, h, i: (b, 0)),
                  pl.BlockSpec((tq, tk), lambda b, h, i: (0, 0))],
        out_specs=pl.BlockSpec((tq, hp * HEAD_W), lambda b, h, i: (b * nq + i, h)),
        scratch_shapes=[pltpu.VMEM((hp, tq, HEAD_W), F32), pltpu.VMEM((hp, tq, 2 * HEAD_W), F32),
                        pltpu.VMEM((hp, tq, tk), F32)],
        out_shape=jax.ShapeDtypeStruct((rows_total, MLA_HEADS * HEAD_W), BF16),
        compiler_params=_params(blk), name="mla_prompt",
    )(qcat, kv, kr128, _diag_bias(tq, tk))


def _pad_rows(x, rows):
    return jnp.concatenate([x, jnp.zeros((rows - x.shape[0], x.shape[1]), x.dtype)], axis=0)


def _two_part_softmax(sp, sn, vp, vn):
    m = jnp.maximum(jnp.max(sp, axis=-1, keepdims=True), jnp.max(sn, axis=-1, keepdims=True))
    pp, pn = jnp.exp(sp - m), jnp.exp(sn - m)
    denom = jnp.sum(pp, axis=-1, keepdims=True) + jnp.sum(pn, axis=-1, keepdims=True)
    return (_dot(pp.astype(BF16), vp) + _dot(pn.astype(BF16), vn)) / denom


def _diff_sample_kernel(l_ref, q_ref, kp_ref, vp_ref, kn_ref, vn_ref, lam_ref, g_ref, li_ref, prev_ref, o_ref,
                        *, past, t):
    del l_ref, prev_ref
    q12 = jnp.concatenate(_split_components(q_ref[...]), axis=0)
    scale = DIFF_DH ** -0.5
    kp, vp = kp_ref[...].astype(BF16), vp_ref[...].astype(BF16)
    kn = _pad_rows(kn_ref[...], HEAD_W).astype(BF16)
    vn = _pad_rows(vn_ref[...], HEAD_W).astype(BF16)

    def chunk_mask(width, k0):
        qpos = past + lax.broadcasted_iota(jnp.int32, (2 * t, width), 0) % t
        kpos = k0 + lax.broadcasted_iota(jnp.int32, (2 * t, width), 1)
        return ((kpos // CHUNK) <= (qpos // CHUNK)) & (kpos < past + t)

    sp = jnp.where(chunk_mask(past, 0), _dot_nt(q12, kp) * scale, NEG_INF)
    sn = jnp.where(chunk_mask(HEAD_W, past), _dot_nt(q12, kn) * scale, NEG_INF)
    o12 = _two_part_softmax(sp, sn, vp, vn)
    lam, lam_init = _diff_lambda(lam_ref, li_ref)
    o = o12[:t] - lam * o12[t:]
    o_ref[...] = (_rms(o, g_ref[...]) * (1.0 - lam_init)).astype(o_ref.dtype)


def _diff_sample(q, k_cache, v_cache, k_buf, v_buf, lam, g, li, merged, lidx, *, nb, t, past, row0):
    rb0 = row0 // t
    cache_spec = pl.BlockSpec((None, None, None, past, HEAD_W), lambda b, h, l: (l[0], b, h, 0, 0))
    new_spec = pl.BlockSpec((None, None, None, t, HEAD_W), lambda b, h, l: (l[0], b, h, 0, 0))
    blk = 2 * _nbytes((past, HEAD_W), F32) + 8 * _nbytes((t, past), F32)
    return pl.pallas_call(
        functools.partial(_diff_sample_kernel, past=past, t=t),
        grid_spec=pltpu.PrefetchScalarGridSpec(
            num_scalar_prefetch=1, grid=(nb, DIFF_HEADS),
            in_specs=[pl.BlockSpec((t, HEAD_W), lambda b, h, l: (b, h)),
                      cache_spec, cache_spec, new_spec, new_spec,
                      pl.BlockSpec((None, 4, DIFF_DH), lambda b, h, l: (l[0], 0, 0)),
                      pl.BlockSpec((None, 1, HEAD_W), lambda b, h, l: (l[0], 0, 0)),
                      pl.BlockSpec((None, 1, HEAD_W), lambda b, h, l: (l[0], 0, 0)),
                      pl.BlockSpec(memory_space=pl.ANY)],
            out_specs=pl.BlockSpec((t, HEAD_W), lambda b, h, l: (rb0 + b, h))),
        out_shape=jax.ShapeDtypeStruct(merged.shape, BF16),
        input_output_aliases={9: 0},
        compiler_params=_params(blk), name="diff_sample",
    )(lidx, q, k_cache, v_cache, k_buf, v_buf, lam, g, li, merged)


def _sb_sample_kernel(l_ref, q_ref, kp_ref, vp_ref, kn_ref, vn_ref, prev_ref, o_ref, *, past, t, cw):
    del l_ref, prev_ref
    q = q_ref[...]
    scale = HEAD_W ** -0.5
    nc = past // cw
    kn = _pad_rows(kn_ref[...], cw).astype(BF16)
    vn = _pad_rows(vn_ref[...], cw).astype(BF16)
    z_past = _dot_nt(q, kp_ref[...].astype(BF16)) * scale
    z = [z_past[:, c * cw:(c + 1) * cw] for c in range(nc)] + [_dot_nt(q, kn) * scale]
    masks = []
    for c in range(nc + 1):
        qpos, kpos = _positions((t, cw), past, c * cw)
        masks.append((kpos < qpos) & (kpos < past + t))
    sp = [_softplus(zc) for zc in z]
    log_stay = jnp.concatenate([jnp.where(m, -s, 0.0) for m, s in zip(masks, sp)], axis=0)
    hi = log_stay.astype(BF16)
    lo = (log_stay - hi.astype(F32)).astype(BF16)
    cum = _dot(jnp.concatenate([hi, lo], axis=0), _upper_ones(cw))
    rows = (nc + 1) * t
    cum = cum[:rows] + cum[rows:]
    carry = jnp.zeros((t, 1), F32)
    weights = [None] * (nc + 1)
    for c in reversed(range(nc + 1)):
        logw = z[c] - sp[c] + cum[c * t:(c + 1) * t] + carry
        weights[c] = jnp.where(masks[c], jnp.exp(logw), 0.0).astype(BF16)
        carry = carry + jnp.sum(log_stay[c * t:(c + 1) * t], axis=-1, keepdims=True)
    acc = _dot(jnp.concatenate(weights[:nc], axis=1), vp_ref[...].astype(BF16)) + _dot(weights[nc], vn)
    o_ref[...] = acc.astype(o_ref.dtype)


def _sb_sample(q, k_cache, v_cache, k_buf, v_buf, merged, lidx, *, nb, t, past, row0):
    rb0 = row0 // t
    cw = _tile(past, 256, V7X_LANES)
    cache_spec = pl.BlockSpec((None, None, None, past, HEAD_W), lambda b, h, l: (l[0], b, h, 0, 0))
    new_spec = pl.BlockSpec((None, None, None, t, HEAD_W), lambda b, h, l: (l[0], b, h, 0, 0))
    blk = 2 * _nbytes((past, HEAD_W), F32) + 8 * _nbytes((t, past), F32)
    return pl.pallas_call(
        functools.partial(_sb_sample_kernel, past=past, t=t, cw=cw),
        grid_spec=pltpu.PrefetchScalarGridSpec(
            num_scalar_prefetch=1, grid=(nb, SB_HEADS),
            in_specs=[pl.BlockSpec((t, HEAD_W), lambda b, h, l: (b, h)),
                      cache_spec, cache_spec, new_spec, new_spec, pl.BlockSpec(memory_space=pl.ANY)],
            out_specs=pl.BlockSpec((t, HEAD_W), lambda b, h, l: (rb0 + b, h))),
        out_shape=jax.ShapeDtypeStruct(merged.shape, BF16),
        input_output_aliases={6: 0},
        compiler_params=_params(blk), name="sb_sample",
    )(lidx, q, k_cache, v_cache, k_buf, v_buf, merged)


def _mla_sample_kernel(l_ref, q_ref, cp_ref, krp_ref, cn_ref, krn_ref, w_ref, prev_ref, o_ref, *, past, t):
    del l_ref, prev_ref
    scale = (HEAD_W + MLA_ROPE) ** -0.5
    q = q_ref[...]
    q_abs, q_rope = [], []
    for h in range(MLA_HEADS):
        w_uk = w_ref[:, 2 * h * HEAD_W:(2 * h + 1) * HEAD_W].astype(BF16)
        q_abs.append(_dot_nt(q[:, h * MLA_QK_PAD:h * MLA_QK_PAD + HEAD_W], w_uk).astype(BF16))
        q_rope.append(q[:, h * MLA_QK_PAD + HEAD_W:(h + 1) * MLA_QK_PAD])
    q_abs = jnp.concatenate(q_abs, axis=0)
    q_rope = jnp.concatenate(q_rope, axis=0)
    rows = MLA_HEADS * t
    c_p = cp_ref[...].astype(BF16)
    c_n = _pad_rows(cn_ref[...], HEAD_W)
    kr_n = _pad_rows(krn_ref[...], HEAD_W)

    def chunk_mask(width, k0):
        qpos = past + lax.broadcasted_iota(jnp.int32, (rows, width), 0) % t
        kpos = k0 + lax.broadcasted_iota(jnp.int32, (rows, width), 1)
        return ((kpos // CHUNK) <= (qpos // CHUNK)) & (kpos < past + t)

    s_past = _dot_nt(q_abs, c_p) + _dot_nt(q_rope[:, :MLA_ROPE], krp_ref[...].astype(BF16))
    s_new = _dot_nt(q_abs, c_n) + _dot_nt(q_rope, kr_n)
    sp = jnp.where(chunk_mask(past, 0), s_past * scale, NEG_INF)
    sn = jnp.where(chunk_mask(HEAD_W, past), s_new * scale, NEG_INF)
    o_lat = _two_part_softmax(sp, sn, c_p, c_n).astype(BF16)
    for h in range(MLA_HEADS):
        w_uv = w_ref[:, (2 * h + 1) * HEAD_W:(2 * h + 2) * HEAD_W].astype(BF16)
        o_ref[:, h * HEAD_W:(h + 1) * HEAD_W] = _dot(o_lat[h * t:(h + 1) * t, :], w_uv).astype(o_ref.dtype)


def _mla_sample(qcat, lat_cache, kr_cache, ckv_new, kr_new128, w_ukv, merged, lidx, *, nb, t, past, row0):
    rb0 = row0 // t
    kv_lora = lat_cache.shape[3]
    blk = (_nbytes((past, kv_lora), F32) + _nbytes(w_ukv.shape[1:], F32)
           + 6 * _nbytes((MLA_HEADS * t, past), F32))
    return pl.pallas_call(
        functools.partial(_mla_sample_kernel, past=past, t=t),
        grid_spec=pltpu.PrefetchScalarGridSpec(
            num_scalar_prefetch=1, grid=(nb,),
            in_specs=[pl.BlockSpec((t, MLA_HEADS * MLA_QK_PAD), lambda b, l: (rb0 + b, 0)),
                      pl.BlockSpec((None, None, past, kv_lora), lambda b, l: (l[0], b, 0, 0)),
                      pl.BlockSpec((None, None, past, MLA_ROPE), lambda b, l: (l[0], b, 0, 0)),
                      pl.BlockSpec((t, kv_lora), lambda b, l: (rb0 + b, 0)),
                      pl.BlockSpec((t, HEAD_W), lambda b, l: (rb0 + b, 0)),
                      pl.BlockSpec((None,) + w_ukv.shape[1:], lambda b, l: (l[0], 0, 0)),
                      pl.BlockSpec(memory_space=pl.ANY)],
            out_specs=pl.BlockSpec((t, MLA_HEADS * HEAD_W), lambda b, l: (rb0 + b, 0))),
        out_shape=jax.ShapeDtypeStruct(merged.shape, BF16),
        input_output_aliases={7: 0},
        compiler_params=_params(blk), name="mla_sample",
    )(lidx, qcat, lat_cache, kr_cache, ckv_new, kr_new128, w_ukv, merged)


def _gate_merge_kernel(l_ref, xn_ref, oa_ref, ob_ref, oc_ref, wga_ref, wgb_ref, wgc_ref,
                       wa_ref, wb_ref, wc_ref, o_ref):
    del l_ref
    xn = xn_ref[...]

    def branch(wg_ref, mix_ref, w_ref):
        return jax.nn.sigmoid(_dot(xn, wg_ref[...])) * _dot(mix_ref[...], w_ref[...])

    merged = branch(wga_ref, oa_ref, wa_ref) + branch(wgb_ref, ob_ref, wb_ref) + branch(wgc_ref, oc_ref, wc_ref)
    o_ref[...] = merged.astype(o_ref.dtype)


def _gate_merge(xn, oa, ob, oc, w_gate, w_a, w_b, w_c, lidx):
    m, d = xn.shape
    tm = _tile(m, 832, V7X_BF16_SUBLANES)
    tn = _tile(d, 256, V7X_LANES)
    nj = d // tn

    def rows(width):
        return pl.BlockSpec((tm, width), lambda i, j, l: (i, 0))

    def gate(branch):
        return pl.BlockSpec((None, d, tn), lambda i, j, l: (l[0], 0, branch * nj + j))

    def proj(kdim):
        return pl.BlockSpec((None, kdim, tn), lambda i, j, l: (l[0], 0, j))

    wa, wb, wc = oa.shape[1], ob.shape[1], oc.shape[1]
    blk = (_nbytes((tm, d + wa + wb + wc), BF16) + _nbytes((3 * d + wa + wb + wc, tn), BF16)
           + 4 * _nbytes((tm, tn), F32))
    return pl.pallas_call(
        _gate_merge_kernel,
        grid_spec=pltpu.PrefetchScalarGridSpec(
            num_scalar_prefetch=1, grid=(m // tm, nj),
            in_specs=[rows(d), rows(wa), rows(wb), rows(wc), gate(0), gate(1), gate(2),
                      proj(wa), proj(wb), proj(wc)],
            out_specs=pl.BlockSpec((tm, tn), lambda i, j, l: (i, j))),
        out_shape=jax.ShapeDtypeStruct((m, d), BF16),
        compiler_params=_params(blk), name="gate_merge",
    )(lidx, xn, oa, ob, oc, w_gate, w_gate, w_gate, w_a, w_b, w_c)


def _swiglu_kernel(l_ref, x_ref, w1_ref, w3_ref, o_ref):
    del l_ref
    x = x_ref[...]
    o_ref[...] = (jax.nn.silu(_dot(x, w1_ref[...].astype(BF16)))
                  * _dot(x, w3_ref[...].astype(BF16))).astype(o_ref.dtype)


def _swiglu(xn, w1, w3, lidx):
    m, d = xn.shape
    f = w1.shape[2]
    tm = _tile(m, 1040, V7X_BF16_SUBLANES)
    tn = _tile(f, 256, V7X_LANES)
    wspec = pl.BlockSpec((None, d, tn), lambda i, j, l: (l[0], 0, j))
    blk = _nbytes((tm, d), BF16) + 3 * _nbytes((d, tn), w1.dtype) + 4 * _nbytes((tm, tn), F32)
    return pl.pallas_call(
        _swiglu_kernel,
        grid_spec=pltpu.PrefetchScalarGridSpec(
            num_scalar_prefetch=1, grid=(m // tm, f // tn),
            in_specs=[pl.BlockSpec((tm, d), lambda i, j, l: (i, 0)), wspec, wspec],
            out_specs=pl.BlockSpec((tm, tn), lambda i, j, l: (i, j))),
        out_shape=jax.ShapeDtypeStruct((m, f), BF16),
        compiler_params=_params(blk), name="swiglu",
    )(lidx, xn, w1, w3)


def _rope_tables(pos):
    inv_freq = ROPE_THETA ** (-jnp.arange(0, MLA_ROPE, 2, dtype=F32) / MLA_ROPE)
    ang = pos.astype(F32)[:, None] * inv_freq[None, :]
    c, s = jnp.cos(ang), jnp.sin(ang)
    one, zero = jnp.ones_like(c), jnp.zeros_like(c)
    cos128 = jnp.concatenate([c, c, c, c], axis=-1)
    sin128 = jnp.concatenate([-s, s, -s, s], axis=-1)
    cos256 = jnp.concatenate([one, one, one, one, c, c, one, one], axis=-1)
    sin256 = jnp.concatenate([zero, zero, zero, zero, -s, s, zero, zero], axis=-1)
    return cos128, sin128, cos256, sin256


def kernel(x_prompt, x_sample, cache_diff_k, cache_diff_v, cache_sb_k, cache_sb_v, cache_mla_latent,
           cache_mla_krope, attn_norm, w_in, diff_lambda, diff_subln, mla_q_norm, mla_w_uq, mla_kv_norm,
           mla_w_ukv, w_gate, w_branch_a, w_branch_b, w_branch_c, w_out, ffn_norm, ffn_w1, ffn_w3, ffn_w2,
           final_norm):
    nbp, tp, d = x_prompt.shape
    nbs, ts = x_sample.shape[:2]
    past = cache_diff_k.shape[3]
    depth = w_in.shape[0]
    mp, ms = nbp * tp, nbs * ts
    m = mp + ms

    h0 = jnp.concatenate([x_prompt.reshape(mp, d), x_sample.reshape(ms, d)], axis=0)
    pos = jnp.concatenate([jnp.tile(jnp.arange(tp, dtype=jnp.int32), nbp),
                           jnp.tile(past + jnp.arange(ts, dtype=jnp.int32), nbs)])
    cos128, sin128, cos256, sin256 = _rope_tables(pos)

    w_in_b = _to_bf16(jnp.swapaxes(w_in, 1, 2))
    w_kr_pad = jnp.pad(w_in_b[:, COL_KR:, :], ((0, 0), (0, HEAD_W - MLA_ROPE), (0, 0)))
    uq = mla_w_uq.astype(BF16).reshape(depth, MLA_Q_LORA, MLA_HEADS, HEAD_W + MLA_ROPE)
    w_uq_pad = jnp.pad(uq, ((0, 0), (0, 0), (0, 0), (0, MLA_QK_PAD - HEAD_W - MLA_ROPE))).reshape(
        depth, MLA_Q_LORA, MLA_HEADS * MLA_QK_PAD)
    w_gate_b, w_a, w_b, w_c = (_to_bf16(w) for w in (w_gate, w_branch_a, w_branch_b, w_branch_c))
    w2 = _to_bf16(ffn_w2)

    attn_g = attn_norm.reshape(depth, 1, d)
    ffn_g = ffn_norm.reshape(depth, 1, d)
    gq = mla_q_norm.reshape(depth, 1, MLA_Q_LORA)
    gkv = mla_kv_norm.reshape(depth, 1, MLA_KV_LORA)
    subln = diff_subln.reshape(depth, 1, HEAD_W)
    lam_init = jnp.asarray([0.8 - 0.6 * math.exp(-0.3 * l) for l in range(depth)], F32)
    lam_init = jnp.broadcast_to(lam_init[:, None, None], (depth, 1, HEAD_W))

    def head_bufs(nb, t):
        return tuple(jnp.zeros((depth, nb, DIFF_HEADS, t, HEAD_W), F32) for _ in range(4))

    def layer(l, carry):
        h, p_bufs, s_bufs, lat_buf, kr_buf = carry
        lidx = jnp.reshape(l, (1,)).astype(jnp.int32)
        xn = _rmsnorm_rows(h, attn_g, lidx, BF16)
        qa_p, qb_p, *p_bufs = _in_proj_heads(xn, w_in_b, cos128, sin128, p_bufs, lidx, row0=0, nb=nbp, t=tp)
        qa_s, qb_s, *s_bufs = _in_proj_heads(xn, w_in_b, cos128, sin128, s_bufs, lidx, row0=mp, nb=nbs, t=ts)
        rest, = _section_proj(xn, w_in_b, lidx, [(COL_CQ, False, F32)], n_cols=IN_REST_W, row0=0, nb=1, t=m,
                              tm_cap=1040, tn=4 * HEAD_W, name="in_proj_mla")
        kr_buf, kr128 = _shared_rope_key(xn, w_kr_pad, cos128, sin128, kr_buf, lidx)
        cqn, ckv_b, lat_buf = _post_mla(rest, gq, gkv, lat_buf, lidx)
        qcat = _matmul(cqn, w_uq_pad, lidx, out_dtype=BF16, tm_cap=1040, tn_cap=1024,
                       rope=(cos256, sin256), name="mla_q_up")
        kv = _matmul(ckv_b, mla_w_ukv, lidx, out_dtype=BF16, tm_cap=1040, tn_cap=1024, name="mla_kv_up")
        oa = _diff_prompt(qa_p, p_bufs[0], p_bufs[1], diff_lambda, subln, lam_init, lidx, nb=nbp, t=tp, rows_total=m)
        oa = _diff_sample(qa_s, cache_diff_k, cache_diff_v, s_bufs[0], s_bufs[1], diff_lambda, subln,
                          lam_init, oa, lidx, nb=nbs, t=ts, past=past, row0=mp)
        ob = _sb_prompt(qb_p, p_bufs[2], p_bufs[3], lidx, nb=nbp, t=tp, rows_total=m)
        ob = _sb_sample(qb_s, cache_sb_k, cache_sb_v, s_bufs[2], s_bufs[3], ob, lidx, nb=nbs, t=ts, past=past,
                        row0=mp)
        oc = _mla_prompt(qcat, kv, kr128, nb=nbp, t=tp, rows_total=m)
        oc = _mla_sample(qcat, cache_mla_latent, cache_mla_krope, ckv_b, kr128, mla_w_ukv, oc, lidx, nb=nbs, t=ts,
                         past=past, row0=mp)
        merged = _gate_merge(xn, oa, ob, oc, w_gate_b, w_a, w_b, w_c, lidx)
        h = _matmul(merged, w_out, lidx, out_dtype=F32, tm_cap=1040, tn_cap=512, res=h, name="out_proj")
        xn2 = _rmsnorm_rows(h, ffn_g, lidx, BF16)
        hid = _swiglu(xn2, ffn_w1, ffn_w3, lidx)
        h = _matmul(hid, w2, lidx, out_dtype=F32, tm_cap=640, tn_cap=256, res=h, name="ffn_down")
        return h, tuple(p_bufs), tuple(s_bufs), lat_buf, kr_buf

    carry = (h0, head_bufs(nbp, tp), head_bufs(nbs, ts),
             jnp.zeros((depth, m, MLA_KV_LORA), F32), jnp.zeros((depth, m, MLA_ROPE), F32))
    h, p_bufs, s_bufs, lat_buf, kr_buf = lax.fori_loop(0, depth, layer, carry)

    final_g, l0 = final_norm.reshape(1, 1, d), jnp.zeros((1,), jnp.int32)
    y_prompt = _rmsnorm_rows(h, final_g, l0, F32, row0=0, rows=mp).reshape(nbp, tp, d)
    y_sample = _rmsnorm_rows(h, final_g, l0, F32, row0=mp, rows=ms).reshape(nbs, ts, d)
    p_lat = lat_buf[:, :mp].reshape(depth, nbp, tp, MLA_KV_LORA)
    s_lat = lat_buf[:, mp:].reshape(depth, nbs, ts, MLA_KV_LORA)
    p_kr = kr_buf[:, :mp].reshape(depth, nbp, tp, MLA_ROPE)
    s_kr = kr_buf[:, mp:].reshape(depth, nbs, ts, MLA_ROPE)
    return (y_prompt, y_sample, *p_bufs, p_lat, p_kr, *s_bufs, s_lat, s_kr)
```

```python
import functools
import math

import jax
import jax.numpy as jnp
from jax import lax
from jax.experimental import pallas as pl
from jax.experimental.pallas import tpu as pltpu

D_MODEL = 4096
BATCH = 2
SEQ = 4096
DEPTH = 4
DEC_BATCH = 8
DEC_SEQ = 16
PAST_LEN = 2048

CHUNK = 64
ROPE_THETA = 10000.0
NORM_EPS = 1e-6
NEG_INF = -1e30

DIFF_HEADS = 8
DIFF_DH = 64
SB_HEADS = 8
MLA_HEADS = 16
MLA_ROPE = 64
MLA_Q_LORA = 1024
MLA_KV_LORA = 512
HEAD_W = 128
MLA_QK_PAD = 2 * HEAD_W
N_BRANCHES = 3
COL_CQ, COL_KR = 6144, 7680
FFN_HIDDEN = -(-8 * D_MODEL // 768) * 256

F32 = jnp.float32
BF16 = jnp.bfloat16

V7X_LANES = 128
V7X_BF16_SUBLANES = 16
V7X_VMEM_LIMIT_CAP = 60 * 1024 * 1024


def _tile(n, cap, mult):
    best = None
    for t in range(mult, min(n, cap) + 1, mult):
        if n % t == 0:
            best = t
    if best is None:
        raise ValueError(f"no tile for {n} (cap {cap}, multiple of {mult})")
    return best


def _params(block_bytes):
    need = 2 * block_bytes + (8 << 20)
    return pltpu.CompilerParams(vmem_limit_bytes=int(min(max(need, 32 << 20), V7X_VMEM_LIMIT_CAP)))


def _nbytes(shape, dtype):
    return math.prod(shape) * jnp.dtype(dtype).itemsize


def _dot(a, b):
    return jnp.dot(a, b, preferred_element_type=F32)


def _dot_nt(a, b):
    return lax.dot_general(a, b, (((1,), (1,)), ((), ())), preferred_element_type=F32)


def _rms(x, g):
    return x * lax.rsqrt(jnp.mean(x * x, axis=-1, keepdims=True) + NORM_EPS) * g


def _rope_lanes(x, cos, sin):
    lane = lax.broadcasted_iota(jnp.int32, x.shape, 1)
    first_half = (lane % 64) < 32
    partner = jnp.where(first_half, pltpu.roll(x, x.shape[1] - 32, 1), pltpu.roll(x, 32, 1))
    return x * cos + partner * sin


def _rmsnorm_kernel(l_ref, x_ref, g_ref, o_ref):
    del l_ref
    o_ref[...] = _rms(x_ref[...], g_ref[...]).astype(o_ref.dtype)


def _rmsnorm_rows(x, g_stack, lidx, out_dtype, row0=0, rows=None):
    d = x.shape[1]
    m = x.shape[0] if rows is None else rows
    tr = _tile(math.gcd(m, row0) if row0 else m, 320, V7X_BF16_SUBLANES)
    rb0 = row0 // tr
    blk = _nbytes((tr, d), F32) + _nbytes((tr, d), out_dtype)
    return pl.pallas_call(
        _rmsnorm_kernel,
        grid_spec=pltpu.PrefetchScalarGridSpec(
            num_scalar_prefetch=1, grid=(m // tr,),
            in_specs=[pl.BlockSpec((tr, d), lambda i, l: (rb0 + i, 0)),
                      pl.BlockSpec((None, 1, d), lambda i, l: (l[0], 0, 0))],
            out_specs=pl.BlockSpec((tr, d), lambda i, l: (i, 0))),
        out_shape=jax.ShapeDtypeStruct((m, d), out_dtype),
        compiler_params=_params(blk), name="rmsnorm_rows",
    )(lidx, x, g_stack)


def _cast_kernel(x_ref, o_ref):
    o_ref[...] = x_ref[...].astype(o_ref.dtype)


def _to_bf16(w):
    depth, kdim, n = w.shape
    tr =_tile(kdim, max(V7X_BF16_SUBLANES, (8 << 20) // (4 * n)), V7X_BF16_SUBLANES)
    return pl.pallas_call(
        _cast_kernel, grid=(depth, kdim // tr),
        in_specs=[pl.BlockSpec((None, tr, n), lambda l, i: (l, i, 0))],
        out_specs=pl.BlockSpec((None, tr, n), lambda l, i: (l, i, 0)),
        out_shape=jax.ShapeDtypeStruct((depth, kdim, n), BF16),
        compiler_params=_params(_nbytes((tr, n), F32) + _nbytes((tr, n), BF16)), name="to_bf16",
    )(w)


def _mm_kernel(l_ref, x_ref, w_ref, *rest, nk, has_res, rope):
    del l_ref
    rest = list(rest)
    r_ref = rest.pop(0) if has_res else None
    cos_ref, sin_ref = (rest.pop(0), rest.pop(0)) if rope else (None, None)
    o_ref = rest.pop(0)
    part = _dot(x_ref[...].astype(BF16), w_ref[...].astype(BF16))

    def finish(acc):
        if has_res:
            acc = acc + r_ref[...]
        if rope:
            cos, sin = cos_ref[...], sin_ref[...]
            gw = cos.shape[1]
            for g in range(acc.shape[1] // gw):
                seg = acc[:, g * gw:(g + 1) * gw]
                o_ref[:, g * gw:(g + 1) * gw] = _rope_lanes(seg, cos, sin).astype(o_ref.dtype)
        else:
            o_ref[...] = acc.astype(o_ref.dtype)

    if nk == 1:
        finish(part)
    else:
        acc_ref, = rest
        k = pl.program_id(2)

        @pl.when(k == 0)
        def _():
            acc_ref[...] = part

        @pl.when(k > 0)
        def _():
            acc_ref[...] += part

        @pl.when(k == nk - 1)
        def _():
            finish(acc_ref[...])


def _matmul(x, w_stack, lidx, *, out_dtype, tm_cap, tn_cap, tk=None, res=None, rope=None,
            x_stacked=False, name="matmul"):
    m, kdim = x.shape[-2:]
    n = w_stack.shape[2]
    tm = _tile(m, tm_cap, V7X_BF16_SUBLANES)
    tn = _tile(n, tn_cap, V7X_LANES) if n % V7X_LANES == 0 else n
    tk = kdim if tk is None else tk
    nk = kdim // tk
    if x_stacked:
        x_spec = pl.BlockSpec((None, tm, tk), lambda i, j, k, l: (l[0], i, k))
    else:
        x_spec = pl.BlockSpec((tm, tk), lambda i, j, k, l: (i, k))
    in_specs = [x_spec, pl.BlockSpec((None, tk, tn), lambda i, j, k, l: (l[0], k, j))]
    args = [x, w_stack]
    blk = _nbytes((tm, tk), x.dtype) + _nbytes((tk, tn), w_stack.dtype) + _nbytes((tm, tn), F32)
    if res is not None:
        in_specs.append(pl.BlockSpec((tm, tn), lambda i, j, k, l: (i, j)))
        args.append(res)
        blk += _nbytes((tm, tn), F32)
    if rope is not None:
        gw = rope[0].shape[1]
        assert tn % gw == 0
        in_specs += [pl.BlockSpec((tm, gw), lambda i, j, k, l: (i, 0))] * 2
        args += list(rope)
        blk += 2 * _nbytes((tm, gw), F32)
    return pl.pallas_call(
        functools.partial(_mm_kernel, nk=nk, has_res=res is not None, rope=rope is not None),
        grid_spec=pltpu.PrefetchScalarGridSpec(
            num_scalar_prefetch=1, grid=(m // tm, n // tn, nk),
            in_specs=in_specs,
            out_specs=pl.BlockSpec((tm, tn), lambda i, j, k, l: (i, j)),
            scratch_shapes=[pltpu.VMEM((tm, tn), F32)] if nk > 1 else []),
        out_shape=jax.ShapeDtypeStruct((m, n), out_dtype),
        compiler_params=_params(blk), name=name,
    )(lidx, *args)


def _kr_kernel(l_ref, xn_ref, w_ref, cos_ref, sin_ref, kr_in, kr_o, kr128_o):
    del l_ref, kr_in
    r = _rope_lanes(_dot_nt(xn_ref[...], w_ref[...]), cos_ref[...], sin_ref[...])
    kr_o[...] = r[:, :MLA_ROPE]
    kr128_o[...] = r.astype(BF16)


def _shared_rope_key(xn, w_kr_pad, cos, sin, kr_buf, lidx):
    m, d = xn.shape
    tm = _tile(m, 640, V7X_BF16_SUBLANES)
    blk = _nbytes((tm, d), BF16) + _nbytes((d, HEAD_W), BF16) + 4 * _nbytes((tm, HEAD_W), F32)
    return pl.pallas_call(
        _kr_kernel,
        grid_spec=pltpu.PrefetchScalarGridSpec(
            num_scalar_prefetch=1, grid=(m // tm,),
            in_specs=[pl.BlockSpec((tm, d), lambda i, l: (i, 0)),
                      pl.BlockSpec((None, HEAD_W, d), lambda i, l: (l[0], 0, 0)),
                      pl.BlockSpec((tm, HEAD_W), lambda i, l: (i, 0)),
                      pl.BlockSpec((tm, HEAD_W), lambda i, l: (i, 0)),
                      pl.BlockSpec(memory_space=pl.ANY)],
            out_specs=[pl.BlockSpec((None, tm, MLA_ROPE), lambda i, l: (l[0], i, 0)),
                       pl.BlockSpec((tm, HEAD_W), lambda i, l: (i, 0))]),
        out_shape=[jax.ShapeDtypeStruct(kr_buf.shape, F32), jax.ShapeDtypeStruct((m, HEAD_W), BF16)],
        input_output_aliases={5: 0},
        compiler_params=_params(blk), name="shared_rope_key",
    )(lidx, xn, w_kr_pad, cos, sin, kr_buf)


IN_SECTION_W = DIFF_HEADS * HEAD_W
IN_REST_W = COL_KR - COL_CQ
MIN_SUBTILE_ROWS = 256


def _sections_kernel(l_ref, x_ref, *refs, kinds, subtiles):
    del l_ref
    n = len(kinds)
    w_refs, rest = refs[:n], list(refs[n:])
    cos_ref, sin_ref = (rest.pop(0), rest.pop(0)) if any(rope for rope, _ in kinds) else (None, None)
    o_refs = rest[len(rest) - n:]
    sub = x_ref.shape[0] // subtiles
    for (rope, heads), w_ref, o_ref in zip(kinds, w_refs, o_refs):
        w = w_ref[...].astype(BF16)
        for r in range(subtiles):
            r0 = r * sub
            acc = _dot_nt(x_ref[r0:r0 + sub, :], w)
            for h in range(acc.shape[1] // HEAD_W):
                seg = acc[:, h * HEAD_W:(h + 1) * HEAD_W]
                if rope:
                    seg = _rope_lanes(seg, cos_ref[r0:r0 + sub, :], sin_ref[r0:r0 + sub, :])
                if heads is None:
                    o_ref[r0:r0 + sub, h * HEAD_W:(h + 1) * HEAD_W] = seg.astype(o_ref.dtype)
                elif heads[0] == 1:
                    o_ref[0, h, r0:r0 + sub, :] = seg
                else:
                    tt = heads[1]
                    for b in range(sub // tt):
                        o_ref[r0 // tt + b, h] = seg[b * tt:(b + 1) * tt, :]


def _section_proj(xn, w_in, lidx, sections, *, n_cols, row0, nb, t, tm_cap, tn, rope=None, name="in_proj"):
    d = xn.shape[1]
    rows = nb * t
    tm = _tile(math.gcd(rows, row0) if row0 else rows, tm_cap, V7X_BF16_SUBLANES)
    assert n_cols % tn == 0 and all(col0 % tn == 0 for col0, _, _ in sections)
    rb0 = row0 // tm
    subtiles = max(k for k in (1, 2, 4) if tm % (k * MIN_SUBTILE_ROWS) == 0 or k == 1)
    bpt, tt = (1, tm) if tm <= t else (tm // t, t)
    nt = t // tt

    def w_spec(col0):
        return pl.BlockSpec((None, tn, d), lambda i, j, l: (l[0], col0 // tn + j, 0))

    in_specs = [pl.BlockSpec((tm, d), lambda i, j, l: (rb0 + i, 0))] + [w_spec(col0) for col0, _, _ in sections]
    args = [xn] + [w_in] * len(sections)
    if any(use_rope for _, use_rope, _ in sections):
        in_specs += [pl.BlockSpec((tm, HEAD_W), lambda i, j, l: (rb0 + i, 0))] * 2
        args += list(rope)
    kinds, out_specs, out_shapes, aliases = [], [], [], {}
    for k, (_, use_rope, out) in enumerate(sections):
        if isinstance(out, jax.Array):
            assert (t % tm == 0) if bpt == 1 else (tm % t == 0 and (tm // subtiles) % t == 0)
            kinds.append((use_rope, (bpt, tt)))
            in_specs.append(pl.BlockSpec(memory_space=pl.ANY))
            args.append(out)
            aliases[len(args)] = k
            out_specs.append(pl.BlockSpec((None, bpt, tn // HEAD_W, tt, HEAD_W),
                                          lambda i, j, l: (l[0], i // nt, j, i % nt, 0)))
            out_shapes.append(jax.ShapeDtypeStruct(out.shape, F32))
        else:
            kinds.append((use_rope, None))
            out_specs.append(pl.BlockSpec((tm, tn), lambda i, j, l: (i, j)))
            out_shapes.append(jax.ShapeDtypeStruct((rows, n_cols), out))
    blk = (_nbytes((tm, d), BF16) + 2 * _nbytes((tm, HEAD_W), F32)
           + len(sections) * (_nbytes((d, tn), w_in.dtype) + 2 * _nbytes((tm, tn), F32)))
    return pl.pallas_call(
        functools.partial(_sections_kernel, kinds=tuple(kinds), subtiles=subtiles),
        grid_spec=pltpu.PrefetchScalarGridSpec(
            num_scalar_prefetch=1, grid=(rows // tm, n_cols // tn), in_specs=in_specs, out_specs=out_specs),
        out_shape=out_shapes, input_output_aliases=aliases,
        compiler_params=_params(blk), name=name,
    )(lidx, *args)


def _in_proj_heads(xn, w_in, cos, sin, bufs, lidx, *, row0, nb, t):
    common = dict(row0=row0, nb=nb, t=t, tm_cap=1024, tn=2 * HEAD_W, n_cols=IN_SECTION_W, rope=(cos, sin))
    sec = lambda k: k * IN_SECTION_W
    qa, dk, qb = _section_proj(xn, w_in, lidx, [(sec(0), True, BF16), (sec(1), True, bufs[0]),
                                                (sec(3), False, BF16)], **common)
    dv, sk, sv = _section_proj(xn, w_in, lidx, [(sec(2), False, bufs[1]), (sec(4), False, bufs[2]),
                                                (sec(5), False, bufs[3])], **common)
    return qa, qb, dk, dv, sk, sv


def _post_mla_kernel(l_ref, cq_ref, ckv_ref, gq_ref, gkv_ref, lat_in, cqn_o, ckvb_o, lat_o):
    del l_ref, lat_in
    cqn_o[...] = _rms(cq_ref[...], gq_ref[...]).astype(BF16)
    c = _rms(ckv_ref[...], gkv_ref[...])
    lat_o[...] = c
    ckvb_o[...] = c.astype(BF16)


def _post_mla(proj, gq, gkv, lat_buf, lidx):
    m = proj.shape[0]
    tt = _tile(m, 640, V7X_BF16_SUBLANES)
    blk = 3 * _nbytes((tt, MLA_Q_LORA + MLA_KV_LORA), F32)
    return pl.pallas_call(
        _post_mla_kernel,
        grid_spec=pltpu.PrefetchScalarGridSpec(
            num_scalar_prefetch=1, grid=(m // tt,),
            in_specs=[pl.BlockSpec((tt, MLA_Q_LORA), lambda i, l: (i, 0)),
                      pl.BlockSpec((tt, MLA_KV_LORA), lambda i, l: (i, MLA_Q_LORA // MLA_KV_LORA)),
                      pl.BlockSpec((None, 1, MLA_Q_LORA), lambda i, l: (l[0], 0, 0)),
                      pl.BlockSpec((None, 1, MLA_KV_LORA), lambda i, l: (l[0], 0, 0)),
                      pl.BlockSpec(memory_space=pl.ANY)],
            out_specs=[pl.BlockSpec((tt, MLA_Q_LORA), lambda i, l: (i, 0)),
                       pl.BlockSpec((tt, MLA_KV_LORA), lambda i, l: (i, 0)),
                       pl.BlockSpec((None, tt, MLA_KV_LORA), lambda i, l: (l[0], i, 0))]),
        out_shape=[jax.ShapeDtypeStruct((m, MLA_Q_LORA), BF16),
                   jax.ShapeDtypeStruct((m, MLA_KV_LORA), BF16),
                   jax.ShapeDtypeStruct(lat_buf.shape, F32)],
        input_output_aliases={5: 2},
        compiler_params=_params(blk), name="post_mla",
    )(lidx, proj, proj, gq, gkv, lat_buf)


LOG2E = math.log2(math.e)
SB_DEAD_LOG = -104.0


def _with_ones(v):
    return jnp.concatenate([v, jnp.ones(v.shape, v.dtype)], axis=1)


def _softmax_step(s2, v_ext, m_ref, acc_ref):
    m_prev = m_ref[...]
    m_new = jnp.maximum(m_prev, jnp.max(s2, axis=-1, keepdims=True))
    alpha = jnp.exp2(m_prev - m_new)
    lanes = m_prev.shape[1]
    p = jnp.concatenate([jnp.exp2(s2[:, c * lanes:(c + 1) * lanes] - m_new).astype(BF16)
                         for c in range(s2.shape[1] // lanes)], axis=1)
    acc_ref[...] = jnp.concatenate([alpha, alpha], axis=1) * acc_ref[...] + _dot(p, v_ext)
    m_ref[...] = m_new


def _softmax_init(m_ref, acc_ref):
    m_ref[...] = jnp.full(m_ref.shape, NEG_INF, F32)
    acc_ref[...] = jnp.zeros(acc_ref.shape, F32)


def _softmax_result(acc_ref):
    acc = acc_ref[...]
    return acc[:, :HEAD_W] / acc[:, HEAD_W:]


def _positions(shape, q0, k0):
    qpos = q0 + lax.broadcasted_iota(jnp.int32, shape, 0)
    kpos = k0 + lax.broadcasted_iota(jnp.int32, shape, 1)
    return qpos, kpos


def _chunk_mask(shape, q0, k0):
    qpos, kpos = _positions(shape, q0, k0)
    return (kpos // CHUNK) <= (qpos // CHUNK)


def _diff_lambda(lam_ref, li_ref):
    lv = lam_ref[...]
    lam_init = li_ref[:, 0:1]
    d1 = jnp.sum(lv[0:1, :] * lv[1:2, :], axis=-1, keepdims=True)
    d2 = jnp.sum(lv[2:3, :] * lv[3:4, :], axis=-1, keepdims=True)
    return jnp.exp(d1) - jnp.exp(d2) + lam_init, lam_init


def _split_components(q):
    lane = lax.broadcasted_iota(jnp.int32, q.shape, 1)
    zero = jnp.zeros_like(q)
    return jnp.where(lane < DIFF_DH, q, zero), jnp.where(lane >= DIFF_DH, q, zero)


def _softplus(z):
    return jnp.maximum(z, 0.0) + jnp.log1p(jnp.exp(-jnp.abs(z)))


def _upper_ones(n):
    r = lax.broadcasted_iota(jnp.int32, (n, n), 0)
    c = lax.broadcasted_iota(jnp.int32, (n, n), 1)
    return jnp.where(r > c, 1.0, 0.0).astype(BF16)


def _sb_block(z, mask, carry, v, tri):
    sp = _softplus(z)
    log_stay = -sp if mask is None else jnp.where(mask, -sp, 0.0)
    hi = log_stay.astype(BF16)
    lo = (log_stay - hi.astype(F32)).astype(BF16)
    between = _dot(hi, tri) + _dot(lo, tri) + carry
    a = jnp.exp(z - sp + between)
    if mask is not None:
        a = jnp.where(mask, a, 0.0)
    return _dot(a.astype(BF16), v), carry + jnp.sum(log_stay, axis=-1, keepdims=True)


def _attn_tiles(t):
    tq = _tile(t, 512, CHUNK)
    return tq, tq


def _diff_prompt_kernel(l_ref, q_ref, k_ref, v_ref, lam_ref, g_ref, li_ref, bias_ref, o_ref, m_ref, acc_ref, s_ref,
                        *, tq, tk, hp):
    del l_ref
    assert tq == tk
    qi = pl.program_id(2)
    scale2 = DIFF_DH ** -0.5 * LOG2E
    for c in range(2 * hp):
        _softmax_init(m_ref.at[c], acc_ref.at[c])

    def rows(j):
        return pl.ds(pl.multiple_of(j * tk, tk), tk)

    def store_logits(j, hh):
        kb = k_ref[hh, rows(j), :].astype(BF16)
        q1, q2 = _split_components(q_ref[:, hh * HEAD_W:(hh + 1) * HEAD_W])
        s_ref[2 * hh] = _dot_nt(q1, kb) * scale2
        s_ref[2 * hh + 1] = _dot_nt(q2, kb) * scale2

    def step(j, hh, bias):
        v_ext = _with_ones(v_ref[hh, rows(j), :].astype(BF16))
        for c in (2 * hh, 2 * hh + 1):
            s2 = s_ref[c] if bias is None else s_ref[c] + bias
            _softmax_step(s2, v_ext, m_ref.at[c], acc_ref.at[c])

    for hh in range(hp):
        store_logits(0, hh)

    def body(j, carry):
        for hh in range(hp):
            step(j, hh, None)
            store_logits(j + 1, hh)
        return carry

    lax.fori_loop(0, qi, body, 0)
    lam, lam_init = _diff_lambda(lam_ref, li_ref)
    for hh in range(hp):
        step(qi, hh, bias_ref[...])
        o = _softmax_result(acc_ref.at[2 * hh]) - lam * _softmax_result(acc_ref.at[2 * hh + 1])
        o_ref[:, hh * HEAD_W:(hh + 1) * HEAD_W] = (_rms(o, g_ref[...]) * (1.0 - lam_init)).astype(o_ref.dtype)


def _diag_bias(tq, tk):
    r = jnp.arange(tq, dtype=jnp.int32)[:, None] // CHUNK
    c = jnp.arange(tk, dtype=jnp.int32)[None, :] // CHUNK
    return jnp.where(c <= r, 0.0, NEG_INF).astype(F32)


def _diff_prompt(q, k_buf, v_buf, lam, g, li, lidx, *, nb, t, rows_total):
    tq, tk = _attn_tiles(t)
    nq = t // tq
    hp = 2
    kv_spec = pl.BlockSpec((None, None, hp, t, HEAD_W), lambda b, h, i, l: (l[0], b, h, 0, 0))
    blk = 2 * hp * _nbytes((t, HEAD_W), F32) + (1 + 10 * hp) * _nbytes((tq, tk), F32)
    return pl.pallas_call(
        functools.partial(_diff_prompt_kernel, tq=tq, tk=tk, hp=hp),
        grid_spec=pltpu.PrefetchScalarGridSpec(
            num_scalar_prefetch=1, grid=(nb, DIFF_HEADS // hp, nq),
            in_specs=[pl.BlockSpec((tq, hp * HEAD_W), lambda b, h, i, l: (b * nq + i, h)),
                      kv_spec, kv_spec,
                      pl.BlockSpec((None, 4, DIFF_DH), lambda b, h, i, l: (l[0], 0, 0)),
                      pl.BlockSpec((None, 1, HEAD_W), lambda b, h, i, l: (l[0], 0, 0)),
                      pl.BlockSpec((None, 1, HEAD_W), lambda b, h, i, l: (l[0], 0, 0)),
                      pl.BlockSpec((tq, tk), lambda b, h, i, l: (0, 0))],
            out_specs=pl.BlockSpec((tq, hp * HEAD_W), lambda b, h, i, l: (b * nq + i, h)),
            scratch_shapes=[pltpu.VMEM((2 * hp, tq, HEAD_W), F32), pltpu.VMEM((2 * hp, tq, 2 * HEAD_W), F32),
                            pltpu.VMEM((2 * hp, tq, tk), F32)]),
        out_shape=jax.ShapeDtypeStruct((rows_total, DIFF_HEADS * HEAD_W), BF16),
        compiler_params=_params(blk), name="diff_prompt",
    )(lidx, q, k_buf, v_buf, lam, g, li, _diag_bias(tq, tk))


def _sb_prompt_kernel(l_ref, q_ref, k_ref, v_ref, o_ref, c_ref, acc_ref, *, tq, tk, hp):
    del l_ref
    qi = pl.program_id(2)
    ratio = tq // tk
    scale = HEAD_W ** -0.5
    tri = _upper_ones(tk)
    c_ref[...] = jnp.zeros(c_ref.shape, F32)
    acc_ref[...] = jnp.zeros(acc_ref.shape, F32)

    def block(j, masked):
        start = pl.multiple_of(j * tk, tk)
        mask = None
        if masked:
            qpos, kpos = _positions((tq, tk), qi * tq, start)
            mask = kpos < qpos
        for hh in range(hp):
            kb = k_ref[hh, pl.ds(start, tk), :].astype(BF16)
            vb = v_ref[hh, pl.ds(start, tk), :].astype(BF16)
            z = _dot_nt(q_ref[:, hh * HEAD_W:(hh + 1) * HEAD_W], kb) * scale
            out, carry = _sb_block(z, mask, c_ref[hh], vb, tri)
            acc_ref[hh] += out
            c_ref[hh] = carry

    for u in reversed(range(ratio)):
        block(qi * ratio + u, True)

    def alive():
        return jnp.max(c_ref[...]) > SB_DEAD_LOG

    def cond(state):
        j, live = state
        return jnp.logical_and(j >= 0, live)

    def body(state):
        j, _ = state
        block(j, False)
        return j - 1, alive()

    lax.while_loop(cond, body, (qi * ratio - 1, alive()))
    for hh in range(hp):
        o_ref[:, hh * HEAD_W:(hh + 1) * HEAD_W] = acc_ref[hh].astype(o_ref.dtype)


def _sb_prompt(q, k_buf, v_buf, lidx, *, nb, t, rows_total):
    tq, _ = _attn_tiles(t)
    tk = _tile(tq, 256, CHUNK)
    nq = t // tq
    hp = 2
    kv_spec = pl.BlockSpec((None, None, hp, t, HEAD_W), lambda b, h, i, l: (l[0], b, h, 0, 0))
    blk = 2 * hp * _nbytes((t, HEAD_W), F32) + 16 * hp * _nbytes((tq, tk), F32)
    return pl.pallas_call(
        functools.partial(_sb_prompt_kernel, tq=tq, tk=tk, hp=hp),
        grid_spec=pltpu.PrefetchScalarGridSpec(
            num_scalar_prefetch=1, grid=(nb, SB_HEADS // hp, nq),
            in_specs=[pl.BlockSpec((tq, hp * HEAD_W), lambda b, h, i, l: (b * nq + i, h)), kv_spec, kv_spec],
            out_specs=pl.BlockSpec((tq, hp * HEAD_W), lambda b, h, i, l: (b * nq + i, h)),
            scratch_shapes=[pltpu.VMEM((hp, tq, 1), F32), pltpu.VMEM((hp, tq, HEAD_W), F32)]),
        out_shape=jax.ShapeDtypeStruct((rows_total, SB_HEADS * HEAD_W), BF16),
        compiler_params=_params(blk), name="sb_prompt",
    )(lidx, q, k_buf, v_buf)


def _mla_prompt_kernel(q_ref, kv_ref, kr_ref, bias_ref, o_ref, m_ref, acc_ref, s_ref, *, tq, tk, hp):
    assert tq == tk
    qi = pl.program_id(2)
    scale2 = (HEAD_W + MLA_ROPE) ** -0.5 * LOG2E
    for hh in range(hp):
        _softmax_init(m_ref.at[hh], acc_ref.at[hh])

    def rows(j):
        return pl.ds(pl.multiple_of(j * tk, tk), tk)

    def logits(j, hh):
        kcat = jnp.concatenate([kv_ref[rows(j), 2 * hh * HEAD_W:(2 * hh + 1) * HEAD_W], kr_ref[rows(j), :]], axis=1)
        return _dot_nt(q_ref[:, hh * MLA_QK_PAD:(hh + 1) * MLA_QK_PAD], kcat) * scale2

    def values(j, hh):
        return _with_ones(kv_ref[rows(j), (2 * hh + 1) * HEAD_W:(2 * hh + 2) * HEAD_W])

    for hh in range(hp):
        s_ref[hh] = logits(0, hh)

    def body(j, c):
        for hh in range(hp):
            _softmax_step(s_ref[hh], values(j, hh), m_ref.at[hh], acc_ref.at[hh])
            s_ref[hh] = logits(j + 1, hh)
        return c

    lax.fori_loop(0, qi, body, 0)
    for hh in range(hp):
        _softmax_step(s_ref[hh] + bias_ref[...], values(qi, hh), m_ref.at[hh], acc_ref.at[hh])
        o_ref[:, hh * HEAD_W:(hh + 1) * HEAD_W] = _softmax_result(acc_ref.at[hh]).astype(o_ref.dtype)


def _mla_prompt(qcat, kv, kr128, *, nb, t, rows_total):
    tq, tk = _attn_tiles(t)
    nq = t // tq
    hp = 4
    blk = (2 * hp + 1) * _nbytes((t, HEAD_W), BF16) + (1 + 6 * hp) * _nbytes((tq, tk), F32)
    return pl.pallas_call(
        functools.partial(_mla_prompt_kernel, tq=tq, tk=tk, hp=hp),
        grid=(nb, MLA_HEADS // hp, nq),
        in_specs=[pl.BlockSpec((tq, hp * MLA_QK_PAD), lambda b, h, i: (b * nq + i, h)),
                  pl.BlockSpec((t, 2 * hp * HEAD_W), lambda b, h, i: (b, h)),
                  pl.BlockSpec((t, HEAD_W), lambda b, h, i: (b, 0)),
                  pl.BlockSpec((tq, tk), lambda b, h, i: (0, 0))],
        out_specs=pl.BlockSpec((tq, hp * HEAD_W), lambda b, h, i: (b * nq + i, h)),
        scratch_shapes=[pltpu.VMEM((hp, tq, HEAD_W), F32), pltpu.VMEM((hp, tq, 2 * HEAD_W), F32),
                        pltpu.VMEM((hp, tq, tk), F32)],
        out_shape=jax.ShapeDtypeStruct((rows_total, MLA_HEADS * HEAD_W), BF16),
        compiler_params=_params(blk), name="mla_prompt",
    )(qcat, kv, kr128, _diag_bias(tq, tk))


def _pad_rows(x, rows):
    return jnp.concatenate([x, jnp.zeros((rows - x.shape[0], x.shape[1]), x.dtype)], axis=0)


def _two_part_softmax(sp, sn, vp, vn):
    m = jnp.maximum(jnp.max(sp, axis=-1, keepdims=True), jnp.max(sn, axis=-1, keepdims=True))
    pp, pn = jnp.exp(sp - m), jnp.exp(sn - m)
    denom = jnp.sum(pp, axis=-1, keepdims=True) + jnp.sum(pn, axis=-1, keepdims=True)
    return (_dot(pp.astype(BF16), vp) + _dot(pn.astype(BF16), vn)) / denom


def _diff_sample_kernel(l_ref, q_ref, kp_ref, vp_ref, kn_ref, vn_ref, lam_ref, g_ref, li_ref, prev_ref, o_ref,
                        *, past, t):
    del l_ref, prev_ref
    q12 = jnp.concatenate(_split_components(q_ref[...]), axis=0)
    scale = DIFF_DH ** -0.5
    kp, vp = kp_ref[...].astype(BF16), vp_ref[...].astype(BF16)
    kn = _pad_rows(kn_ref[...], HEAD_W).astype(BF16)
    vn = _pad_rows(vn_ref[...], HEAD_W).astype(BF16)

    def chunk_mask(width, k0):
        qpos = past + lax.broadcasted_iota(jnp.int32, (2 * t, width), 0) % t
        kpos = k0 + lax.broadcasted_iota(jnp.int32, (2 * t, width), 1)
        return ((kpos // CHUNK) <= (qpos // CHUNK)) & (kpos < past + t)

    sp = jnp.where(chunk_mask(past, 0), _dot_nt(q12, kp) * scale, NEG_INF)
    sn = jnp.where(chunk_mask(HEAD_W, past), _dot_nt(q12, kn) * scale, NEG_INF)
    o12 = _two_part_softmax(sp, sn, vp, vn)
    lam, lam_init = _diff_lambda(lam_ref, li_ref)
    o = o12[:t] - lam * o12[t:]
    o_ref[...] = (_rms(o, g_ref[...]) * (1.0 - lam_init)).astype(o_ref.dtype)


def _diff_sample(q, k_cache, v_cache, k_buf, v_buf, lam, g, li, merged, lidx, *, nb, t, past, row0):
    rb0 = row0 // t
    cache_spec = pl.BlockSpec((None, None, None, past, HEAD_W), lambda b, h, l: (l[0], b, h, 0, 0))
    new_spec = pl.BlockSpec((None, None, None, t, HEAD_W), lambda b, h, l: (l[0], b, h, 0, 0))
    blk = 2 * _nbytes((past, HEAD_W), F32) + 8 * _nbytes((t, past), F32)
    return pl.pallas_call(
        functools.partial(_diff_sample_kernel, past=past, t=t),
        grid_spec=pltpu.PrefetchScalarGridSpec(
            num_scalar_prefetch=1, grid=(nb, DIFF_HEADS),
            in_specs=[pl.BlockSpec((t, HEAD_W), lambda b, h, l: (b, h)),
                      cache_spec, cache_spec, new_spec, new_spec,
                      pl.BlockSpec((None, 4, DIFF_DH), lambda b, h, l: (l[0], 0, 0)),
                      pl.BlockSpec((None, 1, HEAD_W), lambda b, h, l: (l[0], 0, 0)),
                      pl.BlockSpec((None, 1, HEAD_W), lambda b, h, l: (l[0], 0, 0)),
                      pl.BlockSpec(memory_space=pl.ANY)],
            out_specs=pl.BlockSpec((t, HEAD_W), lambda b, h, l: (rb0 + b, h))),
        out_shape=jax.ShapeDtypeStruct(merged.shape, BF16),
        input_output_aliases={9: 0},
        compiler_params=_params(blk), name="diff_sample",
    )(lidx, q, k_cache, v_cache, k_buf, v_buf, lam, g, li, merged)


def _sb_sample_kernel(l_ref, q_ref, kp_ref, vp_ref, kn_ref, vn_ref, prev_ref, o_ref, *, past, t, cw):
    del l_ref, prev_ref
    q = q_ref[...]
    scale = HEAD_W ** -0.5
    nc = past // cw
    kn = _pad_rows(kn_ref[...], cw).astype(BF16)
    vn = _pad_rows(vn_ref[...], cw).astype(BF16)
    z_past = _dot_nt(q, kp_ref[...].astype(BF16)) * scale
    z = [z_past[:, c * cw:(c + 1) * cw] for c in range(nc)] + [_dot_nt(q, kn) * scale]
    masks = []
    for c in range(nc + 1):
        qpos, kpos = _positions((t, cw), past, c * cw)
        masks.append((kpos < qpos) & (kpos < past + t))
    sp = [_softplus(zc) for zc in z]
    log_stay = jnp.concatenate([jnp.where(m, -s, 0.0) for m, s in zip(masks, sp)], axis=0)
    hi = log_stay.astype(BF16)
    lo = (log_stay - hi.astype(F32)).astype(BF16)
    cum = _dot(jnp.concatenate([hi, lo], axis=0), _upper_ones(cw))
    rows = (nc + 1) * t
    cum = cum[:rows] + cum[rows:]
    carry = jnp.zeros((t, 1), F32)
    weights = [None] * (nc + 1)
    for c in reversed(range(nc + 1)):
        logw = z[c] - sp[c] + cum[c * t:(c + 1) * t] + carry
        weights[c] = jnp.where(masks[c], jnp.exp(logw), 0.0).astype(BF16)
        carry = carry + jnp.sum(log_stay[c * t:(c + 1) * t], axis=-1, keepdims=True)
    acc = _dot(jnp.concatenate(weights[:nc], axis=1), vp_ref[...].astype(BF16)) + _dot(weights[nc], vn)
    o_ref[...] = acc.astype(o_ref.dtype)


def _sb_sample(q, k_cache, v_cache, k_buf, v_buf, merged, lidx, *, nb, t, past, row0):
    rb0 = row0 // t
    cw = _tile(past, 256, V7X_LANES)
    cache_spec = pl.BlockSpec((None, None, None, past, HEAD_W), lambda b, h, l: (l[0], b, h, 0, 0))
    new_spec = pl.BlockSpec((None, None, None, t, HEAD_W), lambda b, h, l: (l[0], b, h, 0, 0))
    blk = 2 * _nbytes((past, HEAD_W), F32) + 8 * _nbytes((t, past), F32)
    return pl.pallas_call(
        functools.partial(_sb_sample_kernel, past=past, t=t, cw=cw),
        grid_spec=pltpu.PrefetchScalarGridSpec(
            num_scalar_prefetch=1, grid=(nb, SB_HEADS),
            in_specs=[pl.BlockSpec((t, HEAD_W), lambda b, h, l: (b, h)),
                      cache_spec, cache_spec, new_spec, new_spec, pl.BlockSpec(memory_space=pl.ANY)],
            out_specs=pl.BlockSpec((t, HEAD_W), lambda b, h, l: (rb0 + b, h))),
        out_shape=jax.ShapeDtypeStruct(merged.shape, BF16),
        input_output_aliases={6: 0},
        compiler_params=_params(blk), name="sb_sample",
    )(lidx, q, k_cache, v_cache, k_buf, v_buf, merged)


def _mla_sample_kernel(l_ref, q_ref, cp_ref, krp_ref, cn_ref, krn_ref, w_ref, prev_ref, o_ref, *, past, t):
    del l_ref, prev_ref
    scale = (HEAD_W + MLA_ROPE) ** -0.5
    q = q_ref[...]
    q_abs, q_rope = [], []
    for h in range(MLA_HEADS):
        w_uk = w_ref[:, 2 * h * HEAD_W:(2 * h + 1) * HEAD_W].astype(BF16)
        q_abs.append(_dot_nt(q[:, h * MLA_QK_PAD:h * MLA_QK_PAD + HEAD_W], w_uk).astype(BF16))
        q_rope.append(q[:, h * MLA_QK_PAD + HEAD_W:(h + 1) * MLA_QK_PAD])
    q_abs = jnp.concatenate(q_abs, axis=0)
    q_rope = jnp.concatenate(q_rope, axis=0)
    rows = MLA_HEADS * t
    c_p = cp_ref[...].astype(BF16)
    c_n = _pad_rows(cn_ref[...], HEAD_W)
    kr_n = _pad_rows(krn_ref[...], HEAD_W)

    def chunk_mask(width, k0):
        qpos = past + lax.broadcasted_iota(jnp.int32, (rows, width), 0) % t
        kpos = k0 + lax.broadcasted_iota(jnp.int32, (rows, width), 1)
        return ((kpos // CHUNK) <= (qpos // CHUNK)) & (kpos < past + t)

    s_past = _dot_nt(q_abs, c_p) + _dot_nt(q_rope[:, :MLA_ROPE], krp_ref[...].astype(BF16))
    s_new = _dot_nt(q_abs, c_n) + _dot_nt(q_rope, kr_n)
    sp = jnp.where(chunk_mask(past, 0), s_past * scale, NEG_INF)
    sn = jnp.where(chunk_mask(HEAD_W, past), s_new * scale, NEG_INF)
    o_lat = _two_part_softmax(sp, sn, c_p, c_n).astype(BF16)
    for h in range(MLA_HEADS):
        w_uv = w_ref[:, (2 * h + 1) * HEAD_W:(2 * h + 2) * HEAD_W].astype(BF16)
        o_ref[:, h * HEAD_W:(h + 1) * HEAD_W] = _dot(o_lat[h * t:(h + 1) * t, :], w_uv).astype(o_ref.dtype)


def _mla_sample(qcat, lat_cache, kr_cache, ckv_new, kr_new128, w_ukv, merged, lidx, *, nb, t, past, row0):
    rb0 = row0 // t
    kv_lora = lat_cache.shape[3]
    blk = (_nbytes((past, kv_lora), F32) + _nbytes(w_ukv.shape[1:], F32)
           + 6 * _nbytes((MLA_HEADS * t, past), F32))
    return pl.pallas_call(
        functools.partial(_mla_sample_kernel, past=past, t=t),
        grid_spec=pltpu.PrefetchScalarGridSpec(
            num_scalar_prefetch=1, grid=(nb,),
            in_specs=[pl.BlockSpec((t, MLA_HEADS * MLA_QK_PAD), lambda b, l: (rb0 + b, 0)),
                      pl.BlockSpec((None, None, past, kv_lora), lambda b, l: (l[0], b, 0, 0)),
                      pl.BlockSpec((None, None, past, MLA_ROPE), lambda b, l: (l[0], b, 0, 0)),
                      pl.BlockSpec((t, kv_lora), lambda b, l: (rb0 + b, 0)),
                      pl.BlockSpec((t, HEAD_W), lambda b, l: (rb0 + b, 0)),
                      pl.BlockSpec((None,) + w_ukv.shape[1:], lambda b, l: (l[0], 0, 0)),
                      pl.BlockSpec(memory_space=pl.ANY)],
            out_specs=pl.BlockSpec((t, MLA_HEADS * HEAD_W), lambda b, l: (rb0 + b, 0))),
        out_shape=jax.ShapeDtypeStruct(merged.shape, BF16),
        input_output_aliases={7: 0},
        compiler_params=_params(blk), name="mla_sample",
    )(lidx, qcat, lat_cache, kr_cache, ckv_new, kr_new128, w_ukv, merged)


def _gate_merge_kernel(l_ref, xn_ref, oa_ref, ob_ref, oc_ref, wga_ref, wgb_ref, wgc_ref,
                       wa_ref, wb_ref, wc_ref, o_ref):
    del l_ref
    xn = xn_ref[...]

    def branch(wg_ref, mix_ref, w_ref):
        return jax.nn.sigmoid(_dot(xn, wg_ref[...])) * _dot(mix_ref[...], w_ref[...].astype(BF16))

    merged = branch(wga_ref, oa_ref, wa_ref) + branch(wgb_ref, ob_ref, wb_ref) + branch(wgc_ref, oc_ref, wc_ref)
    o_ref[...] = merged.astype(o_ref.dtype)


def _gate_merge(xn, oa, ob, oc, w_gate, w_a, w_b, w_c, lidx):
    m, d = xn.shape
    tm = _tile(m, 832, V7X_BF16_SUBLANES)
    tn = _tile(d, 256, V7X_LANES)
    nj = d // tn

    def rows(width):
        return pl.BlockSpec((tm, width), lambda i, j, l: (i, 0))

    def gate(branch):
        return pl.BlockSpec((None, d, tn), lambda i, j, l: (l[0], 0, branch * nj + j))

    def proj(kdim):
        return pl.BlockSpec((None, kdim, tn), lambda i, j, l: (l[0], 0, j))

    wa, wb, wc = oa.shape[1], ob.shape[1], oc.shape[1]
    blk = (_nbytes((tm, d + wa + wb + wc), BF16) + _nbytes((3 * d, tn), w_gate.dtype)
           + _nbytes((wa + wb + wc, tn), w_a.dtype) + 4 * _nbytes((tm, tn), F32))
    return pl.pallas_call(
        _gate_merge_kernel,
        grid_spec=pltpu.PrefetchScalarGridSpec(
            num_scalar_prefetch=1, grid=(m // tm, nj),
            in_specs=[rows(d), rows(wa), rows(wb), rows(wc), gate(0), gate(1), gate(2),
                      proj(wa), proj(wb), proj(wc)],
            out_specs=pl.BlockSpec((tm, tn), lambda i, j, l: (i, j))),
        out_shape=jax.ShapeDtypeStruct((m, d), BF16),
        compiler_params=_params(blk), name="gate_merge",
    )(lidx, xn, oa, ob, oc, w_gate, w_gate, w_gate, w_a, w_b, w_c)


def _swiglu_kernel(l_ref, x_ref, w1_ref, w3_ref, o_ref):
    del l_ref
    x = x_ref[...]
    o_ref[...] = (jax.nn.silu(_dot(x, w1_ref[...].astype(BF16)))
                  * _dot(x, w3_ref[...].astype(BF16))).astype(o_ref.dtype)


def _swiglu(xn, w1, w3, lidx):
    m, d = xn.shape
    f = w1.shape[2]
    tm = _tile(m, 1040, V7X_BF16_SUBLANES)
    tn = _tile(f, 256, V7X_LANES)
    wspec = pl.BlockSpec((None, d, tn), lambda i, j, l: (l[0], 0, j))
    blk = _nbytes((tm, d), BF16) + 3 * _nbytes((d, tn), w1.dtype) + 4 * _nbytes((tm, tn), F32)
    return pl.pallas_call(
        _swiglu_kernel,
        grid_spec=pltpu.PrefetchScalarGridSpec(
            num_scalar_prefetch=1, grid=(m // tm, f // tn),
            in_specs=[pl.BlockSpec((tm, d), lambda i, j, l: (i, 0)), wspec, wspec],
            out_specs=pl.BlockSpec((tm, tn), lambda i, j, l: (i, j))),
        out_shape=jax.ShapeDtypeStruct((m, f), BF16),
        compiler_params=_params(blk), name="swiglu",
    )(lidx, xn, w1, w3)


def _rope_tables(pos):
    inv_freq = ROPE_THETA ** (-jnp.arange(0, MLA_ROPE, 2, dtype=F32) / MLA_ROPE)
    ang = pos.astype(F32)[:, None] * inv_freq[None, :]
    c, s = jnp.cos(ang), jnp.sin(ang)
    one, zero = jnp.ones_like(c), jnp.zeros_like(c)
    cos128 = jnp.concatenate([c, c, c, c], axis=-1)
    sin128 = jnp.concatenate([-s, s, -s, s], axis=-1)
    cos256 = jnp.concatenate([one, one, one, one, c, c, one, one], axis=-1)
    sin256 = jnp.concatenate([zero, zero, zero, zero, -s, s, zero, zero], axis=-1)
    return cos128, sin128, cos256, sin256


def kernel(x_prompt, x_sample, cache_diff_k, cache_diff_v, cache_sb_k, cache_sb_v, cache_mla_latent,
           cache_mla_krope, attn_norm, w_in, diff_lambda, diff_subln, mla_q_norm, mla_w_uq, mla_kv_norm,
           mla_w_ukv, w_gate, w_branch_a, w_branch_b, w_branch_c, w_out, ffn_norm, ffn_w1, ffn_w3, ffn_w2,
           final_norm):
    nbp, tp, d = x_prompt.shape
    nbs, ts = x_sample.shape[:2]
    past = cache_diff_k.shape[3]
    depth = w_in.shape[0]
    mp, ms = nbp * tp, nbs * ts
    m = mp + ms

    h0 = jnp.concatenate([x_prompt.reshape(mp, d), x_sample.reshape(ms, d)], axis=0)
    pos = jnp.concatenate([jnp.tile(jnp.arange(tp, dtype=jnp.int32), nbp),
                           jnp.tile(past + jnp.arange(ts, dtype=jnp.int32), nbs)])
    cos128, sin128, cos256, sin256 = _rope_tables(pos)

    w_in_b = _to_bf16(jnp.swapaxes(w_in, 1, 2))
    w_kr_pad = jnp.pad(w_in_b[:, COL_KR:, :], ((0, 0), (0, HEAD_W - MLA_ROPE), (0, 0)))
    uq = mla_w_uq.astype(BF16).reshape(depth, MLA_Q_LORA, MLA_HEADS, HEAD_W + MLA_ROPE)
    w_uq_pad = jnp.pad(uq, ((0, 0), (0, 0), (0, 0), (0, MLA_QK_PAD - HEAD_W - MLA_ROPE))).reshape(
        depth, MLA_Q_LORA, MLA_HEADS * MLA_QK_PAD)
    w_gate_b, w_a, w_b, w_c = _to_bf16(w_gate), w_branch_a, w_branch_b, w_branch_c
    w2 = _to_bf16(ffn_w2)

    attn_g = attn_norm.reshape(depth, 1, d)
    ffn_g = ffn_norm.reshape(depth, 1, d)
    gq = mla_q_norm.reshape(depth, 1, MLA_Q_LORA)
    gkv = mla_kv_norm.reshape(depth, 1, MLA_KV_LORA)
    subln = diff_subln.reshape(depth, 1, HEAD_W)
    lam_init = jnp.asarray([0.8 - 0.6 * math.exp(-0.3 * l) for l in range(depth)], F32)
    lam_init = jnp.broadcast_to(lam_init[:, None, None], (depth, 1, HEAD_W))

    def head_bufs(nb, t):
        return tuple(jnp.zeros((depth, nb, DIFF_HEADS, t, HEAD_W), F32) for _ in range(4))

    def layer(l, carry):
        h, p_bufs, s_bufs, lat_buf, kr_buf = carry
        lidx = jnp.reshape(l, (1,)).astype(jnp.int32)
        xn = _rmsnorm_rows(h, attn_g, lidx, BF16)
        qa_p, qb_p, *p_bufs = _in_proj_heads(xn, w_in_b, cos128, sin128, p_bufs, lidx, row0=0, nb=nbp, t=tp)
        qa_s, qb_s, *s_bufs = _in_proj_heads(xn, w_in_b, cos128, sin128, s_bufs, lidx, row0=mp, nb=nbs, t=ts)
        rest, = _section_proj(xn, w_in_b, lidx, [(COL_CQ, False, F32)], n_cols=IN_REST_W, row0=0, nb=1, t=m,
                              tm_cap=1040, tn=4 * HEAD_W, name="in_proj_mla")
        kr_buf, kr128 = _shared_rope_key(xn, w_kr_pad, cos128, sin128, kr_buf, lidx)
        cqn, ckv_b, lat_buf = _post_mla(rest, gq, gkv, lat_buf, lidx)
        qcat = _matmul(cqn, w_uq_pad, lidx, out_dtype=BF16, tm_cap=1040, tn_cap=1024,
                       rope=(cos256, sin256), name="mla_q_up")
        kv = _matmul(ckv_b, mla_w_ukv, lidx, out_dtype=BF16, tm_cap=1040, tn_cap=1024, name="mla_kv_up")
        oa = _diff_prompt(qa_p, p_bufs[0], p_bufs[1], diff_lambda, subln, lam_init, lidx, nb=nbp, t=tp, rows_total=m)
        oa = _diff_sample(qa_s, cache_diff_k, cache_diff_v, s_bufs[0], s_bufs[1], diff_lambda, subln,
                          lam_init, oa, lidx, nb=nbs, t=ts, past=past, row0=mp)
        ob = _sb_prompt(qb_p, p_bufs[2], p_bufs[3], lidx, nb=nbp, t=tp, rows_total=m)
        ob = _sb_sample(qb_s, cache_sb_k, cache_sb_v, s_bufs[2], s_bufs[3], ob, lidx, nb=nbs, t=ts, past=past,
                        row0=mp)
        oc = _mla_prompt(qcat, kv, kr128, nb=nbp, t=tp, rows_total=m)
        oc = _mla_sample(qcat, cache_mla_latent, cache_mla_krope, ckv_b, kr128, mla_w_ukv, oc, lidx, nb=nbs, t=ts,
                         past=past, row0=mp)
        merged = _gate_merge(xn, oa, ob, oc, w_gate_b, w_a, w_b, w_c, lidx)
        h = _matmul(merged, w_out, lidx, out_dtype=F32, tm_cap=1040, tn_cap=512, res=h, name="out_proj")
        xn2 = _rmsnorm_rows(h, ffn_g, lidx, BF16)
        hid = _swiglu(xn2, ffn_w1, ffn_w3, lidx)
        h = _matmul(hid, w2, lidx, out_dtype=F32, tm_cap=640, tn_cap=256, res=h, name="ffn_down")
        return h, tuple(p_bufs), tuple(s_bufs), lat_buf, kr_buf

    carry = (h0, head_bufs(nbp, tp), head_bufs(nbs, ts),
             jnp.zeros((depth, m, MLA_KV_LORA), F32), jnp.zeros((depth, m, MLA_ROPE), F32))
    h, p_bufs, s_bufs, lat_buf, kr_buf = lax.fori_loop(0, depth, layer, carry)

    final_g, l0 = final_norm.reshape(1, 1, d), jnp.zeros((1,), jnp.int32)
    y_prompt = _rmsnorm_rows(h, final_g, l0, F32, row0=0, rows=mp).reshape(nbp, tp, d)
    y_sample = _rmsnorm_rows(h, final_g, l0, F32, row0=mp, rows=ms).reshape(nbs, ts, d)
    p_lat = lat_buf[:, :mp].reshape(depth, nbp, tp, MLA_KV_LORA)
    s_lat = lat_buf[:, mp:].reshape(depth, nbs, ts, MLA_KV_LORA)
    p_kr = kr_buf[:, :mp].reshape(depth, nbp, tp, MLA_ROPE)
    s_kr = kr_buf[:, mp:].reshape(depth, nbs, ts, MLA_ROPE)
    return (y_prompt, y_sample, *p_bufs, p_lat, p_kr, *s_bufs, s_lat, s_kr)
```
